```python
import math
import jax, jax.numpy as jnp
from jax import lax
import numpy as np

D_MODEL = 2048
BATCH = 4
SEQ = 2048
DEPTH = 4

CTX_LEN = 256
GRID_W = 64
EPS = 1e-6

MIX_W = D_MODEL
HY_W = D_MODEL // 4
MLA_HEADS = 8
MLA_DN = 128
MLA_DR = 64
MLA_DV = 128
MLA_W = MLA_HEADS * MLA_DV
Q_RANK = D_MODEL // 4
KV_RANK = D_MODEL // 8
POOL_W = D_MODEL // 4
POOL_WINDOWS = (2, 4, 8, 16)
N_POOL_GROUPS = len(POOL_WINDOWS)
POOL_GROUP = POOL_W // N_POOL_GROUPS

SHORT_CONV = 3
FILTER_EMB = 33
FILTER_BANDS = (FILTER_EMB - 1) // 2
FILTER_HIDDEN = 64
DECAY_TARGET = 1e-2
FAST_DECAY = 0.3
SLOW_DECAY = 1.5

ROPE_BASE = 10000.0
Q_BLOCK = 128
ATTN_SCALE = (MLA_DN + MLA_DR) ** -0.5

OFF_HY = 0
OFF_HY_G = OFF_HY + 3 * HY_W
OFF_Q = OFF_HY_G + HY_W
OFF_KV = OFF_Q + Q_RANK
OFF_KR = OFF_KV + KV_RANK
OFF_MLA_G = OFF_KR + MLA_DR
OFF_POOL = OFF_MLA_G + MLA_W
OFF_POOL_G = OFF_POOL + POOL_W
N_IN = OFF_POOL_G + POOL_W

kernel_name = "hybrid_hyena_mla_pool_prefix_dit"


def rmsnorm(x, g):
    xf = x.astype(jnp.float32)
    y = xf * lax.rsqrt(jnp.mean(xf * xf, axis=-1, keepdims=True) + EPS)
    return (y * g.astype(jnp.float32)).astype(x.dtype)


def adaln(cond, w_ada, b_ada):
    mod = jax.nn.silu(cond) @ w_ada + b_ada
    shift, scale, gate = jnp.split(mod[:, None, :], 3, axis=-1)
    return shift, scale, gate


def rope_tables(L):
    n_rows = L // GRID_W
    row = jnp.broadcast_to(jnp.arange(n_rows, dtype=jnp.float32)[:, None], (n_rows, GRID_W)).reshape(L)
    col = jnp.broadcast_to(jnp.arange(GRID_W, dtype=jnp.float32)[None, :], (n_rows, GRID_W)).reshape(L)
    n_freq = MLA_DR // 4
    inv = ROPE_BASE ** (-jnp.arange(n_freq, dtype=jnp.float32) / n_freq)
    ang = jnp.concatenate([row[:, None] * inv, col[:, None] * inv], axis=-1)
    return jnp.cos(ang), jnp.sin(ang)


def apply_rope(x, cos, sin):
    xf = x.astype(jnp.float32).reshape(*x.shape[:-1], -1, 2)
    x0, x1 = xf[..., 0], xf[..., 1]
    out = jnp.stack([x0 * cos - x1 * sin, x0 * sin + x1 * cos], axis=-1)
    return out.reshape(x.shape).astype(x.dtype)


def short_conv(u, w, b):
    L = u.shape[1]
    up = jnp.pad(u, ((0, 0), (1, 1), (0, 0)))
    return up[:, :L] * w[0] + up[:, 1:L + 1] * w[1] + up[:, 2:] * w[2] + b


def hyena_spectrum(L, w1, b1, freq, w2, b2, w3):
    f32 = jnp.float32
    t = jnp.linspace(0.0, 1.0, L, dtype=f32)[:, None]
    wpos = (2.0 * math.pi / L) * jnp.arange(L, dtype=f32)[:, None]
    bands = jnp.linspace(1e-4, FILTER_BANDS - 1, FILTER_BANDS, dtype=f32)[None, :]
    feats = jnp.concatenate([t, jnp.cos(bands * wpos), -jnp.sin(bands * wpos)], axis=-1)
    h = jnp.sin(freq * (feats @ w1 + b1))
    h = jnp.sin(freq * (h @ w2 + b2))
    h = (h @ w3).astype(f32).reshape(L, 2, 2, HY_W)
    deltas = jnp.abs(jnp.linspace(math.log(DECAY_TARGET) / SLOW_DECAY,
                                  math.log(DECAY_TARGET) / FAST_DECAY, HY_W, dtype=f32))
    h = h * jnp.exp(-t[:, :, None, None] * deltas)
    k = jnp.concatenate([h[:, :, 0], jnp.zeros_like(h[:1, :, 0]), h[:0:-1, :, 1]], axis=0)
    k = k * lax.rsqrt(jnp.sum(k * k, axis=0, keepdims=True) + EPS)
    return jnp.fft.rfft(k, axis=0)


def fft_long_conv(z, k_f, bias):
    L = z.shape[1]
    zf = z.astype(jnp.float32)
    y = jnp.fft.irfft(jnp.fft.rfft(zf, n=2 * L, axis=1) * k_f, n=2 * L, axis=1)[:, :L]
    return (y + zf * bias.astype(jnp.float32)).astype(z.dtype)


def hyena_mix(u, lp):
    L = u.shape[1]
    u = short_conv(u, lp["hy_conv_w"], lp["hy_conv_b"])
    v, x1, x2 = jnp.split(u, 3, axis=-1)
    k_f = hyena_spectrum(L, lp["hf_w1"], lp["hf_b1"], lp["hf_freq"], lp["hf_w2"], lp["hf_b2"], lp["hf_w3"])
    z = x1 * fft_long_conv(v, k_f[:, 0], lp["hy_bias"][0])
    return x2 * fft_long_conv(z, k_f[:, 1], lp["hy_bias"][1])


def mla_queries(u_q, q_norm_g, w_uq, rope):
    B, L, _ = u_q.shape
    q = (rmsnorm(u_q, q_norm_g) @ w_uq).reshape(B, L, MLA_HEADS, MLA_DN + MLA_DR)
    q_nope, q_rope = q[..., :MLA_DN], q[..., MLA_DN:]
    if rope is not None:
        q_rope = apply_rope(q_rope, rope[0][:, None, :], rope[1][:, None, :])
    return q_nope, q_rope


def mla_keys(u_kvr, kv_norm_g, w_ukv, rope):
    B, L, _ = u_kvr.shape
    kv = (rmsnorm(u_kvr[..., :KV_RANK], kv_norm_g) @ w_ukv).reshape(B, L, MLA_HEADS, MLA_DN + MLA_DV)
    k_rope = u_kvr[..., KV_RANK:]
    if rope is not None:
        k_rope = apply_rope(k_rope, rope[0], rope[1])
    return kv[..., :MLA_DN], k_rope, kv[..., MLA_DN:]


def attend(q_nope, q_rope, k_nope, k_rope, v):
    s = (jnp.einsum('bqhd,bkhd->bhqk', q_nope, k_nope)
         + jnp.einsum('bqhr,bkr->bhqk', q_rope, k_rope))
    p = jax.nn.softmax(s.astype(jnp.float32) * ATTN_SCALE, axis=-1).astype(v.dtype)
    return jnp.einsum('bhqk,bkhd->bqhd', p, v)


def blocked_attend(q_nope, q_rope, k_nope, k_rope, v):
    B, L, H, _ = q_nope.shape
    nb = L // Q_BLOCK
    qn = q_nope.reshape(B, nb, Q_BLOCK, H, MLA_DN).swapaxes(0, 1)
    qr = q_rope.reshape(B, nb, Q_BLOCK, H, MLA_DR).swapaxes(0, 1)
    o = lax.map(lambda qb: attend(qb[0], qb[1], k_nope, k_rope, v), (qn, qr))
    return o.swapaxes(0, 1).reshape(B, L, H * MLA_DV)


def pool_mix(u, w_pool, pool_scale):
    B, L, _ = u.shape
    uf = u.astype(jnp.float32).reshape(B, L, N_POOL_GROUPS, POOL_GROUP)
    cs = jnp.pad(jnp.cumsum(uf, axis=1), ((0, 0), (1, 0), (0, 0), (0, 0)))
    t = jnp.arange(L)[:, None]
    halves = jnp.asarray(np.array([w // 2 for w in POOL_WINDOWS], dtype=np.int32))[None, :]
    lo = jnp.clip(t - halves, 0, L)
    hi = jnp.clip(t + halves, 0, L)
    gid = jnp.arange(N_POOL_GROUPS)[None, :]
    mean = (cs[:, hi, gid] - cs[:, lo, gid]) / (hi - lo).astype(jnp.float32)[None, :, :, None]
    d = (mean - uf).astype(u.dtype)
    y = jnp.einsum('blgc,gcd->blgd', d, w_pool).reshape(B, L, POOL_W)
    return y * pool_scale


def mixer(u, keys, lp, rope, blocked):
    B, L, _ = u.shape
    hy = hyena_mix(u[..., OFF_HY:OFF_HY_G], lp) * jax.nn.silu(u[..., OFF_HY_G:OFF_Q])
    q_nope, q_rope = mla_queries(u[..., OFF_Q:OFF_KV], lp["q_norm_g"], lp["w_uq"], rope)
    if blocked:
        att = blocked_attend(q_nope, q_rope, *keys)
    else:
        att = attend(q_nope, q_rope, *keys).reshape(B, L, MLA_W)
    mla = att * jax.nn.silu(u[..., OFF_MLA_G:OFF_POOL])
    pool = pool_mix(u[..., OFF_POOL:OFF_POOL_G], lp["w_pool"], lp["pool_scale"]) * jax.nn.silu(u[..., OFF_POOL_G:])
    return jnp.concatenate([hy, mla, pool], axis=-1)


def setup_inputs(seed: int = 0) -> dict:
    key = jax.random.key(seed)
    ks = jax.random.split(key, 25)

    def nrm(k, shape, scale):
        return jax.random.normal(k, shape, jnp.float32) * scale

    return {
        "x": nrm(ks[0], (BATCH, SEQ, D_MODEL), 1.0),
        "c": nrm(ks[1], (BATCH, D_MODEL), 1.0),
        "ctx": nrm(ks[2], (BATCH, CTX_LEN, D_MODEL), 1.0),
        "c_ctx": nrm(ks[3], (D_MODEL,), 1.0),
        "norm_g": 1.0 + nrm(ks[4], (DEPTH, D_MODEL), 0.02),
        "w_ada": nrm(ks[5], (DEPTH, D_MODEL, 3 * D_MODEL), 0.5 * D_MODEL ** -0.5),
        "b_ada": nrm(ks[6], (DEPTH, 3 * D_MODEL), 0.02),
        "w_in": nrm(ks[7], (DEPTH, D_MODEL, N_IN), D_MODEL ** -0.5),
        "hy_conv_w": nrm(ks[8], (DEPTH, SHORT_CONV, 3 * HY_W), SHORT_CONV ** -0.5),
        "hy_conv_b": nrm(ks[9], (DEPTH, 3 * HY_W), 0.02),
        "hf_w1": nrm(ks[10], (DEPTH, FILTER_EMB, FILTER_HIDDEN), FILTER_EMB ** -0.5),
        "hf_b1": nrm(ks[11], (DEPTH, FILTER_HIDDEN), 0.02),
        "hf_freq": 1.0 + nrm(ks[12], (DEPTH, FILTER_HIDDEN), 0.02),
        "hf_w2": nrm(ks[13], (DEPTH, FILTER_HIDDEN, FILTER_HIDDEN), FILTER_HIDDEN ** -0.5),
        "hf_b2": nrm(ks[14], (DEPTH, FILTER_HIDDEN), 0.02),
        "hf_w3": nrm(ks[15], (DEPTH, FILTER_HIDDEN, 4 * HY_W), FILTER_HIDDEN ** -0.5),
        "hy_bias": nrm(ks[16], (DEPTH, 2, HY_W), 0.1),
        "q_norm_g": 1.0 + nrm(ks[17], (DEPTH, Q_RANK), 0.02),
        "w_uq": nrm(ks[18], (DEPTH, Q_RANK, MLA_HEADS * (MLA_DN + MLA_DR)), Q_RANK ** -0.5),
        "kv_norm_g": 1.0 + nrm(ks[19], (DEPTH, KV_RANK), 0.02),
        "w_ukv": nrm(ks[20], (DEPTH, KV_RANK, MLA_HEADS * (MLA_DN + MLA_DV)), KV_RANK ** -0.5),
        "w_pool": nrm(ks[21], (DEPTH, N_POOL_GROUPS, POOL_GROUP, POOL_GROUP), POOL_GROUP ** -0.5),
        "pool_scale": 1.0 + nrm(ks[22], (DEPTH, POOL_W), 0.1),
        "w_out": nrm(ks[23], (DEPTH, MIX_W, D_MODEL), MIX_W ** -0.5),
        "final_norm_g": 1.0 + nrm(ks[24], (D_MODEL,), 0.02),
    }


def reference(x, c, ctx, c_ctx, norm_g, w_ada, b_ada, w_in, hy_conv_w, hy_conv_b,
              hf_w1, hf_b1, hf_freq, hf_w2, hf_b2, hf_w3, hy_bias,
              q_norm_g, w_uq, kv_norm_g, w_ukv, w_pool, pool_scale, w_out, final_norm_g):
    rope = rope_tables(x.shape[1])
    x_l, x_c = x, ctx
    for l in range(DEPTH):
        last = l == DEPTH - 1
        lp = {"hy_conv_w": hy_conv_w[l], "hy_conv_b": hy_conv_b[l],
              "hf_w1": hf_w1[l], "hf_b1": hf_b1[l], "hf_freq": hf_freq[l],
              "hf_w2": hf_w2[l], "hf_b2": hf_b2[l], "hf_w3": hf_w3[l], "hy_bias": hy_bias[l],
              "q_norm_g": q_norm_g[l], "w_uq": w_uq[l],
              "w_pool": w_pool[l], "pool_scale": pool_scale[l]}
        sh, sc, gt = adaln(c, w_ada[l], b_ada[l])
        sh_c, sc_c, gt_c = adaln(c_ctx[None], w_ada[l], b_ada[l])
        h_l = rmsnorm(x_l, norm_g[l]) * (1.0 + sc) + sh
        h_c = rmsnorm(x_c, norm_g[l]) * (1.0 + sc_c) + sh_c
        u_l = h_l @ w_in[l]
        if last:
            u_c = h_c @ w_in[l][:, OFF_KV:OFF_MLA_G]
            u_c_kvr = u_c
        else:
            u_c = h_c @ w_in[l]
            u_c_kvr = u_c[..., OFF_KV:OFF_MLA_G]
        k_c = mla_keys(u_c_kvr, kv_norm_g[l], w_ukv[l], None)
        k_l = mla_keys(u_l[..., OFF_KV:OFF_MLA_G], kv_norm_g[l], w_ukv[l], rope)
        keys_all = (jnp.concatenate([k_l[0], k_c[0]], axis=1),
                    jnp.concatenate([k_l[1], k_c[1]], axis=1),
                    jnp.concatenate([k_l[2], k_c[2]], axis=1))
        out_l = mixer(u_l, keys_all, lp, rope, True)
        if not last:
            out_c = mixer(u_c, k_c, lp, None, False)
            x_c = x_c + gt_c * (out_c @ w_out[l])
        x_l = x_l + gt * (out_l @ w_out[l])
    return rmsnorm(x_l, final_norm_g)
```

```python
import functools
import math

import jax
import jax.numpy as jnp
from jax import lax
from jax.experimental import pallas as pl
from jax.experimental.pallas import tpu as pltpu

F32 = jnp.float32
BF16 = jnp.bfloat16

D_MODEL = 2048
BATCH = 4
SEQ = 2048
DEPTH = 4
CTX_LEN = 256
GRID_W = 64
EPS = 1e-6

HY_W = 512
MLA_HEADS = 8
MLA_DN = 128
MLA_DR = 64
MLA_DV = 128
MLA_W = MLA_HEADS * MLA_DV
Q_RANK = 512
KV_RANK = 256
POOL_W = 512
POOL_WINDOWS = (2, 4, 8, 16)
POOL_GROUP = 128

FILTER_EMB = 33
FILTER_BANDS = 16
FILTER_HIDDEN = 64
DECAY_TARGET = 1e-2
FAST_DECAY = 0.3
SLOW_DECAY = 1.5
ROPE_BASE = 10000.0
ATTN_SCALE = (MLA_DN + MLA_DR) ** -0.5

R_OFF_KR = 2816
R_OFF_MLA_G = 2880

U_HY = 0
U_Q = 2048
U_KV = 2560
U_KR1 = 2816
U_KR2 = 2944
U_MLA_G = 3072
U_POOL = 4096
U_POOL_G = 4608
U_W = 5120

ROWS_L = BATCH * SEQ
ROWS_C = BATCH * CTX_LEN
ROWS_ALL = ROWS_L + ROWS_C

VMEM_LIMIT_BYTES = 56 * 1024 * 1024
LANE = 128


def _cparams(sem):
    return pltpu.CompilerParams(dimension_semantics=sem, vmem_limit_bytes=VMEM_LIMIT_BYTES)


def _silu(x):
    return x * jax.nn.sigmoid(x)


def _dot(a, b):
    return jnp.dot(a, b, preferred_element_type=F32)


def _dot_nt(a, b):
    return lax.dot_general(a, b, (((1,), (1,)), ((), ())), preferred_element_type=F32)


def _adaln_kernel(c_ref, w_ref, b_ref, o_ref):
    a = _silu(c_ref[...]).astype(BF16)
    o_ref[...] = _dot(a, w_ref[...].astype(BF16)) + b_ref[...]


def _adaln(cond, w_ada, b_ada, tn=1024):
    depth, d, n = w_ada.shape
    return pl.pallas_call(
        _adaln_kernel,
        grid=(depth, n // tn),
        in_specs=[
            pl.BlockSpec((8, d), lambda l, j: (0, 0)),
            pl.BlockSpec((None, d, tn), lambda l, j: (l, 0, j)),
            pl.BlockSpec((None, 1, tn), lambda l, j: (l, 0, j)),
        ],
        out_specs=pl.BlockSpec((None, 8, tn), lambda l, j: (l, 0, j)),
        out_shape=jax.ShapeDtypeStruct((depth, 8, n), F32),
        compiler_params=_cparams(("parallel", "parallel")),
        name="adaln",
    )(cond, w_ada, b_ada)


def _inproj_kernel(x_ref, g_ref, sh_ref, sc_ref, w_ref, o_ref, h_ref, *, sub):
    @pl.when(pl.program_id(1) == 0)
    def _():
        g = g_ref[...]
        sc = 1.0 + sc_ref[...]
        sh = sh_ref[...]
        for r in range(0, x_ref.shape[0], sub):
            x = x_ref[pl.ds(r, sub), :]
            ms = jnp.mean(x * x, axis=-1, keepdims=True)
            y = x * lax.rsqrt(ms + EPS) * g
            h_ref[pl.ds(r, sub), :] = (y * sc + sh).astype(BF16)

    o_ref[...] = _dot(h_ref[...], w_ref[...]).astype(o_ref.dtype)


def _inproj(x_all, g, mod, w, *, n_row_tiles, tm=1024, tn=1280):
    d = x_all.shape[1]
    n = w.shape[1]
    tiles_per_batch = SEQ // tm

    def mod_row(i):
        return jnp.minimum(i // tiles_per_batch, BATCH)

    return pl.pallas_call(
        functools.partial(_inproj_kernel, sub=256),
        grid=(n_row_tiles, n // tn),
        in_specs=[
            pl.BlockSpec((tm, d), lambda i, j: (i, 0)),
            pl.BlockSpec((1, d), lambda i, j: (0, 0)),
            pl.BlockSpec((None, 1, d), lambda i, j: (mod_row(i), 0, 0)),
            pl.BlockSpec((None, 1, d), lambda i, j: (mod_row(i), 0, 1)),
            pl.BlockSpec((d, tn), lambda i, j: (0, j)),
        ],
        out_specs=pl.BlockSpec((tm, tn), lambda i, j: (i, j)),
        out_shape=jax.ShapeDtypeStruct((x_all.shape[0], n), BF16),
        scratch_shapes=[pltpu.VMEM((tm, d), BF16)],
        compiler_params=_cparams(("parallel", "arbitrary")),
        name="inproj",
    )(x_all, g, mod, mod, w)


def _filter_kernel(feats_ref, t_ref, dl_ref, w1_ref, b1_ref, fr_ref, w2_ref, b2_ref, w3_ref,
                   c_ref, s_ref, kr_ref, ki_ref, kn_ref, *, L):
    hp = lax.Precision.HIGHEST
    n_fft = 2 * L
    fr = fr_ref[...]
    h = jnp.sin(fr * (jnp.dot(feats_ref[...], w1_ref[...], precision=hp, preferred_element_type=F32)
                      + b1_ref[...]))
    h = jnp.sin(fr * (jnp.dot(h, w2_ref[...], precision=hp, preferred_element_type=F32)
                      + b2_ref[...]))
    decay = jnp.exp(-t_ref[...] * dl_ref[...])
    row = lax.broadcasted_iota(jnp.int32, (L, 1), 0)
    sign = jnp.where(row % 2 == 0, 1.0, -1.0).astype(F32)
    wgt = jnp.where(row == 0, 1.0 / n_fft, 2.0 / n_fft).astype(F32)
    for o in range(2):
        hf = jnp.dot(h, w3_ref[o, 0], precision=hp, preferred_element_type=F32) * decay
        hb = jnp.dot(h, w3_ref[o, 1], precision=hp, preferred_element_type=F32) * decay
        hb = jnp.where(row == 0, 0.0, hb)
        nrm = lax.rsqrt(jnp.sum(hf * hf + hb * hb, axis=0, keepdims=True) + EPS)
        a = (hf + hb) * nrm
        dd = (hf - hb) * nrm
        kr_ref[o] = (_dot(c_ref[...], a.astype(BF16)) * wgt).astype(BF16)
        ki_ref[o] = (_dot(s_ref[...], dd.astype(BF16)) * wgt).astype(BF16)
        kn_ref[o] = jnp.sum(a * sign, axis=0, keepdims=True) * (1.0 / n_fft)


def _hyena_filter(feats, t, deltas, w1, b1, fr, w2, b2, w3, cmat, smat, *, cb=256):
    L = cmat.shape[0]
    full2 = lambda j: (0, 0)
    return pl.pallas_call(
        functools.partial(_filter_kernel, L=L),
        grid=(HY_W // cb,),
        in_specs=[
            pl.BlockSpec(feats.shape, full2),
            pl.BlockSpec(t.shape, full2),
            pl.BlockSpec((1, cb), lambda j: (0, j)),
            pl.BlockSpec(w1.shape, full2),
            pl.BlockSpec(b1.shape, full2),
            pl.BlockSpec(fr.shape, full2),
            pl.BlockSpec(w2.shape, full2),
            pl.BlockSpec(b2.shape, full2),
            pl.BlockSpec((2, 2, FILTER_HIDDEN, cb), lambda j: (0, 0, 0, j)),
            pl.BlockSpec((L, L), full2, pipeline_mode=pl.Buffered(1)),
            pl.BlockSpec((L, L), full2, pipeline_mode=pl.Buffered(1)),
        ],
        out_specs=[
            pl.BlockSpec((2, L, cb), lambda j: (0, 0, j)),
            pl.BlockSpec((2, L, cb), lambda j: (0, 0, j)),
            pl.BlockSpec((2, 1, cb), lambda j: (0, 0, j)),
        ],
        out_shape=[
            jax.ShapeDtypeStruct((2, L, HY_W), BF16),
            jax.ShapeDtypeStruct((2, L, HY_W), BF16),
            jax.ShapeDtypeStruct((2, 1, HY_W), F32),
        ],
        compiler_params=_cparams(("parallel",)),
        name=f"hyena_filter_{L}",
    )(feats, t, deltas, w1, b1, fr, w2, b2, w3, cmat, smat)


_PAD = 8


def _hyena_kernel(v_ref, x1_ref, x2_ref, g_ref, cw_ref, cb_ref, hb_ref, kr_ref, ki_ref, kn_ref,
                  c_ref, s_ref, o_ref, pad_ref, z_ref, zb_ref, xc_ref, yr_ref, yi_ref, *, L, RC):
    cb = z_ref.shape[1]
    zeros = jnp.zeros((_PAD, cb), F32)
    pad_ref[pl.ds(0, _PAD), :] = zeros
    pad_ref[pl.ds(L + _PAD, _PAD), :] = zeros

    def short_conv(src_ref, p):
        pad_ref[pl.ds(_PAD, L), :] = src_ref[...].astype(F32)
        w = cw_ref[:, p, :]
        return (pad_ref[pl.ds(_PAD - 1, L), :] * w[0:1] + pad_ref[pl.ds(_PAD, L), :] * w[1:2]
                + pad_ref[pl.ds(_PAD + 1, L), :] * w[2:3] + cb_ref[p:p + 1, :])

    sign_l = jnp.where(lax.broadcasted_iota(jnp.int32, (L, 1), 0) % 2 == 0, 1.0, -1.0).astype(F32)
    sign_c = sign_l[0:RC]

    z_ref[...] = short_conv(v_ref, 0)
    for o in range(2):
        xc_ref[...] = short_conv(x1_ref if o == 0 else x2_ref, 1 + o)
        z = z_ref[...]
        zb_ref[...] = z.astype(BF16)
        yn = kn_ref[o] * jnp.sum(z * sign_l, axis=0, keepdims=True)
        bias = hb_ref[o:o + 1, :]

        def fwd(i, carry):
            r = pl.multiple_of(i * RC, RC)
            zr = _dot(c_ref[pl.ds(r, RC), :], zb_ref[...])
            zi = _dot(s_ref[pl.ds(r, RC), :], zb_ref[...])
            kr = kr_ref[o, pl.ds(r, RC), :].astype(F32)
            ki = ki_ref[o, pl.ds(r, RC), :].astype(F32)
            yr_ref[pl.ds(r, RC), :] = (kr * zr - ki * zi).astype(BF16)
            yi_ref[pl.ds(r, RC), :] = (kr * zi + ki * zr).astype(BF16)
            return carry

        lax.fori_loop(0, L // RC, fwd, 0)

        def inv(i, carry):
            r = pl.multiple_of(i * RC, RC)
            y = _dot(c_ref[pl.ds(r, RC), :], yr_ref[...]) + _dot(s_ref[pl.ds(r, RC), :], yi_ref[...])
            y = y + sign_c * yn + z_ref[pl.ds(r, RC), :] * bias
            y = xc_ref[pl.ds(r, RC), :] * y
            if o == 0:
                z_ref[pl.ds(r, RC), :] = y
            else:
                o_ref[pl.ds(r, RC), :] = (y * _silu(g_ref[pl.ds(r, RC), :].astype(F32))).astype(o_ref.dtype)
            return carry

        lax.fori_loop(0, L // RC, inv, 0)


def _hyena(u_all, prev, conv_w, conv_b, hy_bias, kr, ki, kn, cmat, smat, *, L, row_blk0, cb=256):
    RC = min(L, 512)
    ncb = HY_W // cb

    def ublk(part):
        return pl.BlockSpec((L, cb), lambda j, b: (row_blk0 + b, part * ncb + j))

    in_specs = [
        ublk(0), ublk(1), ublk(2), ublk(3),
        pl.BlockSpec((3, 3, cb), lambda j, b: (0, 0, j)),
        pl.BlockSpec((3, cb), lambda j, b: (0, j)),
        pl.BlockSpec((2, cb), lambda j, b: (0, j)),
        pl.BlockSpec((2, L, cb), lambda j, b: (0, 0, j)),
        pl.BlockSpec((2, L, cb), lambda j, b: (0, 0, j)),
        pl.BlockSpec((2, 1, cb), lambda j, b: (0, 0, j)),
        pl.BlockSpec((L, L), lambda j, b: (0, 0), pipeline_mode=pl.Buffered(1)),
        pl.BlockSpec((L, L), lambda j, b: (0, 0), pipeline_mode=pl.Buffered(1)),
    ]
    args = [u_all, u_all, u_all, u_all, conv_w, conv_b, hy_bias, kr, ki, kn, cmat, smat]
    kern = functools.partial(_hyena_kernel, L=L, RC=RC)
    aliases = {}
    if prev is not None:
        in_specs.append(pl.BlockSpec(memory_space=pl.ANY))
        args.append(prev)
        aliases = {len(args) - 1: 0}
        kern = _drop_arg(kern, len(args) - 1)
    return pl.pallas_call(
        kern,
        grid=(ncb, BATCH),
        in_specs=in_specs,
        out_specs=pl.BlockSpec((L, cb), lambda j, b: (row_blk0 + b, j)),
        out_shape=jax.ShapeDtypeStruct((ROWS_ALL, HY_W), BF16),
        scratch_shapes=[
            pltpu.VMEM((L + 2 * _PAD, cb), F32),
            pltpu.VMEM((L, cb), F32),
            pltpu.VMEM((L, cb), BF16),
            pltpu.VMEM((L, cb), F32),
            pltpu.VMEM((L, cb), BF16),
            pltpu.VMEM((L, cb), BF16),
        ],
        input_output_aliases=aliases,
        compiler_params=_cparams(("parallel", "parallel")),
        name=f"hyena_{L}",
    )(*args)


def _drop_arg(kern, idx):
    def wrapped(*refs):
        return kern(*refs[:idx], *refs[idx + 1:])
    return wrapped


def _qkv_kernel(uq_ref, ukv_ref, k1_ref, k2_ref, qg_ref, kg_ref, wq_ref, wkv_ref, cos_ref, sin_ref,
                q_ref, k_ref, v_ref):
    def rms(x, g):
        x = x.astype(F32)
        return (x * lax.rsqrt(jnp.mean(x * x, axis=-1, keepdims=True) + EPS) * g).astype(BF16)

    cos = cos_ref[...]
    sin = sin_ref[...]
    lane = lax.broadcasted_iota(jnp.int32, (1, LANE), 1)
    half_mask = [(lane < MLA_DR), (lane >= MLA_DR)]

    qa = _dot(rms(uq_ref[...], qg_ref[...]), wq_ref[...])
    kv = _dot(rms(ukv_ref[...], kg_ref[...]), wkv_ref[...])
    k_rope = (k1_ref[...].astype(F32) * cos + k2_ref[...].astype(F32) * sin).astype(BF16)
    n_rot = MLA_HEADS * MLA_DR
    for h in range(MLA_HEADS):
        c = h // 2
        qr = qa[:, MLA_W + c * LANE:MLA_W + (c + 1) * LANE]
        qs = qa[:, MLA_W + n_rot + c * LANE:MLA_W + n_rot + (c + 1) * LANE]
        rot = jnp.where(half_mask[h % 2], qr * cos + qs * sin, 0.0)
        q_ref[h, :, 0:LANE] = (qa[:, h * LANE:(h + 1) * LANE] * ATTN_SCALE).astype(BF16)
        q_ref[h, :, LANE:2 * LANE] = (rot * ATTN_SCALE).astype(BF16)
        k_ref[h, :, 0:LANE] = kv[:, h * LANE:(h + 1) * LANE].astype(BF16)
        k_ref[h, :, LANE:2 * LANE] = k_rope
        v_ref[h] = kv[:, MLA_W + h * LANE:MLA_W + (h + 1) * LANE].astype(BF16)


def _qkv(u_all, qg, kg, wq, wkv, cos_t, sin_t, *, tr=512):
    n_lat = ROWS_L // tr
    per_seq = SEQ // tr

    def tab(i):
        return (jnp.where(i < n_lat, i % per_seq, per_seq), 0)

    full = lambda i: (0, 0)
    return pl.pallas_call(
        _qkv_kernel,
        grid=(ROWS_ALL // tr,),
        in_specs=[
            pl.BlockSpec((tr, Q_RANK), lambda i: (i, U_Q // Q_RANK)),
            pl.BlockSpec((tr, KV_RANK), lambda i: (i, U_KV // KV_RANK)),
            pl.BlockSpec((tr, LANE), lambda i: (i, U_KR1 // LANE)),
            pl.BlockSpec((tr, LANE), lambda i: (i, U_KR2 // LANE)),
            pl.BlockSpec((1, Q_RANK), full),
            pl.BlockSpec((1, KV_RANK), full),
            pl.BlockSpec(wq.shape, full),
            pl.BlockSpec(wkv.shape, full),
            pl.BlockSpec((tr, LANE), tab),
            pl.BlockSpec((tr, LANE), tab),
        ],
        out_specs=[
            pl.BlockSpec((MLA_HEADS, tr, 2 * LANE), lambda i: (0, i, 0)),
            pl.BlockSpec((MLA_HEADS, tr, 2 * LANE), lambda i: (0, i, 0)),
            pl.BlockSpec((MLA_HEADS, tr, MLA_DV), lambda i: (0, i, 0)),
        ],
        out_shape=[
            jax.ShapeDtypeStruct((MLA_HEADS, ROWS_ALL, 2 * LANE), BF16),
            jax.ShapeDtypeStruct((MLA_HEADS, ROWS_ALL, 2 * LANE), BF16),
            jax.ShapeDtypeStruct((MLA_HEADS, ROWS_ALL, MLA_DV), BF16),
        ],
        compiler_params=_cparams(("parallel",)),
        name="qkv",
    )(u_all, u_all, u_all, u_all, qg, kg, wq, wkv, cos_t, sin_t)


def _attn_kernel(*refs, n_src):
    q_ref = refs[0]
    k_refs = refs[1:1 + n_src]
    v_refs = refs[1 + n_src:1 + 2 * n_src]
    g_ref = refs[1 + 2 * n_src]
    o_ref = refs[2 + 2 * n_src]
    for h in range(MLA_HEADS):
        q = q_ref[h]
        s = [_dot_nt(q, k[h]) for k in k_refs]
        m = s[0].max(axis=-1, keepdims=True)
        for si in s[1:]:
            m = jnp.maximum(m, si.max(axis=-1, keepdims=True))
        p = [jnp.exp(si - m) for si in s]
        l = p[0].sum(axis=-1, keepdims=True)
        for pi in p[1:]:
            l = l + pi.sum(axis=-1, keepdims=True)
        acc = _dot(p[0].astype(BF16), v_refs[0][h])
        for pi, v in zip(p[1:], v_refs[1:]):
            acc = acc + _dot(pi.astype(BF16), v[h])
        gate = _silu(g_ref[:, h * LANE:(h + 1) * LANE].astype(F32))
        o_ref[:, h * LANE:(h + 1) * LANE] = (acc / l * gate).astype(o_ref.dtype)


def _attention(q, k, v, u_all, prev, *, latent, tq=256):
    H = MLA_HEADS
    if latent:
        n_q = SEQ // tq
        q_blk = lambda b, i: b * n_q + i
        k_specs = [pl.BlockSpec((H, SEQ, 2 * LANE), lambda b, i: (0, b, 0)),
                   pl.BlockSpec((H, CTX_LEN, 2 * LANE), lambda b, i: (0, ROWS_L // CTX_LEN + b, 0))]
        v_specs = [pl.BlockSpec((H, SEQ, MLA_DV), lambda b, i: (0, b, 0)),
                   pl.BlockSpec((H, CTX_LEN, MLA_DV), lambda b, i: (0, ROWS_L // CTX_LEN + b, 0))]
    else:
        n_q = CTX_LEN // tq
        q_blk = lambda b, i: ROWS_L // tq + b * n_q + i
        k_specs = [pl.BlockSpec((H, CTX_LEN, 2 * LANE), lambda b, i: (0, ROWS_L // CTX_LEN + b, 0))]
        v_specs = [pl.BlockSpec((H, CTX_LEN, MLA_DV), lambda b, i: (0, ROWS_L // CTX_LEN + b, 0))]
    n_src = len(k_specs)
    in_specs = ([pl.BlockSpec((H, tq, 2 * LANE), lambda b, i: (0, q_blk(b, i), 0))] + k_specs + v_specs
                + [pl.BlockSpec((tq, MLA_W), lambda b, i: (q_blk(b, i), U_MLA_G // MLA_W))])
    args = [q] + [k] * n_src + [v] * n_src + [u_all]
    kern = functools.partial(_attn_kernel, n_src=n_src)
    aliases = {}
    if prev is not None:
        in_specs.append(pl.BlockSpec(memory_space=pl.ANY))
        args.append(prev)
        aliases = {len(args) - 1: 0}
        kern = _drop_arg(kern, len(args) - 1)
    return pl.pallas_call(
        kern,
        grid=(BATCH, n_q),
        in_specs=in_specs,
        out_specs=pl.BlockSpec((tq, MLA_W), lambda b, i: (q_blk(b, i), 0)),
        out_shape=jax.ShapeDtypeStruct((ROWS_ALL, MLA_W), BF16),
        input_output_aliases=aliases,
        compiler_params=_cparams(("parallel", "parallel")),
        name="attn_latent" if latent else "attn_ctx",
    )(*args)


_POOL_PAD = 16


def _pool_kernel(x_ref, g_ref, w_ref, sc_ref, o_ref, pad_ref, *, L):
    zeros = jnp.zeros((_POOL_PAD, POOL_GROUP), F32)
    pad_ref[pl.ds(0, _POOL_PAD), :] = zeros
    pad_ref[pl.ds(L + _POOL_PAD, _POOL_PAD), :] = zeros
    t = lax.broadcasted_iota(jnp.int32, (L, 1), 0)
    for gi, win in enumerate(POOL_WINDOWS):
        half = win // 2
        cols = slice(gi * POOL_GROUP, (gi + 1) * POOL_GROUP)
        x = x_ref[:, cols].astype(F32)
        pad_ref[pl.ds(_POOL_PAD, L), :] = x
        acc = pad_ref[pl.ds(_POOL_PAD - half, L), :]
        for j in range(-half + 1, half):
            acc = acc + pad_ref[pl.ds(_POOL_PAD + j, L), :]
        cnt = (jnp.minimum(t + half, L) - jnp.maximum(t - half, 0)).astype(F32)
        dlt = (acc / cnt - x).astype(BF16)
        y = _dot(dlt, w_ref[gi]) * sc_ref[:, cols]
        o_ref[:, cols] = (y * _silu(g_ref[:, cols].astype(F32))).astype(o_ref.dtype)


def _pool(u_all, prev, w_pool, pool_scale, *, L, row_blk0):
    in_specs = [
        pl.BlockSpec((L, POOL_W), lambda b: (row_blk0 + b, U_POOL // POOL_W)),
        pl.BlockSpec((L, POOL_W), lambda b: (row_blk0 + b, U_POOL_G // POOL_W)),
        pl.BlockSpec(w_pool.shape, lambda b: (0, 0, 0)),
        pl.BlockSpec((1, POOL_W), lambda b: (0, 0)),
    ]
    args = [u_all, u_all, w_pool, pool_scale]
    kern = functools.partial(_pool_kernel, L=L)
    aliases = {}
    if prev is not None:
        in_specs.append(pl.BlockSpec(memory_space=pl.ANY))
        args.append(prev)
        aliases = {len(args) - 1: 0}
        kern = _drop_arg(kern, len(args) - 1)
    return pl.pallas_call(
        kern,
        grid=(BATCH,),
        in_specs=in_specs,
        out_specs=pl.BlockSpec((L, POOL_W), lambda b: (row_blk0 + b, 0)),
        out_shape=jax.ShapeDtypeStruct((ROWS_ALL, POOL_W), BF16),
        scratch_shapes=[pltpu.VMEM((L + 2 * _POOL_PAD, POOL_GROUP), F32)],
        input_output_aliases=aliases,
        compiler_params=_cparams(("parallel",)),
        name=f"pool_{L}",
    )(*args)


def _outproj_kernel(x_ref, hy_ref, at_ref, po_ref, w1_ref, w2_ref, w3_ref, gt_ref, fg_ref, o_ref, *, final):
    acc = _dot(hy_ref[...], w1_ref[...]) + _dot(at_ref[...], w2_ref[...]) + _dot(po_ref[...], w3_ref[...])
    y = x_ref[...] + gt_ref[...] * acc
    if final:
        y = y * lax.rsqrt(jnp.mean(y * y, axis=-1, keepdims=True) + EPS) * fg_ref[...]
    o_ref[...] = y


def _outproj(x_all, hy, att, po, w_hy, w_at, w_po, mod, fg, *, n_row_tiles, final, tm=512):
    d = x_all.shape[1]
    tiles_per_batch = SEQ // tm

    def mod_row(i):
        return jnp.minimum(i // tiles_per_batch, BATCH)

    full = lambda i: (0, 0)
    return pl.pallas_call(
        functools.partial(_outproj_kernel, final=final),
        grid=(n_row_tiles,),
        in_specs=[
            pl.BlockSpec((tm, d), lambda i: (i, 0)),
            pl.BlockSpec((tm, HY_W), lambda i: (i, 0)),
            pl.BlockSpec((tm, MLA_W), lambda i: (i, 0)),
            pl.BlockSpec((tm, POOL_W), lambda i: (i, 0)),
            pl.BlockSpec(w_hy.shape, full, pipeline_mode=pl.Buffered(1)),
            pl.BlockSpec(w_at.shape, full, pipeline_mode=pl.Buffered(1)),
            pl.BlockSpec(w_po.shape, full, pipeline_mode=pl.Buffered(1)),
            pl.BlockSpec((None, 1, d), lambda i: (mod_row(i), 0, 2)),
            pl.BlockSpec((1, d), full),
        ],
        out_specs=pl.BlockSpec((tm, d), lambda i: (i, 0)),
        out_shape=jax.ShapeDtypeStruct((n_row_tiles * tm, d), F32),
        compiler_params=_cparams(("parallel",)),
        name="outproj_final" if final else "outproj",
    )(x_all, hy, att, po, w_hy, w_at, w_po, mod, fg)


def _rope_tables(L):
    n_rows = L // GRID_W
    row = jnp.broadcast_to(jnp.arange(n_rows, dtype=F32)[:, None], (n_rows, GRID_W)).reshape(L)
    col = jnp.broadcast_to(jnp.arange(GRID_W, dtype=F32)[None, :], (n_rows, GRID_W)).reshape(L)
    n_freq = MLA_DR // 4
    inv = ROPE_BASE ** (-jnp.arange(n_freq, dtype=F32) / n_freq)
    ang = jnp.concatenate([row[:, None] * inv, col[:, None] * inv], axis=-1)
    return jnp.cos(ang), jnp.sin(ang)


def _rope_lane_tables(ident_rows):
    cos, sin = _rope_tables(SEQ)
    cos_t = jnp.concatenate([cos, cos, cos, cos], axis=-1)
    sin_t = jnp.concatenate([-sin, sin, -sin, sin], axis=-1)
    cos_t = jnp.concatenate([cos_t, jnp.ones((ident_rows, LANE), F32)], axis=0)
    sin_t = jnp.concatenate([sin_t, jnp.zeros((ident_rows, LANE), F32)], axis=0)
    return cos_t, sin_t


def _dft_tables(L):
    n_fft = 2 * L
    idx = jnp.arange(L, dtype=jnp.int32)
    m = (idx[:, None] * idx[None, :]) % n_fft
    ang = m.astype(F32) * (2.0 * math.pi / n_fft)
    return jnp.cos(ang).astype(BF16), jnp.sin(ang).astype(BF16)


def _filter_tables(L):
    t = jnp.linspace(0.0, 1.0, L, dtype=F32)[:, None]
    wpos = (2.0 * math.pi / L) * jnp.arange(L, dtype=F32)[:, None]
    bands = jnp.linspace(1e-4, FILTER_BANDS - 1, FILTER_BANDS, dtype=F32)[None, :]
    feats = jnp.concatenate([t, jnp.cos(bands * wpos), -jnp.sin(bands * wpos)], axis=-1)
    feats = jnp.pad(feats, ((0, 0), (0, LANE - FILTER_EMB)))
    return feats, t


def _decay_rates():
    return jnp.abs(jnp.linspace(math.log(DECAY_TARGET) / SLOW_DECAY,
                                math.log(DECAY_TARGET) / FAST_DECAY, HY_W, dtype=F32))[None, :]


def _pack_w_in(w_in):
    kr = w_in[:, :, R_OFF_KR:R_OFF_MLA_G]
    a, b = kr[..., 0::2], kr[..., 1::2]
    return jnp.concatenate([w_in[:, :, :R_OFF_KR], a, b, a, b, b, a, b, a, w_in[:, :, R_OFF_MLA_G:]],
                           axis=-1).astype(BF16)


def _pack_w_uq(w_uq):
    w = w_uq.reshape(DEPTH, Q_RANK, MLA_HEADS, MLA_DN + MLA_DR)
    nope = w[..., :MLA_DN].reshape(DEPTH, Q_RANK, MLA_W)
    a, b = w[..., MLA_DN::2], w[..., MLA_DN + 1::2]
    rot = jnp.concatenate([a, b], axis=-1).reshape(DEPTH, Q_RANK, MLA_HEADS * MLA_DR)
    swp = jnp.concatenate([b, a], axis=-1).reshape(DEPTH, Q_RANK, MLA_HEADS * MLA_DR)
    return jnp.concatenate([nope, rot, swp], axis=-1).astype(BF16)


def _pack_w_ukv(w_ukv):
    w = w_ukv.reshape(DEPTH, KV_RANK, MLA_HEADS, MLA_DN + MLA_DV)
    return jnp.concatenate([w[..., :MLA_DN].reshape(DEPTH, KV_RANK, MLA_W),
                            w[..., MLA_DN:].reshape(DEPTH, KV_RANK, MLA_W)], axis=-1).astype(BF16)


def kernel(x, c, ctx, c_ctx, norm_g, w_ada, b_ada, w_in, hy_conv_w, hy_conv_b, hf_w1, hf_b1, hf_freq,
           hf_w2, hf_b2, hf_w3, hy_bias, q_norm_g, w_uq, kv_norm_g, w_ukv, w_pool, pool_scale, w_out,
           final_norm_g):
    assert x.shape == (BATCH, SEQ, D_MODEL) and ctx.shape == (BATCH, CTX_LEN, D_MODEL)

    qkv_tr = 512
    cos_t, sin_t = _rope_lane_tables(qkv_tr)
    dft = {L: _dft_tables(L) for L in (SEQ, CTX_LEN)}
    ftab = {L: _filter_tables(L) for L in (SEQ, CTX_LEN)}
    deltas = _decay_rates()

    w_in_p = _pack_w_in(w_in)
    w_q_p = _pack_w_uq(w_uq)
    w_kv_p = _pack_w_ukv(w_ukv)
    w_out_b = w_out.astype(BF16)
    w_pool_b = w_pool.astype(BF16)
    w1_p = jnp.pad(hf_w1, ((0, 0), (0, LANE - FILTER_EMB), (0, 0)))
    w3_p = hf_w3.reshape(DEPTH, FILTER_HIDDEN, 2, 2, HY_W).transpose(0, 2, 3, 1, 4)
    conv_w = hy_conv_w.reshape(DEPTH, 3, 3, HY_W)
    conv_b = hy_conv_b.reshape(DEPTH, 3, HY_W)

    cond = jnp.concatenate([c, c_ctx[None], jnp.zeros((8 - BATCH - 1, D_MODEL), F32)], axis=0)
    mod = _adaln(cond, w_ada, b_ada[:, None, :])
    mod = mod.reshape(DEPTH, 8, 1, 3 * D_MODEL)

    x_all = jnp.concatenate([x.reshape(ROWS_L, D_MODEL), ctx.reshape(ROWS_C, D_MODEL)], axis=0)
    row2 = lambda a: a[None, :]

    for l in range(DEPTH):
        last = l == DEPTH - 1
        u_all = _inproj(x_all, row2(norm_g[l]), mod[l], w_in_p[l], n_row_tiles=ROWS_ALL // 1024)
        q, k, v = _qkv(u_all, row2(q_norm_g[l]), row2(kv_norm_g[l]), w_q_p[l], w_kv_p[l], cos_t, sin_t,
                       tr=qkv_tr)
        hy = att = po = None
        for L, row_blk0, latent in ((SEQ, 0, True), (CTX_LEN, ROWS_L // CTX_LEN, False)):
            if last and not latent:
                continue
            cmat, smat = dft[L]
            feats, t = ftab[L]
            kr, ki, kn = _hyena_filter(feats, t, deltas, w1_p[l], row2(hf_b1[l]), row2(hf_freq[l]),
                                       hf_w2[l], row2(hf_b2[l]), w3_p[l], cmat, smat)
            hy = _hyena(u_all, hy, conv_w[l], conv_b[l], hy_bias[l], kr, ki, kn, cmat, smat,
                        L=L, row_blk0=row_blk0)
            att = _attention(q, k, v, u_all, att, latent=latent, tq=256)
            po = _pool(u_all, po, w_pool_b[l], row2(pool_scale[l]), L=L, row_blk0=row_blk0)
        n_tiles = (ROWS_L if last else ROWS_ALL) // 512
        x_all = _outproj(x_all, hy, att, po, w_out_b[l, :HY_W], w_out_b[l, HY_W:HY_W + MLA_W],
                         w_out_b[l, HY_W + MLA_W:], mod[l], row2(final_norm_g),
                         n_row_tiles=n_tiles, final=last)
    return x_all.reshape(BATCH, SEQ, D_MODEL)
```

```python
import functools
import math

import jax
import jax.numpy as jnp
import numpy as np
from jax import lax
from jax.experimental import pallas as pl
from jax.experimental.pallas import tpu as pltpu

F32 = jnp.float32
BF16 = jnp.bfloat16

D_MODEL = 2048
BATCH = 4
SEQ = 2048
DEPTH = 4
CTX_LEN = 256
GRID_W = 64
EPS = 1e-6

HY_W = 512
MLA_HEADS = 8
MLA_DN = 128
MLA_DR = 64
MLA_DV = 128
MLA_W = MLA_HEADS * MLA_DV
Q_RANK = 512
KV_RANK = 256
POOL_W = 512
POOL_WINDOWS = (2, 4, 8, 16)
POOL_GROUP = 128

FILTER_EMB = 33
FILTER_BANDS = 16
FILTER_HIDDEN = 64
DECAY_TARGET = 1e-2
FAST_DECAY = 0.3
SLOW_DECAY = 1.5
ROPE_BASE = 10000.0
ATTN_SCALE = (MLA_DN + MLA_DR) ** -0.5

R_OFF_KR = 2816
R_OFF_MLA_G = 2880

U_HY = 0
U_Q = 2048
U_KV = 2560
U_KR1 = 2816
U_KR2 = 2944
U_MLA_G = 3072
U_POOL = 4096
U_POOL_G = 4608
U_W = 5120

ROWS_L = BATCH * SEQ
ROWS_C = BATCH * CTX_LEN
ROWS_ALL = ROWS_L + ROWS_C

VMEM_LIMIT_BYTES = 56 * 1024 * 1024
LANE = 128


def _cparams(sem):
    return pltpu.CompilerParams(dimension_semantics=sem, vmem_limit_bytes=VMEM_LIMIT_BYTES)


def _silu(x):
    return x * jax.nn.sigmoid(x)


def _dot(a, b):
    return jnp.dot(a, b, preferred_element_type=F32)


def _dot_nt(a, b):
    return lax.dot_general(a, b, (((1,), (1,)), ((), ())), preferred_element_type=F32)


def _adaln_kernel(c_ref, w_ref, b_ref, o_ref):
    a = _silu(c_ref[...]).astype(BF16)
    o_ref[...] = _dot(a, w_ref[...].astype(BF16)) + b_ref[...]


def _adaln(cond, w_ada, b_ada, tn=1024):
    depth, d, n = w_ada.shape
    return pl.pallas_call(
        _adaln_kernel,
        grid=(depth, n // tn),
        in_specs=[
            pl.BlockSpec((8, d), lambda l, j: (0, 0)),
            pl.BlockSpec((None, d, tn), lambda l, j: (l, 0, j)),
            pl.BlockSpec((None, 1, tn), lambda l, j: (l, 0, j)),
        ],
        out_specs=pl.BlockSpec((None, 8, tn), lambda l, j: (l, 0, j)),
        out_shape=jax.ShapeDtypeStruct((depth, 8, n), F32),
        compiler_params=_cparams(("parallel", "parallel")),
        name="adaln",
    )(cond, w_ada, b_ada)


def _inproj_kernel(x_ref, g_ref, sh_ref, sc_ref, w_ref, o_ref, h_ref, *, sub):
    @pl.when(pl.program_id(1) == 0)
    def _():
        g = g_ref[...]
        sc = 1.0 + sc_ref[...]
        sh = sh_ref[...]
        for r in range(0, x_ref.shape[0], sub):
            x = x_ref[pl.ds(r, sub), :]
            ms = jnp.mean(x * x, axis=-1, keepdims=True)
            y = x * lax.rsqrt(ms + EPS) * g
            h_ref[pl.ds(r, sub), :] = (y * sc + sh).astype(BF16)

    o_ref[...] = _dot(h_ref[...], w_ref[...]).astype(o_ref.dtype)


def _inproj(x_all, g, mod, w, *, n_row_tiles, tm=1024, tn=1280):
    d = x_all.shape[1]
    n = w.shape[1]
    tiles_per_batch = SEQ // tm

    def mod_row(i):
        return jnp.minimum(i // tiles_per_batch, BATCH)

    return pl.pallas_call(
        functools.partial(_inproj_kernel, sub=256),
        grid=(n_row_tiles, n // tn),
        in_specs=[
            pl.BlockSpec((tm, d), lambda i, j: (i, 0)),
            pl.BlockSpec((1, d), lambda i, j: (0, 0)),
            pl.BlockSpec((None, 1, d), lambda i, j: (mod_row(i), 0, 0)),
            pl.BlockSpec((None, 1, d), lambda i, j: (mod_row(i), 0, 1)),
            pl.BlockSpec((d, tn), lambda i, j: (0, j)),
        ],
        out_specs=pl.BlockSpec((tm, tn), lambda i, j: (i, j)),
        out_shape=jax.ShapeDtypeStruct((x_all.shape[0], n), BF16),
        scratch_shapes=[pltpu.VMEM((tm, d), BF16)],
        compiler_params=_cparams(("parallel", "arbitrary")),
        name="inproj",
    )(x_all, g, mod, mod, w)


def _hyena_block(L):
    return min(L, 512)


def _filter_kernel(feats_ref, t_ref, dl_ref, w1_ref, b1_ref, fr_ref, w2_ref, b2_ref, w3_ref,
                   c_ref, s_ref, g_ref, gn_ref, *, L, P):
    hp = lax.Precision.HIGHEST
    hdot = lambda a, b: jnp.dot(a, b, precision=hp, preferred_element_type=F32)
    nb = L // P
    m_fft = 2 * P
    fr = fr_ref[...]
    h = jnp.sin(fr * (hdot(feats_ref[...], w1_ref[...]) + b1_ref[...]))
    h = jnp.sin(fr * (hdot(h, w2_ref[...]) + b2_ref[...]))
    decay = jnp.exp(-t_ref[...] * dl_ref[...])
    row = lax.broadcasted_iota(jnp.int32, (L, 1), 0)
    rp = lax.broadcasted_iota(jnp.int32, (P, 1), 0)
    sgn = jnp.where(rp % 2 == 0, 1.0, -1.0).astype(F32)
    wgt = jnp.where(rp == 0, 1.0 / m_fft, 2.0 / m_fft).astype(F32)
    cm = c_ref[...]
    sm = s_ref[...]
    hf = hdot(h, w3_ref[0]) * decay
    hb = hdot(h, w3_ref[1]) * decay
    hb = jnp.where(row == 0, 0.0, hb)
    nrm = lax.rsqrt(jnp.sum(hf * hf + hb * hb, axis=0, keepdims=True) + EPS)
    F, B = [], []
    for arr, out in ((hf * nrm, F), (hb * nrm, B)):
        for j in range(nb):
            blk = arr[j * P:(j + 1) * P]
            out.append((hdot(cm, blk), hdot(sm, blk), jnp.sum(blk * sgn, axis=0, keepdims=True), blk[0:1]))
    for d in range(-(nb - 1), nb):
        if d >= 1:
            gr = F[d][0] + sgn * (F[d - 1][0] - F[d - 1][3])
            gi = F[d][1] + sgn * F[d - 1][1]
            gn = F[d][2] + F[d - 1][2] - F[d - 1][3]
        elif d == 0:
            gr = F[0][0] + B[0][0]
            gi = F[0][1] - B[0][1]
            gn = F[0][2] + B[0][2]
        else:
            e = -d
            gr = B[e][0] + sgn * (B[e - 1][0] - B[e - 1][3])
            gi = -B[e][1] - sgn * B[e - 1][1]
            gn = B[e][2] + B[e - 1][2] - B[e - 1][3]
        g_ref[d + nb - 1, 0] = gr * wgt
        g_ref[d + nb - 1, 1] = gi * wgt
        gn_ref[d + nb - 1] = gn * (1.0 / m_fft)


def _hyena_filter(L, w1, b1, fr, w2, b2, w3, *, cb=256):
    P = _hyena_block(L)
    nd = 2 * (L // P) - 1
    feats, t = _filter_tables(L)
    deltas = _decay_rates()
    cmat, smat = _dft_tables(P, F32)
    full2 = lambda j, o: (0, 0)
    return pl.pallas_call(
        functools.partial(_filter_kernel, L=L, P=P),
        grid=(HY_W // cb, 2),
        in_specs=[
            pl.BlockSpec(feats.shape, full2),
            pl.BlockSpec(t.shape, full2),
            pl.BlockSpec((1, cb), lambda j, o: (0, j)),
            pl.BlockSpec(w1.shape, full2),
            pl.BlockSpec(b1.shape, full2),
            pl.BlockSpec(fr.shape, full2),
            pl.BlockSpec(w2.shape, full2),
            pl.BlockSpec(b2.shape, full2),
            pl.BlockSpec((None, 2, FILTER_HIDDEN, cb), lambda j, o: (o, 0, 0, j)),
            pl.BlockSpec((P, P), full2),
            pl.BlockSpec((P, P), full2),
        ],
        out_specs=[
            pl.BlockSpec((None, nd, 2, P, cb), lambda j, o: (o, 0, 0, 0, j)),
            pl.BlockSpec((None, nd, 1, cb), lambda j, o: (o, 0, 0, j)),
        ],
        out_shape=[
            jax.ShapeDtypeStruct((2, nd, 2, P, HY_W), F32),
            jax.ShapeDtypeStruct((2, nd, 1, HY_W), F32),
        ],
        compiler_params=_cparams(("parallel", "parallel")),
        name=f"hyena_filter_{L}",
    )(feats, t, deltas, w1, b1, fr, w2, b2, w3, cmat, smat)


_PAD = 8


_MAC_ROWS = 32


def _hyena_kernel(v_ref, x1_ref, x2_ref, gate_ref, cw_ref, cb_ref, hb_ref, g_ref, gn_ref,
                  c_ref, s_ref, o_ref, pad_ref, z_ref, zb_ref, xc_ref, zf_ref, yr_ref, yi_ref, *, L, P):
    cb = z_ref.shape[1]
    nb = L // P
    zeros = jnp.zeros((_PAD, cb), F32)
    pad_ref[pl.ds(0, _PAD), :] = zeros
    pad_ref[pl.ds(L + _PAD, _PAD), :] = zeros

    def short_conv(src_ref, p):
        pad_ref[pl.ds(_PAD, L), :] = src_ref[...].astype(F32)
        w = cw_ref[:, p, :]
        return (pad_ref[pl.ds(_PAD - 1, L), :] * w[0:1] + pad_ref[pl.ds(_PAD, L), :] * w[1:2]
                + pad_ref[pl.ds(_PAD + 1, L), :] * w[2:3] + cb_ref[p:p + 1, :])

    sign = jnp.where(lax.broadcasted_iota(jnp.int32, (P, 1), 0) % 2 == 0, 1.0, -1.0).astype(F32)
    cm = c_ref[...]
    sm = s_ref[...]

    z_ref[...] = short_conv(v_ref, 0)
    for o in range(2):
        xc_ref[...] = short_conv(x1_ref if o == 0 else x2_ref, 1 + o)
        zb_ref[...] = z_ref[...].astype(BF16)
        zn = []
        for j in range(nb):
            rows = pl.ds(j * P, P)
            zf_ref[0, j] = _dot(cm, zb_ref[rows, :])
            zf_ref[1, j] = _dot(sm, zb_ref[rows, :])
            zn.append(jnp.sum(z_ref[rows, :] * sign, axis=0, keepdims=True))
        bias = hb_ref[o:o + 1, :]
        for i in range(nb):
            def mac(r, carry):
                rr = pl.ds(pl.multiple_of(r * _MAC_ROWS, _MAC_ROWS), _MAC_ROWS)
                yr = yi = None
                for j in range(nb):
                    d = i - j + nb - 1
                    gr, gi = g_ref[o, d, 0, rr, :], g_ref[o, d, 1, rr, :]
                    zr, zi = zf_ref[0, j, rr, :], zf_ref[1, j, rr, :]
                    pr, pi = gr * zr - gi * zi, gr * zi + gi * zr
                    yr, yi = (pr, pi) if yr is None else (yr + pr, yi + pi)
                yr_ref[rr, :] = yr.astype(BF16)
                yi_ref[rr, :] = yi.astype(BF16)
                return carry

            lax.fori_loop(0, P // _MAC_ROWS, mac, 0)
            yn = gn_ref[o, i + nb - 1] * zn[0]
            for j in range(1, nb):
                yn = yn + gn_ref[o, i - j + nb - 1] * zn[j]
            rows = pl.ds(i * P, P)
            y = _dot(cm, yr_ref[...]) + _dot(sm, yi_ref[...])
            y = y + sign * yn + z_ref[rows, :] * bias
            y = xc_ref[rows, :] * y
            if o == 0:
                z_ref[rows, :] = y
            else:
                o_ref[rows, :] = (y * _silu(gate_ref[rows, :].astype(F32))).astype(o_ref.dtype)


def _hyena(u_all, prev, conv_w, conv_b, hy_bias, g, gn, *, L, row_blk0, cb=256):
    P = _hyena_block(L)
    nb = L // P
    nd = 2 * nb - 1
    ncb = HY_W // cb
    cmat, smat = _dft_tables(P, BF16)

    def ublk(part):
        return pl.BlockSpec((L, cb), lambda j, b: (row_blk0 + b, part * ncb + j))

    in_specs = [
        ublk(0), ublk(1), ublk(2), ublk(3),
        pl.BlockSpec((3, 3, cb), lambda j, b: (0, 0, j)),
        pl.BlockSpec((3, cb), lambda j, b: (0, j)),
        pl.BlockSpec((2, cb), lambda j, b: (0, j)),
        pl.BlockSpec((2, nd, 2, P, cb), lambda j, b: (0, 0, 0, 0, j), pipeline_mode=pl.Buffered(1)),
        pl.BlockSpec((2, nd, 1, cb), lambda j, b: (0, 0, 0, j)),
        pl.BlockSpec((P, P), lambda j, b: (0, 0)),
        pl.BlockSpec((P, P), lambda j, b: (0, 0)),
    ]
    args = [u_all, u_all, u_all, u_all, conv_w, conv_b, hy_bias, g, gn, cmat, smat]
    kern = functools.partial(_hyena_kernel, L=L, P=P)
    aliases = {}
    if prev is not None:
        in_specs.append(pl.BlockSpec(memory_space=pl.ANY))
        args.append(prev)
        aliases = {len(args) - 1: 0}
        kern = _drop_arg(kern, len(args) - 1)
    return pl.pallas_call(
        kern,
        grid=(ncb, BATCH),
        in_specs=in_specs,
        out_specs=pl.BlockSpec((L, cb), lambda j, b: (row_blk0 + b, j)),
        out_shape=jax.ShapeDtypeStruct((ROWS_ALL, HY_W), BF16),
        scratch_shapes=[
            pltpu.VMEM((L + 2 * _PAD, cb), F32),
            pltpu.VMEM((L, cb), F32),
            pltpu.VMEM((L, cb), BF16),
            pltpu.VMEM((L, cb), F32),
            pltpu.VMEM((2, nb, P, cb), F32),
            pltpu.VMEM((P, cb), BF16),
            pltpu.VMEM((P, cb), BF16),
        ],
        input_output_aliases=aliases,
        compiler_params=_cparams(("parallel", "parallel")),
        name=f"hyena_{L}",
    )(*args)


def _drop_arg(kern, idx):
    def wrapped(*refs):
        return kern(*refs[:idx], *refs[idx + 1:])
    return wrapped


def _qkv_kernel(uq_ref, ukv_ref, k1_ref, k2_ref, qg_ref, kg_ref, wq_ref, wkv_ref, cos_ref, sin_ref,
                q_ref, k_ref, v_ref):
    def rms(x, g):
        x = x.astype(F32)
        return (x * lax.rsqrt(jnp.mean(x * x, axis=-1, keepdims=True) + EPS) * g).astype(BF16)

    cos = cos_ref[...]
    sin = sin_ref[...]
    lane = lax.broadcasted_iota(jnp.int32, (1, LANE), 1)
    half_mask = [(lane < MLA_DR), (lane >= MLA_DR)]

    qa = _dot(rms(uq_ref[...], qg_ref[...]), wq_ref[...])
    kv = _dot(rms(ukv_ref[...], kg_ref[...]), wkv_ref[...])
    k_rope = (k1_ref[...].astype(F32) * cos + k2_ref[...].astype(F32) * sin).astype(BF16)
    n_rot = MLA_HEADS * MLA_DR
    for h in range(MLA_HEADS):
        c = h // 2
        qr = qa[:, MLA_W + c * LANE:MLA_W + (c + 1) * LANE]
        qs = qa[:, MLA_W + n_rot + c * LANE:MLA_W + n_rot + (c + 1) * LANE]
        rot = jnp.where(half_mask[h % 2], qr * cos + qs * sin, 0.0)
        q_ref[h, :, 0:LANE] = (qa[:, h * LANE:(h + 1) * LANE] * ATTN_SCALE).astype(BF16)
        q_ref[h, :, LANE:2 * LANE] = (rot * ATTN_SCALE).astype(BF16)
        k_ref[h, :, 0:LANE] = kv[:, h * LANE:(h + 1) * LANE].astype(BF16)
        k_ref[h, :, LANE:2 * LANE] = k_rope
        v_ref[h] = kv[:, MLA_W + h * LANE:MLA_W + (h + 1) * LANE].astype(BF16)


def _qkv(u_all, qg, kg, wq, wkv, cos_t, sin_t, *, tr=512):
    n_lat = ROWS_L // tr
    per_seq = SEQ // tr

    def tab(i):
        return (jnp.where(i < n_lat, i % per_seq, per_seq), 0)

    full = lambda i: (0, 0)
    return pl.pallas_call(
        _qkv_kernel,
        grid=(ROWS_ALL // tr,),
        in_specs=[
            pl.BlockSpec((tr, Q_RANK), lambda i: (i, U_Q // Q_RANK)),
            pl.BlockSpec((tr, KV_RANK), lambda i: (i, U_KV // KV_RANK)),
            pl.BlockSpec((tr, LANE), lambda i: (i, U_KR1 // LANE)),
            pl.BlockSpec((tr, LANE), lambda i: (i, U_KR2 // LANE)),
            pl.BlockSpec((1, Q_RANK), full),
            pl.BlockSpec((1, KV_RANK), full),
            pl.BlockSpec(wq.shape, full),
            pl.BlockSpec(wkv.shape, full),
            pl.BlockSpec((tr, LANE), tab),
            pl.BlockSpec((tr, LANE), tab),
        ],
        out_specs=[
            pl.BlockSpec((MLA_HEADS, tr, 2 * LANE), lambda i: (0, i, 0)),
            pl.BlockSpec((MLA_HEADS, tr, 2 * LANE), lambda i: (0, i, 0)),
            pl.BlockSpec((MLA_HEADS, tr, MLA_DV), lambda i: (0, i, 0)),
        ],
        out_shape=[
            jax.ShapeDtypeStruct((MLA_HEADS, ROWS_ALL, 2 * LANE), BF16),
            jax.ShapeDtypeStruct((MLA_HEADS, ROWS_ALL, 2 * LANE), BF16),
            jax.ShapeDtypeStruct((MLA_HEADS, ROWS_ALL, MLA_DV), BF16),
        ],
        compiler_params=_cparams(("parallel",)),
        name="qkv",
    )(u_all, u_all, u_all, u_all, qg, kg, wq, wkv, cos_t, sin_t)


def _attn_kernel(*refs, n_src):
    q_ref = refs[0]
    k_refs = refs[1:1 + n_src]
    v_refs = refs[1 + n_src:1 + 2 * n_src]
    g_ref = refs[1 + 2 * n_src]
    o_ref = refs[2 + 2 * n_src]
    for h in range(MLA_HEADS):
        q = q_ref[h]
        s = [_dot_nt(q, k[h]) for k in k_refs]
        m = s[0].max(axis=-1, keepdims=True)
        for si in s[1:]:
            m = jnp.maximum(m, si.max(axis=-1, keepdims=True))
        p = [jnp.exp(si - m) for si in s]
        l = p[0].sum(axis=-1, keepdims=True)
        for pi in p[1:]:
            l = l + pi.sum(axis=-1, keepdims=True)
        acc = _dot(p[0].astype(BF16), v_refs[0][h])
        for pi, v in zip(p[1:], v_refs[1:]):
            acc = acc + _dot(pi.astype(BF16), v[h])
        gate = _silu(g_ref[:, h * LANE:(h + 1) * LANE].astype(F32))
        o_ref[:, h * LANE:(h + 1) * LANE] = (acc / l * gate).astype(o_ref.dtype)


def _attention(q, k, v, u_all, prev, *, latent, tq=256):
    H = MLA_HEADS
    if latent:
        n_q = SEQ // tq
        q_blk = lambda b, i: b * n_q + i
        k_specs = [pl.BlockSpec((H, SEQ, 2 * LANE), lambda b, i: (0, b, 0)),
                   pl.BlockSpec((H, CTX_LEN, 2 * LANE), lambda b, i: (0, ROWS_L // CTX_LEN + b, 0))]
        v_specs = [pl.BlockSpec((H, SEQ, MLA_DV), lambda b, i: (0, b, 0)),
                   pl.BlockSpec((H, CTX_LEN, MLA_DV), lambda b, i: (0, ROWS_L // CTX_LEN + b, 0))]
    else:
        n_q = CTX_LEN // tq
        q_blk = lambda b, i: ROWS_L // tq + b * n_q + i
        k_specs = [pl.BlockSpec((H, CTX_LEN, 2 * LANE), lambda b, i: (0, ROWS_L // CTX_LEN + b, 0))]
        v_specs = [pl.BlockSpec((H, CTX_LEN, MLA_DV), lambda b, i: (0, ROWS_L // CTX_LEN + b, 0))]
    n_src = len(k_specs)
    in_specs = ([pl.BlockSpec((H, tq, 2 * LANE), lambda b, i: (0, q_blk(b, i), 0))] + k_specs + v_specs
                + [pl.BlockSpec((tq, MLA_W), lambda b, i: (q_blk(b, i), U_MLA_G // MLA_W))])
    args = [q] + [k] * n_src + [v] * n_src + [u_all]
    kern = functools.partial(_attn_kernel, n_src=n_src)
    aliases = {}
    if prev is not None:
        in_specs.append(pl.BlockSpec(memory_space=pl.ANY))
        args.append(prev)
        aliases = {len(args) - 1: 0}
        kern = _drop_arg(kern, len(args) - 1)
    return pl.pallas_call(
        kern,
        grid=(BATCH, n_q),
        in_specs=in_specs,
        out_specs=pl.BlockSpec((tq, MLA_W), lambda b, i: (q_blk(b, i), 0)),
        out_shape=jax.ShapeDtypeStruct((ROWS_ALL, MLA_W), BF16),
        input_output_aliases=aliases,
        compiler_params=_cparams(("parallel", "parallel")),
        name="attn_latent" if latent else "attn_ctx",
    )(*args)


_POOL_PAD = 16


def _pool_kernel(x_ref, g_ref, w_ref, sc_ref, o_ref, pad_ref, *, L):
    zeros = jnp.zeros((_POOL_PAD, POOL_GROUP), F32)
    pad_ref[pl.ds(0, _POOL_PAD), :] = zeros
    pad_ref[pl.ds(L + _POOL_PAD, _POOL_PAD), :] = zeros
    t = lax.broadcasted_iota(jnp.int32, (L, 1), 0)
    for gi, win in enumerate(POOL_WINDOWS):
        half = win // 2
        cols = slice(gi * POOL_GROUP, (gi + 1) * POOL_GROUP)
        x = x_ref[:, cols].astype(F32)
        pad_ref[pl.ds(_POOL_PAD, L), :] = x
        acc = pad_ref[pl.ds(_POOL_PAD - half, L), :]
        for j in range(-half + 1, half):
            acc = acc + pad_ref[pl.ds(_POOL_PAD + j, L), :]
        cnt = (jnp.minimum(t + half, L) - jnp.maximum(t - half, 0)).astype(F32)
        dlt = (acc / cnt - x).astype(BF16)
        y = _dot(dlt, w_ref[gi]) * sc_ref[:, cols]
        o_ref[:, cols] = (y * _silu(g_ref[:, cols].astype(F32))).astype(o_ref.dtype)


def _pool(u_all, prev, w_pool, pool_scale, *, L, row_blk0):
    in_specs = [
        pl.BlockSpec((L, POOL_W), lambda b: (row_blk0 + b, U_POOL // POOL_W)),
        pl.BlockSpec((L, POOL_W), lambda b: (row_blk0 + b, U_POOL_G // POOL_W)),
        pl.BlockSpec(w_pool.shape, lambda b: (0, 0, 0)),
        pl.BlockSpec((1, POOL_W), lambda b: (0, 0)),
    ]
    args = [u_all, u_all, w_pool, pool_scale]
    kern = functools.partial(_pool_kernel, L=L)
    aliases = {}
    if prev is not None:
        in_specs.append(pl.BlockSpec(memory_space=pl.ANY))
        args.append(prev)
        aliases = {len(args) - 1: 0}
        kern = _drop_arg(kern, len(args) - 1)
    return pl.pallas_call(
        kern,
        grid=(BATCH,),
        in_specs=in_specs,
        out_specs=pl.BlockSpec((L, POOL_W), lambda b: (row_blk0 + b, 0)),
        out_shape=jax.ShapeDtypeStruct((ROWS_ALL, POOL_W), BF16),
        scratch_shapes=[pltpu.VMEM((L + 2 * _POOL_PAD, POOL_GROUP), F32)],
        input_output_aliases=aliases,
        compiler_params=_cparams(("parallel",)),
        name=f"pool_{L}",
    )(*args)


def _outproj_kernel(x_ref, hy_ref, at_ref, po_ref, w1_ref, w2_ref, w3_ref, gt_ref, fg_ref, o_ref, *, final):
    acc = _dot(hy_ref[...], w1_ref[...]) + _dot(at_ref[...], w2_ref[...]) + _dot(po_ref[...], w3_ref[...])
    y = x_ref[...] + gt_ref[...] * acc
    if final:
        y = y * lax.rsqrt(jnp.mean(y * y, axis=-1, keepdims=True) + EPS) * fg_ref[...]
    o_ref[...] = y


def _outproj(x_all, hy, att, po, w_hy, w_at, w_po, mod, fg, *, n_row_tiles, final, tm=512):
    d = x_all.shape[1]
    tiles_per_batch = SEQ // tm

    def mod_row(i):
        return jnp.minimum(i // tiles_per_batch, BATCH)

    full = lambda i: (0, 0)
    return pl.pallas_call(
        functools.partial(_outproj_kernel, final=final),
        grid=(n_row_tiles,),
        in_specs=[
            pl.BlockSpec((tm, d), lambda i: (i, 0)),
            pl.BlockSpec((tm, HY_W), lambda i: (i, 0)),
            pl.BlockSpec((tm, MLA_W), lambda i: (i, 0)),
            pl.BlockSpec((tm, POOL_W), lambda i: (i, 0)),
            pl.BlockSpec(w_hy.shape, full, pipeline_mode=pl.Buffered(1)),
            pl.BlockSpec(w_at.shape, full, pipeline_mode=pl.Buffered(1)),
            pl.BlockSpec(w_po.shape, full, pipeline_mode=pl.Buffered(1)),
            pl.BlockSpec((None, 1, d), lambda i: (mod_row(i), 0, 2)),
            pl.BlockSpec((1, d), full),
        ],
        out_specs=pl.BlockSpec((tm, d), lambda i: (i, 0)),
        out_shape=jax.ShapeDtypeStruct((n_row_tiles * tm, d), F32),
        compiler_params=_cparams(("parallel",)),
        name="outproj_final" if final else "outproj",
    )(x_all, hy, att, po, w_hy, w_at, w_po, mod, fg)


def _rope_lane_tables(ident_rows):
    n_rows = SEQ // GRID_W
    row = np.repeat(np.arange(n_rows, dtype=np.float64), GRID_W)
    col = np.tile(np.arange(GRID_W, dtype=np.float64), n_rows)
    n_freq = MLA_DR // 4
    inv = ROPE_BASE ** (-np.arange(n_freq, dtype=np.float64) / n_freq)
    ang = np.concatenate([row[:, None] * inv, col[:, None] * inv], axis=-1)
    cos, sin = np.cos(ang), np.sin(ang)
    cos_t = np.concatenate([cos, cos, cos, cos], axis=-1)
    sin_t = np.concatenate([-sin, sin, -sin, sin], axis=-1)
    cos_t = np.concatenate([cos_t, np.ones((ident_rows, LANE))], axis=0)
    sin_t = np.concatenate([sin_t, np.zeros((ident_rows, LANE))], axis=0)
    return jnp.asarray(cos_t, F32), jnp.asarray(sin_t, F32)


def _dft_tables(P, dtype):
    idx = np.arange(P, dtype=np.int64)
    ang = ((idx[:, None] * idx[None, :]) % (2 * P)).astype(np.float64) * (math.pi / P)
    return jnp.asarray(np.cos(ang), F32).astype(dtype), jnp.asarray(np.sin(ang), F32).astype(dtype)


def _filter_tables(L):
    t = np.linspace(0.0, 1.0, L)[:, None]
    wpos = (2.0 * math.pi / L) * np.arange(L, dtype=np.float64)[:, None]
    bands = np.linspace(1e-4, FILTER_BANDS - 1, FILTER_BANDS)[None, :]
    feats = np.concatenate([t, np.cos(bands * wpos), -np.sin(bands * wpos)], axis=-1)
    feats = np.pad(feats, ((0, 0), (0, LANE - FILTER_EMB)))
    return jnp.asarray(feats, F32), jnp.asarray(t, F32)


def _decay_rates():
    d = np.abs(np.linspace(math.log(DECAY_TARGET) / SLOW_DECAY, math.log(DECAY_TARGET) / FAST_DECAY, HY_W))
    return jnp.asarray(d[None, :], F32)


def _kr_permutation():
    p = np.zeros((MLA_DR, 2 * LANE), np.float32)
    half = MLA_DR // 2
    for grp, odd in enumerate((0, 1, 0, 1, 1, 0, 1, 0)):
        for i in range(half):
            p[2 * i + odd, grp * half + i] = 1.0
    return jnp.asarray(p, BF16)


def _pack_w_in_kernel(w_ref, p_ref, o_ref):
    o_ref[:, 0:R_OFF_KR] = w_ref[:, 0:R_OFF_KR].astype(BF16)
    kr = w_ref[:, R_OFF_KR:R_OFF_MLA_G].astype(BF16)
    o_ref[:, U_KR1:U_MLA_G] = _dot(kr, p_ref[...]).astype(BF16)
    o_ref[:, U_MLA_G:U_W] = w_ref[:, R_OFF_MLA_G:].astype(BF16)


def _pack_w_in(w_in, tk=256):
    depth, d, n = w_in.shape
    perm = _kr_permutation()
    return pl.pallas_call(
        _pack_w_in_kernel,
        grid=(depth, d // tk),
        in_specs=[
            pl.BlockSpec((None, tk, n), lambda l, i: (l, i, 0)),
            pl.BlockSpec(perm.shape, lambda l, i: (0, 0)),
        ],
        out_specs=pl.BlockSpec((None, tk, U_W), lambda l, i: (l, i, 0)),
        out_shape=jax.ShapeDtypeStruct((depth, d, U_W), BF16),
        compiler_params=_cparams(("parallel", "parallel")),
        name="pack_w_in",
    )(w_in, perm)


def _pack_w_uq(w_uq):
    w = w_uq.reshape(DEPTH, Q_RANK, MLA_HEADS, MLA_DN + MLA_DR)
    nope = w[..., :MLA_DN].reshape(DEPTH, Q_RANK, MLA_W)
    a, b = w[..., MLA_DN::2], w[..., MLA_DN + 1::2]
    rot = jnp.concatenate([a, b], axis=-1).reshape(DEPTH, Q_RANK, MLA_HEADS * MLA_DR)
    swp = jnp.concatenate([b, a], axis=-1).reshape(DEPTH, Q_RANK, MLA_HEADS * MLA_DR)
    return jnp.concatenate([nope, rot, swp], axis=-1).astype(BF16)


def _pack_w_ukv(w_ukv):
    w = w_ukv.reshape(DEPTH, KV_RANK, MLA_HEADS, MLA_DN + MLA_DV)
    return jnp.concatenate([w[..., :MLA_DN].reshape(DEPTH, KV_RANK, MLA_W),
                            w[..., MLA_DN:].reshape(DEPTH, KV_RANK, MLA_W)], axis=-1).astype(BF16)


def kernel(x, c, ctx, c_ctx, norm_g, w_ada, b_ada, w_in, hy_conv_w, hy_conv_b, hf_w1, hf_b1, hf_freq,
           hf_w2, hf_b2, hf_w3, hy_bias, q_norm_g, w_uq, kv_norm_g, w_ukv, w_pool, pool_scale, w_out,
           final_norm_g):
    assert x.shape == (BATCH, SEQ, D_MODEL) and ctx.shape == (BATCH, CTX_LEN, D_MODEL)

    qkv_tr = 512
    cos_t, sin_t = _rope_lane_tables(qkv_tr)

    w_in_p = _pack_w_in(w_in)
    w_q_p = _pack_w_uq(w_uq)
    w_kv_p = _pack_w_ukv(w_ukv)
    w_out_b = w_out.astype(BF16)
    w_pool_b = w_pool.astype(BF16)
    w1_p = jnp.pad(hf_w1, ((0, 0), (0, LANE - FILTER_EMB), (0, 0)))
    w3_p = hf_w3.reshape(DEPTH, FILTER_HIDDEN, 2, 2, HY_W).transpose(0, 2, 3, 1, 4)
    conv_w = hy_conv_w.reshape(DEPTH, 3, 3, HY_W)
    conv_b = hy_conv_b.reshape(DEPTH, 3, HY_W)

    cond = jnp.concatenate([c, c_ctx[None], jnp.zeros((8 - BATCH - 1, D_MODEL), F32)], axis=0)
    mod = _adaln(cond, w_ada, b_ada[:, None, :])
    mod = mod.reshape(DEPTH, 8, 1, 3 * D_MODEL)

    x_all = jnp.concatenate([x.reshape(ROWS_L, D_MODEL), ctx.reshape(ROWS_C, D_MODEL)], axis=0)
    row2 = lambda a: a[None, :]

    for l in range(DEPTH):
        last = l == DEPTH - 1
        u_all = _inproj(x_all, row2(norm_g[l]), mod[l], w_in_p[l], n_row_tiles=ROWS_ALL // 1024)
        q, k, v = _qkv(u_all, row2(q_norm_g[l]), row2(kv_norm_g[l]), w_q_p[l], w_kv_p[l], cos_t, sin_t,
                       tr=qkv_tr)
        hy = att = po = None
        for L, row_blk0, latent in ((SEQ, 0, True), (CTX_LEN, ROWS_L // CTX_LEN, False)):
            if last and not latent:
                continue
            g, gn = _hyena_filter(L, w1_p[l], row2(hf_b1[l]), row2(hf_freq[l]), hf_w2[l], row2(hf_b2[l]),
                                  w3_p[l])
            hy = _hyena(u_all, hy, conv_w[l], conv_b[l], hy_bias[l], g, gn, L=L, row_blk0=row_blk0)
            att = _attention(q, k, v, u_all, att, latent=latent, tq=256)
            po = _pool(u_all, po, w_pool_b[l], row2(pool_scale[l]), L=L, row_blk0=row_blk0)
        n_tiles = (ROWS_L if last else ROWS_ALL) // 512
        x_all = _outproj(x_all, hy, att, po, w_out_b[l, :HY_W], w_out_b[l, HY_W:HY_W + MLA_W],
                         w_out_b[l, HY_W + MLA_W:], mod[l], row2(final_norm_g),
                         n_row_tiles=n_tiles, final=last)
    return x_all.reshape(BATCH, SEQ, D_MODEL)
```

```python
import functools
import math

import jax
import jax.numpy as jnp
import numpy as np
from jax import lax
from jax.experimental import pallas as pl
from jax.experimental.pallas import tpu as pltpu

F32 = jnp.float32
BF16 = jnp.bfloat16

D_MODEL = 2048
BATCH = 4
SEQ = 2048
DEPTH = 4
CTX_LEN = 256
GRID_W = 64
EPS = 1e-6

HY_W = 512
MLA_HEADS = 8
MLA_DN = 128
MLA_DR = 64
MLA_DV = 128
MLA_W = MLA_HEADS * MLA_DV
Q_RANK = 512
KV_RANK = 256
POOL_W = 512
POOL_WINDOWS = (2, 4, 8, 16)
POOL_GROUP = 128

FILTER_EMB = 33
FILTER_BANDS = 16
FILTER_HIDDEN = 64
DECAY_TARGET = 1e-2
FAST_DECAY = 0.3
SLOW_DECAY = 1.5
ROPE_BASE = 10000.0
ATTN_SCALE = (MLA_DN + MLA_DR) ** -0.5

R_OFF_KR = 2816
R_OFF_MLA_G = 2880

U_HY = 0
U_Q = 2048
U_KV = 2560
U_KR1 = 2816
U_KR2 = 2944
U_MLA_G = 3072
U_POOL = 4096
U_POOL_G = 4608
U_W = 5120

ROWS_L = BATCH * SEQ
ROWS_C = BATCH * CTX_LEN
ROWS_ALL = ROWS_L + ROWS_C

VMEM_LIMIT_BYTES = 56 * 1024 * 1024
LANE = 128


def _cparams(sem):
    return pltpu.CompilerParams(dimension_semantics=sem, vmem_limit_bytes=VMEM_LIMIT_BYTES)


def _silu(x):
    return x * jax.nn.sigmoid(x)


def _dot(a, b):
    return jnp.dot(a, b, preferred_element_type=F32)


def _dot_nt(a, b):
    return lax.dot_general(a, b, (((1,), (1,)), ((), ())), preferred_element_type=F32)


def _adaln_kernel(c_ref, w_ref, b_ref, o_ref):
    a = _silu(c_ref[...]).astype(BF16)
    o_ref[...] = _dot(a, w_ref[...].astype(BF16)) + b_ref[...]


def _adaln(cond, w_ada, b_ada, tn=1024):
    depth, d, n = w_ada.shape
    return pl.pallas_call(
        _adaln_kernel,
        grid=(depth, n // tn),
        in_specs=[
            pl.BlockSpec((8, d), lambda l, j: (0, 0)),
            pl.BlockSpec((None, d, tn), lambda l, j: (l, 0, j)),
            pl.BlockSpec((None, 1, tn), lambda l, j: (l, 0, j)),
        ],
        out_specs=pl.BlockSpec((None, 8, tn), lambda l, j: (l, 0, j)),
        out_shape=jax.ShapeDtypeStruct((depth, 8, n), F32),
        compiler_params=_cparams(("parallel", "parallel")),
        name="adaln",
    )(cond, w_ada, b_ada)


def _inproj_kernel(x_ref, g_ref, sh_ref, sc_ref, w_ref, o_ref, h_ref, *, sub):
    @pl.when(pl.program_id(1) == 0)
    def _():
        g = g_ref[...]
        sc = 1.0 + sc_ref[...]
        sh = sh_ref[...]
        for r in range(0, x_ref.shape[0], sub):
            x = x_ref[pl.ds(r, sub), :]
            ms = jnp.mean(x * x, axis=-1, keepdims=True)
            y = x * lax.rsqrt(ms + EPS) * g
            h_ref[pl.ds(r, sub), :] = (y * sc + sh).astype(BF16)

    o_ref[...] = _dot_nt(h_ref[...], w_ref[...]).astype(o_ref.dtype)


def _layer_spec(arr, l):
    zeros = (0,) * (arr.ndim - 1)
    return pl.BlockSpec((None,) + arr.shape[1:], lambda *_: (l,) + zeros)


def _inproj(x_all, g, mod, w_t, l, *, n_row_tiles, tm=1024, tn=1280):
    d = x_all.shape[1]
    n = w_t.shape[1]
    tiles_per_batch = SEQ // tm

    def mod_row(i):
        return jnp.minimum(i // tiles_per_batch, BATCH)

    return pl.pallas_call(
        functools.partial(_inproj_kernel, sub=256),
        grid=(n_row_tiles, n // tn),
        in_specs=[
            pl.BlockSpec((tm, d), lambda i, j: (i, 0)),
            _layer_spec(g, l),
            pl.BlockSpec((None, None, 1, d), lambda i, j: (l, mod_row(i), 0, 0)),
            pl.BlockSpec((None, None, 1, d), lambda i, j: (l, mod_row(i), 0, 1)),
            pl.BlockSpec((None, tn, d), lambda i, j: (l, j, 0)),
        ],
        out_specs=pl.BlockSpec((tm, tn), lambda i, j: (i, j)),
        out_shape=jax.ShapeDtypeStruct((x_all.shape[0], n), BF16),
        scratch_shapes=[pltpu.VMEM((tm, d), BF16)],
        compiler_params=_cparams(("parallel", "arbitrary")),
        name="inproj",
    )(x_all, g, mod, mod, w_t)


def _hyena_block(L):
    return min(L, 512)


def _filter_kernel(feats_ref, t_ref, dl_ref, w1_ref, b1_ref, fr_ref, w2_ref, b2_ref, w3_ref,
                   c_ref, s_ref, g_ref, gn_ref, h_ref, *, L, P):
    hp = lax.Precision.HIGHEST
    hdot = lambda a, b: jnp.dot(a, b, precision=hp, preferred_element_type=F32)
    nb = L // P
    m_fft = 2 * P

    @pl.when(pl.program_id(0) == 0)
    def _():
        fr = fr_ref[...]
        h1 = jnp.sin(fr * (hdot(feats_ref[...], w1_ref[...]) + b1_ref[...]))
        h_ref[...] = jnp.sin(fr * (hdot(h1, w2_ref[...]) + b2_ref[...]))

    h = h_ref[...]
    decay = jnp.exp(-t_ref[...] * dl_ref[...])
    row = lax.broadcasted_iota(jnp.int32, (L, 1), 0)
    rp = lax.broadcasted_iota(jnp.int32, (P, 1), 0)
    sgn = jnp.where(rp % 2 == 0, 1.0, -1.0).astype(F32)
    wgt = jnp.where(rp == 0, 1.0 / m_fft, 2.0 / m_fft).astype(F32)
    cm = c_ref[...]
    sm = s_ref[...]
    hf = hdot(h, w3_ref[0]) * decay
    hb = hdot(h, w3_ref[1]) * decay
    hb = jnp.where(row == 0, 0.0, hb)
    nrm = lax.rsqrt(jnp.sum(hf * hf + hb * hb, axis=0, keepdims=True) + EPS)
    F, B = [], []
    for arr, out in ((hf * nrm, F), (hb * nrm, B)):
        for j in range(nb):
            blk = arr[j * P:(j + 1) * P].astype(BF16)
            b32 = blk.astype(F32)
            out.append((_dot(cm, blk), _dot(sm, blk), jnp.sum(b32 * sgn, axis=0, keepdims=True), b32[0:1]))
    for d in range(-(nb - 1), nb):
        if d >= 1:
            gr = F[d][0] + sgn * (F[d - 1][0] - F[d - 1][3])
            gi = F[d][1] + sgn * F[d - 1][1]
            gn = F[d][2] + F[d - 1][2] - F[d - 1][3]
        elif d == 0:
            gr = F[0][0] + B[0][0]
            gi = F[0][1] - B[0][1]
            gn = F[0][2] + B[0][2]
        else:
            e = -d
            gr = B[e][0] + sgn * (B[e - 1][0] - B[e - 1][3])
            gi = -B[e][1] - sgn * B[e - 1][1]
            gn = B[e][2] + B[e - 1][2] - B[e - 1][3]
        g_ref[d + nb - 1, 0] = gr * wgt
        g_ref[d + nb - 1, 1] = gi * wgt
        gn_ref[d + nb - 1] = gn * (1.0 / m_fft)


def _hyena_filter(L, w1, b1, fr, w2, b2, w3, l, *, cb=256):
    P = _hyena_block(L)
    nd = 2 * (L // P) - 1
    feats, t = _filter_tables(L)
    deltas = _decay_rates()
    cmat, smat = _dft_tables(P, BF16)
    full2 = lambda s: (0, 0)
    return pl.pallas_call(
        functools.partial(_filter_kernel, L=L, P=P),
        grid=(2 * (HY_W // cb),),
        in_specs=[
            pl.BlockSpec(feats.shape, full2),
            pl.BlockSpec(t.shape, full2),
            pl.BlockSpec((1, cb), lambda s: (0, s // 2)),
            _layer_spec(w1, l), _layer_spec(b1, l), _layer_spec(fr, l), _layer_spec(w2, l), _layer_spec(b2, l),
            pl.BlockSpec((None, None, 2, FILTER_HIDDEN, cb), lambda s: (l, s % 2, 0, 0, s // 2)),
            pl.BlockSpec((P, P), full2),
            pl.BlockSpec((P, P), full2),
        ],
        out_specs=[
            pl.BlockSpec((None, nd, 2, P, cb), lambda s: (s % 2, 0, 0, 0, s // 2)),
            pl.BlockSpec((None, nd, 1, cb), lambda s: (s % 2, 0, 0, s // 2)),
        ],
        out_shape=[
            jax.ShapeDtypeStruct((2, nd, 2, P, HY_W), F32),
            jax.ShapeDtypeStruct((2, nd, 1, HY_W), F32),
        ],
        scratch_shapes=[pltpu.VMEM((L, FILTER_HIDDEN), F32)],
        compiler_params=_cparams(("arbitrary",)),
        name=f"hyena_filter_{L}",
    )(feats, t, deltas, w1, b1, fr, w2, b2, w3, cmat, smat)


_PAD = 8


_MAC_ROWS = 32


def _hyena_kernel(v_ref, x1_ref, x2_ref, gate_ref, cw_ref, cb_ref, hb_ref, g_ref, gn_ref,
                  c_ref, s_ref, o_ref, pad_ref, z_ref, zb_ref, xc_ref, zf_ref, yr_ref, yi_ref, *, L, P):
    cb = z_ref.shape[1]
    nb = L // P
    zeros = jnp.zeros((_PAD, cb), F32)
    pad_ref[pl.ds(0, _PAD), :] = zeros
    pad_ref[pl.ds(L + _PAD, _PAD), :] = zeros

    def short_conv(src_ref, p):
        pad_ref[pl.ds(_PAD, L), :] = src_ref[...].astype(F32)
        w = cw_ref[:, p, :]
        return (pad_ref[pl.ds(_PAD - 1, L), :] * w[0:1] + pad_ref[pl.ds(_PAD, L), :] * w[1:2]
                + pad_ref[pl.ds(_PAD + 1, L), :] * w[2:3] + cb_ref[p:p + 1, :])

    sign = jnp.where(lax.broadcasted_iota(jnp.int32, (P, 1), 0) % 2 == 0, 1.0, -1.0).astype(F32)
    cm = c_ref[...]
    sm = s_ref[...]

    z_ref[...] = short_conv(v_ref, 0)
    for o in range(2):
        xc_ref[...] = short_conv(x1_ref if o == 0 else x2_ref, 1 + o)
        zb_ref[...] = z_ref[...].astype(BF16)
        zn = []
        for j in range(nb):
            rows = pl.ds(j * P, P)
            zf_ref[0, j] = _dot(cm, zb_ref[rows, :])
            zf_ref[1, j] = _dot(sm, zb_ref[rows, :])
            zn.append(jnp.sum(z_ref[rows, :] * sign, axis=0, keepdims=True))
        bias = hb_ref[o:o + 1, :]
        for i in range(nb):
            def mac(r, carry):
                rr = pl.ds(pl.multiple_of(r * _MAC_ROWS, _MAC_ROWS), _MAC_ROWS)
                yr = yi = None
                for j in range(nb):
                    d = i - j + nb - 1
                    gr, gi = g_ref[o, d, 0, rr, :], g_ref[o, d, 1, rr, :]
                    zr, zi = zf_ref[0, j, rr, :], zf_ref[1, j, rr, :]
                    pr, pi = gr * zr - gi * zi, gr * zi + gi * zr
                    yr, yi = (pr, pi) if yr is None else (yr + pr, yi + pi)
                yr_ref[rr, :] = yr.astype(BF16)
                yi_ref[rr, :] = yi.astype(BF16)
                return carry

            lax.fori_loop(0, P // _MAC_ROWS, mac, 0)
            yn = gn_ref[o, i + nb - 1] * zn[0]
            for j in range(1, nb):
                yn = yn + gn_ref[o, i - j + nb - 1] * zn[j]
            rows = pl.ds(i * P, P)
            y = _dot(cm, yr_ref[...]) + _dot(sm, yi_ref[...])
            y = y + sign * yn + z_ref[rows, :] * bias
            y = xc_ref[rows, :] * y
            if o == 0:
                z_ref[rows, :] = y
            else:
                o_ref[rows, :] = (y * _silu(gate_ref[rows, :].astype(F32))).astype(o_ref.dtype)


def _hyena(u_all, prev, conv_w, conv_b, hy_bias, g, gn, l, *, L, row_blk0, cb=256):
    P = _hyena_block(L)
    nb = L // P
    nd = 2 * nb - 1
    ncb = HY_W // cb
    cmat, smat = _dft_tables(P, BF16)

    def ublk(part):
        return pl.BlockSpec((L, cb), lambda j, b: (row_blk0 + b, part * ncb + j))

    in_specs = [
        ublk(0), ublk(1), ublk(2), ublk(3),
        pl.BlockSpec((None, 3, 3, cb), lambda j, b: (l, 0, 0, j)),
        pl.BlockSpec((None, 3, cb), lambda j, b: (l, 0, j)),
        pl.BlockSpec((None, 2, cb), lambda j, b: (l, 0, j)),
        pl.BlockSpec((2, nd, 2, P, cb), lambda j, b: (0, 0, 0, 0, j), pipeline_mode=pl.Buffered(1)),
        pl.BlockSpec((2, nd, 1, cb), lambda j, b: (0, 0, 0, j)),
        pl.BlockSpec((P, P), lambda j, b: (0, 0)),
        pl.BlockSpec((P, P), lambda j, b: (0, 0)),
    ]
    args = [u_all, u_all, u_all, u_all, conv_w, conv_b, hy_bias, g, gn, cmat, smat]
    kern = functools.partial(_hyena_kernel, L=L, P=P)
    aliases = {}
    if prev is not None:
        in_specs.append(pl.BlockSpec(memory_space=pl.ANY))
        args.append(prev)
        aliases = {len(args) - 1: 0}
        kern = _drop_arg(kern, len(args) - 1)
    return pl.pallas_call(
        kern,
        grid=(ncb, BATCH),
        in_specs=in_specs,
        out_specs=pl.BlockSpec((L, cb), lambda j, b: (row_blk0 + b, j)),
        out_shape=jax.ShapeDtypeStruct((ROWS_ALL, HY_W), BF16),
        scratch_shapes=[
            pltpu.VMEM((L + 2 * _PAD, cb), F32),
            pltpu.VMEM((L, cb), F32),
            pltpu.VMEM((L, cb), BF16),
            pltpu.VMEM((L, cb), F32),
            pltpu.VMEM((2, nb, P, cb), F32),
            pltpu.VMEM((P, cb), BF16),
            pltpu.VMEM((P, cb), BF16),
        ],
        input_output_aliases=aliases,
        compiler_params=_cparams(("parallel", "parallel")),
        name=f"hyena_{L}",
    )(*args)


def _drop_arg(kern, idx):
    def wrapped(*refs):
        return kern(*refs[:idx], *refs[idx + 1:])
    return wrapped


def _qkv_kernel(uq_ref, ukv_ref, k1_ref, k2_ref, qg_ref, kg_ref, wq_ref, wkv_ref, cos_ref, sin_ref,
                q_ref, k_ref, v_ref):
    def rms(x, g):
        x = x.astype(F32)
        return (x * lax.rsqrt(jnp.mean(x * x, axis=-1, keepdims=True) + EPS) * g).astype(BF16)

    cos = cos_ref[...]
    sin = sin_ref[...]
    lane = lax.broadcasted_iota(jnp.int32, (1, LANE), 1)
    half_mask = [(lane < MLA_DR), (lane >= MLA_DR)]

    qa = _dot(rms(uq_ref[...], qg_ref[...]), wq_ref[...])
    kv = _dot(rms(ukv_ref[...], kg_ref[...]), wkv_ref[...])
    k_rope = (k1_ref[...].astype(F32) * cos + k2_ref[...].astype(F32) * sin).astype(BF16)
    n_rot = MLA_HEADS * MLA_DR
    for h in range(MLA_HEADS):
        c = h // 2
        qr = qa[:, MLA_W + c * LANE:MLA_W + (c + 1) * LANE]
        qs = qa[:, MLA_W + n_rot + c * LANE:MLA_W + n_rot + (c + 1) * LANE]
        rot = jnp.where(half_mask[h % 2], qr * cos + qs * sin, 0.0)
        q_ref[h, :, 0:LANE] = (qa[:, h * LANE:(h + 1) * LANE] * ATTN_SCALE).astype(BF16)
        q_ref[h, :, LANE:2 * LANE] = (rot * ATTN_SCALE).astype(BF16)
        k_ref[h, :, 0:LANE] = kv[:, h * LANE:(h + 1) * LANE].astype(BF16)
        k_ref[h, :, LANE:2 * LANE] = k_rope
        v_ref[h] = kv[:, MLA_W + h * LANE:MLA_W + (h + 1) * LANE].astype(BF16)


def _qkv(u_all, qg, kg, wq, wkv, cos_t, sin_t, l, *, tr=512):
    n_lat = ROWS_L // tr
    per_seq = SEQ // tr

    def tab(i):
        return (jnp.where(i < n_lat, i % per_seq, per_seq), 0)

    return pl.pallas_call(
        _qkv_kernel,
        grid=(ROWS_ALL // tr,),
        in_specs=[
            pl.BlockSpec((tr, Q_RANK), lambda i: (i, U_Q // Q_RANK)),
            pl.BlockSpec((tr, KV_RANK), lambda i: (i, U_KV // KV_RANK)),
            pl.BlockSpec((tr, LANE), lambda i: (i, U_KR1 // LANE)),
            pl.BlockSpec((tr, LANE), lambda i: (i, U_KR2 // LANE)),
            _layer_spec(qg, l), _layer_spec(kg, l), _layer_spec(wq, l), _layer_spec(wkv, l),
            pl.BlockSpec((tr, LANE), tab),
            pl.BlockSpec((tr, LANE), tab),
        ],
        out_specs=[
            pl.BlockSpec((MLA_HEADS, tr, 2 * LANE), lambda i: (0, i, 0)),
            pl.BlockSpec((MLA_HEADS, tr, 2 * LANE), lambda i: (0, i, 0)),
            pl.BlockSpec((MLA_HEADS, tr, MLA_DV), lambda i: (0, i, 0)),
        ],
        out_shape=[
            jax.ShapeDtypeStruct((MLA_HEADS, ROWS_ALL, 2 * LANE), BF16),
            jax.ShapeDtypeStruct((MLA_HEADS, ROWS_ALL, 2 * LANE), BF16),
            jax.ShapeDtypeStruct((MLA_HEADS, ROWS_ALL, MLA_DV), BF16),
        ],
        compiler_params=_cparams(("parallel",)),
        name="qkv",
    )(u_all, u_all, u_all, u_all, qg, kg, wq, wkv, cos_t, sin_t)


def _attn_kernel(*refs, n_src):
    q_ref = refs[0]
    k_refs = refs[1:1 + n_src]
    v_refs = refs[1 + n_src:1 + 2 * n_src]
    g_ref = refs[1 + 2 * n_src]
    o_ref = refs[2 + 2 * n_src]
    for h in range(MLA_HEADS):
        q = q_ref[h]
        s = [_dot_nt(q, k[h]) for k in k_refs]
        m = s[0].max(axis=-1, keepdims=True)
        for si in s[1:]:
            m = jnp.maximum(m, si.max(axis=-1, keepdims=True))
        p = [jnp.exp(si - m) for si in s]
        l = p[0].sum(axis=-1, keepdims=True)
        for pi in p[1:]:
            l = l + pi.sum(axis=-1, keepdims=True)
        acc = _dot(p[0].astype(BF16), v_refs[0][h])
        for pi, v in zip(p[1:], v_refs[1:]):
            acc = acc + _dot(pi.astype(BF16), v[h])
        gate = _silu(g_ref[:, h * LANE:(h + 1) * LANE].astype(F32))
        o_ref[:, h * LANE:(h + 1) * LANE] = (acc / l * gate).astype(o_ref.dtype)


def _attention(q, k, v, u_all, prev, *, latent, tq=256):
    H = MLA_HEADS
    if latent:
        n_q = SEQ // tq
        q_blk = lambda b, i: b * n_q + i
        k_specs = [pl.BlockSpec((H, SEQ, 2 * LANE), lambda b, i: (0, b, 0)),
                   pl.BlockSpec((H, CTX_LEN, 2 * LANE), lambda b, i: (0, ROWS_L // CTX_LEN + b, 0))]
        v_specs = [pl.BlockSpec((H, SEQ, MLA_DV), lambda b, i: (0, b, 0)),
                   pl.BlockSpec((H, CTX_LEN, MLA_DV), lambda b, i: (0, ROWS_L // CTX_LEN + b, 0))]
    else:
        n_q = CTX_LEN // tq
        q_blk = lambda b, i: ROWS_L // tq + b * n_q + i
        k_specs = [pl.BlockSpec((H, CTX_LEN, 2 * LANE), lambda b, i: (0, ROWS_L // CTX_LEN + b, 0))]
        v_specs = [pl.BlockSpec((H, CTX_LEN, MLA_DV), lambda b, i: (0, ROWS_L // CTX_LEN + b, 0))]
    n_src = len(k_specs)
    in_specs = ([pl.BlockSpec((H, tq, 2 * LANE), lambda b, i: (0, q_blk(b, i), 0))] + k_specs + v_specs
                + [pl.BlockSpec((tq, MLA_W), lambda b, i: (q_blk(b, i), U_MLA_G // MLA_W))])
    args = [q] + [k] * n_src + [v] * n_src + [u_all]
    kern = functools.partial(_attn_kernel, n_src=n_src)
    aliases = {}
    if prev is not None:
        in_specs.append(pl.BlockSpec(memory_space=pl.ANY))
        args.append(prev)
        aliases = {len(args) - 1: 0}
        kern = _drop_arg(kern, len(args) - 1)
    return pl.pallas_call(
        kern,
        grid=(BATCH, n_q),
        in_specs=in_specs,
        out_specs=pl.BlockSpec((tq, MLA_W), lambda b, i: (q_blk(b, i), 0)),
        out_shape=jax.ShapeDtypeStruct((ROWS_ALL, MLA_W), BF16),
        input_output_aliases=aliases,
        compiler_params=_cparams(("parallel", "parallel")),
        name="attn_latent" if latent else "attn_ctx",
    )(*args)


_POOL_PAD = 16


def _pool_kernel(x_ref, g_ref, w_ref, sc_ref, o_ref, pad_ref, *, L):
    zeros = jnp.zeros((_POOL_PAD, POOL_GROUP), F32)
    pad_ref[pl.ds(0, _POOL_PAD), :] = zeros
    pad_ref[pl.ds(L + _POOL_PAD, _POOL_PAD), :] = zeros
    t = lax.broadcasted_iota(jnp.int32, (L, 1), 0)
    for gi, win in enumerate(POOL_WINDOWS):
        half = win // 2
        cols = slice(gi * POOL_GROUP, (gi + 1) * POOL_GROUP)
        x = x_ref[:, cols].astype(F32)
        pad_ref[pl.ds(_POOL_PAD, L), :] = x
        acc = pad_ref[pl.ds(_POOL_PAD - half, L), :]
        for j in range(-half + 1, half):
            acc = acc + pad_ref[pl.ds(_POOL_PAD + j, L), :]
        cnt = (jnp.minimum(t + half, L) - jnp.maximum(t - half, 0)).astype(F32)
        dlt = (acc / cnt - x).astype(BF16)
        y = _dot(dlt, w_ref[gi]) * sc_ref[:, cols]
        o_ref[:, cols] = (y * _silu(g_ref[:, cols].astype(F32))).astype(o_ref.dtype)


def _pool(u_all, prev, w_pool, pool_scale, l, *, L, row_blk0):
    in_specs = [
        pl.BlockSpec((L, POOL_W), lambda b: (row_blk0 + b, U_POOL // POOL_W)),
        pl.BlockSpec((L, POOL_W), lambda b: (row_blk0 + b, U_POOL_G // POOL_W)),
        _layer_spec(w_pool, l),
        _layer_spec(pool_scale, l),
    ]
    args = [u_all, u_all, w_pool, pool_scale]
    kern = functools.partial(_pool_kernel, L=L)
    aliases = {}
    if prev is not None:
        in_specs.append(pl.BlockSpec(memory_space=pl.ANY))
        args.append(prev)
        aliases = {len(args) - 1: 0}
        kern = _drop_arg(kern, len(args) - 1)
    return pl.pallas_call(
        kern,
        grid=(BATCH,),
        in_specs=in_specs,
        out_specs=pl.BlockSpec((L, POOL_W), lambda b: (row_blk0 + b, 0)),
        out_shape=jax.ShapeDtypeStruct((ROWS_ALL, POOL_W), BF16),
        scratch_shapes=[pltpu.VMEM((L + 2 * _POOL_PAD, POOL_GROUP), F32)],
        input_output_aliases=aliases,
        compiler_params=_cparams(("parallel",)),
        name=f"pool_{L}",
    )(*args)


def _outproj_kernel(x_ref, hy_ref, at_ref, po_ref, w_ref, gt_ref, fg_ref, o_ref, *, final):
    acc = (_dot(hy_ref[...], w_ref[0:HY_W, :]) + _dot(at_ref[...], w_ref[HY_W:HY_W + MLA_W, :])
           + _dot(po_ref[...], w_ref[HY_W + MLA_W:, :]))
    y = x_ref[...] + gt_ref[...] * acc
    if final:
        y = y * lax.rsqrt(jnp.mean(y * y, axis=-1, keepdims=True) + EPS) * fg_ref[...]
    o_ref[...] = y


def _outproj(x_all, hy, att, po, w_out, mod, fg, l, *, n_row_tiles, final, tm=512):
    d = x_all.shape[1]
    tiles_per_batch = SEQ // tm

    def mod_row(i):
        return jnp.minimum(i // tiles_per_batch, BATCH)

    return pl.pallas_call(
        functools.partial(_outproj_kernel, final=final),
        grid=(n_row_tiles,),
        in_specs=[
            pl.BlockSpec((tm, d), lambda i: (i, 0)),
            pl.BlockSpec((tm, HY_W), lambda i: (i, 0)),
            pl.BlockSpec((tm, MLA_W), lambda i: (i, 0)),
            pl.BlockSpec((tm, POOL_W), lambda i: (i, 0)),
            pl.BlockSpec((None,) + w_out.shape[1:], lambda i: (l, 0, 0), pipeline_mode=pl.Buffered(1)),
            pl.BlockSpec((None, None, 1, d), lambda i: (l, mod_row(i), 0, 2)),
            pl.BlockSpec((1, d), lambda i: (0, 0)),
        ],
        out_specs=pl.BlockSpec((tm, d), lambda i: (i, 0)),
        out_shape=jax.ShapeDtypeStruct((n_row_tiles * tm, d), F32),
        compiler_params=_cparams(("parallel",)),
        name="outproj_final" if final else "outproj",
    )(x_all, hy, att, po, w_out, mod, fg)


def _rope_lane_tables(ident_rows):
    n_rows = SEQ // GRID_W
    row = np.repeat(np.arange(n_rows, dtype=np.float64), GRID_W)
    col = np.tile(np.arange(GRID_W, dtype=np.float64), n_rows)
    n_freq = MLA_DR // 4
    inv = ROPE_BASE ** (-np.arange(n_freq, dtype=np.float64) / n_freq)
    ang = np.concatenate([row[:, None] * inv, col[:, None] * inv], axis=-1)
    cos, sin = np.cos(ang), np.sin(ang)
    cos_t = np.concatenate([cos, cos, cos, cos], axis=-1)
    sin_t = np.concatenate([-sin, sin, -sin, sin], axis=-1)
    cos_t = np.concatenate([cos_t, np.ones((ident_rows, LANE))], axis=0)
    sin_t = np.concatenate([sin_t, np.zeros((ident_rows, LANE))], axis=0)
    return jnp.asarray(cos_t, F32), jnp.asarray(sin_t, F32)


def _dft_tables(P, dtype):
    idx = np.arange(P, dtype=np.int64)
    ang = ((idx[:, None] * idx[None, :]) % (2 * P)).astype(np.float64) * (math.pi / P)
    return jnp.asarray(np.cos(ang), F32).astype(dtype), jnp.asarray(np.sin(ang), F32).astype(dtype)


def _filter_tables(L):
    t = np.linspace(0.0, 1.0, L)[:, None]
    wpos = (2.0 * math.pi / L) * np.arange(L, dtype=np.float64)[:, None]
    bands = np.linspace(1e-4, FILTER_BANDS - 1, FILTER_BANDS)[None, :]
    feats = np.concatenate([t, np.cos(bands * wpos), -np.sin(bands * wpos)], axis=-1)
    feats = np.pad(feats, ((0, 0), (0, LANE - FILTER_EMB)))
    return jnp.asarray(feats, F32), jnp.asarray(t, F32)


def _decay_rates():
    d = np.abs(np.linspace(math.log(DECAY_TARGET) / SLOW_DECAY, math.log(DECAY_TARGET) / FAST_DECAY, HY_W))
    return jnp.asarray(d[None, :], F32)


def _kr_permutation():
    p = np.zeros((2 * LANE, MLA_DR), np.float32)
    half = MLA_DR // 2
    for grp, odd in enumerate((0, 1, 0, 1, 1, 0, 1, 0)):
        for i in range(half):
            p[grp * half + i, 2 * i + odd] = 1.0
    return jnp.asarray(p, BF16)


_PACK_UNIT = MLA_DR
_PACK_BLK = 4 * _PACK_UNIT
_PACK_KR_STEP = R_OFF_KR // _PACK_BLK


def _pack_w_in_kernel(a0_ref, a1_ref, a2_ref, a3_ref, p_ref, o_ref):
    i = pl.program_id(1)

    @pl.when(i != _PACK_KR_STEP)
    def _():
        for k, a_ref in enumerate((a0_ref, a1_ref, a2_ref, a3_ref)):
            o_ref[pl.ds(k * _PACK_UNIT, _PACK_UNIT), :] = a_ref[...].astype(BF16)

    @pl.when(i == _PACK_KR_STEP)
    def _():
        o_ref[...] = _dot(p_ref[...], a0_ref[...].astype(BF16)).astype(BF16)


def _pack_w_in(w_in_t):
    depth, n, d = w_in_t.shape
    assert R_OFF_KR % _PACK_BLK == 0 and U_KR1 == R_OFF_KR and U_MLA_G == U_KR1 + _PACK_BLK
    perm = _kr_permutation()

    def src(k):
        def index_map(l, i):
            unit = jnp.where(i < _PACK_KR_STEP, 4 * i + k,
                             jnp.where(i == _PACK_KR_STEP, R_OFF_KR // _PACK_UNIT, 4 * i + k - 3))
            return (l, unit, 0)
        return pl.BlockSpec((None, _PACK_UNIT, d), index_map)

    return pl.pallas_call(
        _pack_w_in_kernel,
        grid=(depth, U_W // _PACK_BLK),
        in_specs=[src(0), src(1), src(2), src(3), pl.BlockSpec(perm.shape, lambda l, i: (0, 0))],
        out_specs=pl.BlockSpec((None, _PACK_BLK, d), lambda l, i: (l, i, 0)),
        out_shape=jax.ShapeDtypeStruct((depth, U_W, d), BF16),
        compiler_params=_cparams(("parallel", "parallel")),
        name="pack_w_in",
    )(w_in_t, w_in_t, w_in_t, w_in_t, perm)


def _pack_w_uq(w_uq):
    w = w_uq.reshape(DEPTH, Q_RANK, MLA_HEADS, MLA_DN + MLA_DR)
    nope = w[..., :MLA_DN].reshape(DEPTH, Q_RANK, MLA_W)
    a, b = w[..., MLA_DN::2], w[..., MLA_DN + 1::2]
    rot = jnp.concatenate([a, b], axis=-1).reshape(DEPTH, Q_RANK, MLA_HEADS * MLA_DR)
    swp = jnp.concatenate([b, a], axis=-1).reshape(DEPTH, Q_RANK, MLA_HEADS * MLA_DR)
    return jnp.concatenate([nope, rot, swp], axis=-1).astype(BF16)


def _pack_w_ukv(w_ukv):
    w = w_ukv.reshape(DEPTH, KV_RANK, MLA_HEADS, MLA_DN + MLA_DV)
    return jnp.concatenate([w[..., :MLA_DN].reshape(DEPTH, KV_RANK, MLA_W),
                            w[..., MLA_DN:].reshape(DEPTH, KV_RANK, MLA_W)], axis=-1).astype(BF16)


def kernel(x, c, ctx, c_ctx, norm_g, w_ada, b_ada, w_in, hy_conv_w, hy_conv_b, hf_w1, hf_b1, hf_freq,
           hf_w2, hf_b2, hf_w3, hy_bias, q_norm_g, w_uq, kv_norm_g, w_ukv, w_pool, pool_scale, w_out,
           final_norm_g):
    assert x.shape == (BATCH, SEQ, D_MODEL) and ctx.shape == (BATCH, CTX_LEN, D_MODEL)

    qkv_tr = 512
    cos_t, sin_t = _rope_lane_tables(qkv_tr)

    w_in_p = _pack_w_in(jnp.swapaxes(w_in, 1, 2))
    w_q_p = _pack_w_uq(w_uq)
    w_kv_p = _pack_w_ukv(w_ukv)
    w_out_b = w_out.astype(BF16)
    w_pool_b = w_pool.astype(BF16)
    w1_p = jnp.pad(hf_w1, ((0, 0), (0, LANE - FILTER_EMB), (0, 0)))
    w3_p = hf_w3.reshape(DEPTH, FILTER_HIDDEN, 2, 2, HY_W).transpose(0, 2, 3, 1, 4)
    conv_w = hy_conv_w.reshape(DEPTH, 3, 3, HY_W)
    conv_b = hy_conv_b.reshape(DEPTH, 3, HY_W)
    rows = lambda a: a[:, None, :]
    norm_g3, qg3, kg3, ps3 = rows(norm_g), rows(q_norm_g), rows(kv_norm_g), rows(pool_scale)
    b1_3, fr_3, b2_3 = rows(hf_b1), rows(hf_freq), rows(hf_b2)

    cond = jnp.concatenate([c, c_ctx[None], jnp.zeros((8 - BATCH - 1, D_MODEL), F32)], axis=0)
    mod = _adaln(cond, w_ada, b_ada[:, None, :])
    mod = mod.reshape(DEPTH, 8, 1, 3 * D_MODEL)

    x_all = jnp.concatenate([x.reshape(ROWS_L, D_MODEL), ctx.reshape(ROWS_C, D_MODEL)], axis=0)

    for l in range(DEPTH):
        last = l == DEPTH - 1
        u_all = _inproj(x_all, norm_g3, mod, w_in_p, l, n_row_tiles=ROWS_ALL // 1024)
        q, k, v = _qkv(u_all, qg3, kg3, w_q_p, w_kv_p, cos_t, sin_t, l, tr=qkv_tr)
        hy = att = po = None
        for L, row_blk0, latent in ((SEQ, 0, True), (CTX_LEN, ROWS_L // CTX_LEN, False)):
            if last and not latent:
                continue
            g, gn = _hyena_filter(L, w1_p, b1_3, fr_3, hf_w2, b2_3, w3_p, l)
            hy = _hyena(u_all, hy, conv_w, conv_b, hy_bias, g, gn, l, L=L, row_blk0=row_blk0)
            att = _attention(q, k, v, u_all, att, latent=latent, tq=256)
            po = _pool(u_all, po, w_pool_b, ps3, l, L=L, row_blk0=row_blk0)
        n_tiles = (ROWS_L if last else ROWS_ALL) // 512
        x_all = _outproj(x_all, hy, att, po, w_out_b, mod, final_norm_g[None, :], l,
                         n_row_tiles=n_tiles, final=last)
    return x_all.reshape(BATCH, SEQ, D_MODEL)
```

```python
import functools
import math

import jax
import jax.numpy as jnp
import numpy as np
from jax import lax
from jax.experimental import pallas as pl
from jax.experimental.pallas import tpu as pltpu

F32 = jnp.float32
BF16 = jnp.bfloat16

D_MODEL = 2048
BATCH = 4
SEQ = 2048
DEPTH = 4
CTX_LEN = 256
GRID_W = 64
EPS = 1e-6

HY_W = 512
MLA_HEADS = 8
MLA_DN = 128
MLA_DR = 64
MLA_DV = 128
MLA_W = MLA_HEADS * MLA_DV
Q_RANK = 512
KV_RANK = 256
POOL_W = 512
POOL_WINDOWS = (2, 4, 8, 16)
POOL_GROUP = 128

FILTER_EMB = 33
FILTER_BANDS = 16
FILTER_HIDDEN = 64
DECAY_TARGET = 1e-2
FAST_DECAY = 0.3
SLOW_DECAY = 1.5
ROPE_BASE = 10000.0
ATTN_SCALE = (MLA_DN + MLA_DR) ** -0.5

R_OFF_KR = 2816
R_OFF_MLA_G = 2880

U_HY = 0
U_Q = 2048
U_KV = 2560
U_KR1 = 2816
U_KR2 = 2944
U_MLA_G = 3072
U_POOL = 4096
U_POOL_G = 4608
U_W = 5120

ROWS_L = BATCH * SEQ
ROWS_C = BATCH * CTX_LEN
ROWS_ALL = ROWS_L + ROWS_C

VMEM_LIMIT_BYTES = 56 * 1024 * 1024
LANE = 128


def _cparams(sem):
    return pltpu.CompilerParams(dimension_semantics=sem, vmem_limit_bytes=VMEM_LIMIT_BYTES)


def _silu(x):
    return x * jax.nn.sigmoid(x)


def _dot(a, b):
    return jnp.dot(a, b, preferred_element_type=F32)


def _dot_nt(a, b):
    return lax.dot_general(a, b, (((1,), (1,)), ((), ())), preferred_element_type=F32)


def _adaln_kernel(c_ref, w_ref, b_ref, o_ref):
    a = _silu(c_ref[...]).astype(BF16)
    o_ref[...] = _dot(a, w_ref[...].astype(BF16)) + b_ref[...]


def _adaln(cond, w_ada, b_ada, tn=1024):
    depth, d, n = w_ada.shape
    return pl.pallas_call(
        _adaln_kernel,
        grid=(depth, n // tn),
        in_specs=[
            pl.BlockSpec((8, d), lambda l, j: (0, 0)),
            pl.BlockSpec((None, d, tn), lambda l, j: (l, 0, j)),
            pl.BlockSpec((None, 1, tn), lambda l, j: (l, 0, j)),
        ],
        out_specs=pl.BlockSpec((None, 8, tn), lambda l, j: (l, 0, j)),
        out_shape=jax.ShapeDtypeStruct((depth, 8, n), F32),
        compiler_params=_cparams(("parallel", "parallel")),
        name="adaln",
    )(cond, w_ada, b_ada)


def _inproj_kernel(x_ref, g_ref, sh_ref, sc_ref, w_ref, o_ref, h_ref, *, sub):
    @pl.when(pl.program_id(1) == 0)
    def _():
        g = g_ref[...]
        sc = 1.0 + sc_ref[...]
        sh = sh_ref[...]
        for r in range(0, x_ref.shape[0], sub):
            x = x_ref[pl.ds(r, sub), :]
            ms = jnp.mean(x * x, axis=-1, keepdims=True)
            y = x * lax.rsqrt(ms + EPS) * g
            h_ref[pl.ds(r, sub), :] = (y * sc + sh).astype(BF16)

    o_ref[...] = _dot_nt(h_ref[...], w_ref[...]).astype(o_ref.dtype)


def _layer_spec(arr, l):
    zeros = (0,) * (arr.ndim - 1)
    return pl.BlockSpec((None,) + arr.shape[1:], lambda *_: (l,) + zeros)


def _inproj(x_all, g, mod, w_t, l, *, n_row_tiles, tm=1024, tn=1280):
    d = x_all.shape[1]
    n = w_t.shape[1]
    tiles_per_batch = SEQ // tm

    def mod_row(i):
        return jnp.minimum(i // tiles_per_batch, BATCH)

    return pl.pallas_call(
        functools.partial(_inproj_kernel, sub=256),
        grid=(n_row_tiles, n // tn),
        in_specs=[
            pl.BlockSpec((tm, d), lambda i, j: (i, 0)),
            _layer_spec(g, l),
            pl.BlockSpec((None, None, 1, d), lambda i, j: (l, mod_row(i), 0, 0)),
            pl.BlockSpec((None, None, 1, d), lambda i, j: (l, mod_row(i), 0, 1)),
            pl.BlockSpec((None, tn, d), lambda i, j: (l, j, 0)),
        ],
        out_specs=pl.BlockSpec((tm, tn), lambda i, j: (i, j)),
        out_shape=jax.ShapeDtypeStruct((x_all.shape[0], n), BF16),
        scratch_shapes=[pltpu.VMEM((tm, d), BF16)],
        compiler_params=_cparams(("parallel", "arbitrary")),
        name="inproj",
    )(x_all, g, mod, mod, w_t)


def _hyena_block(L):
    return min(L, 512)


def _filter_kernel(feats_ref, t_ref, dl_ref, w1_ref, b1_ref, fr_ref, w2_ref, b2_ref, w3_ref,
                   c_ref, s_ref, g_ref, gn_ref, h_ref, *, L, P):
    hp = lax.Precision.HIGHEST
    hdot = lambda a, b: jnp.dot(a, b, precision=hp, preferred_element_type=F32)
    nb = L // P
    m_fft = 2 * P

    @pl.when(pl.program_id(0) == 0)
    def _():
        fr = fr_ref[...]
        h1 = jnp.sin(fr * (hdot(feats_ref[...], w1_ref[...]) + b1_ref[...]))
        h_ref[...] = jnp.sin(fr * (hdot(h1, w2_ref[...]) + b2_ref[...]))

    h = h_ref[...]
    decay = jnp.exp(-t_ref[...] * dl_ref[...])
    row = lax.broadcasted_iota(jnp.int32, (L, 1), 0)
    rp = lax.broadcasted_iota(jnp.int32, (P, 1), 0)
    sgn = jnp.where(rp % 2 == 0, 1.0, -1.0).astype(F32)
    wgt = jnp.where(rp == 0, 1.0 / m_fft, 2.0 / m_fft).astype(F32)
    cm = c_ref[...]
    sm = s_ref[...]
    hf = hdot(h, w3_ref[0]) * decay
    hb = hdot(h, w3_ref[1]) * decay
    hb = jnp.where(row == 0, 0.0, hb)
    nrm = lax.rsqrt(jnp.sum(hf * hf + hb * hb, axis=0, keepdims=True) + EPS)
    F, B = [], []
    for arr, out in ((hf * nrm, F), (hb * nrm, B)):
        for j in range(nb):
            blk = arr[j * P:(j + 1) * P].astype(BF16)
            b32 = blk.astype(F32)
            out.append((_dot(cm, blk), _dot(sm, blk), jnp.sum(b32 * sgn, axis=0, keepdims=True), b32[0:1]))
    for d in range(-(nb - 1), nb):
        if d >= 1:
            gr = F[d][0] + sgn * (F[d - 1][0] - F[d - 1][3])
            gi = F[d][1] + sgn * F[d - 1][1]
            gn = F[d][2] + F[d - 1][2] - F[d - 1][3]
        elif d == 0:
            gr = F[0][0] + B[0][0]
            gi = F[0][1] - B[0][1]
            gn = F[0][2] + B[0][2]
        else:
            e = -d
            gr = B[e][0] + sgn * (B[e - 1][0] - B[e - 1][3])
            gi = -B[e][1] - sgn * B[e - 1][1]
            gn = B[e][2] + B[e - 1][2] - B[e - 1][3]
        g_ref[d + nb - 1, 0] = gr * wgt
        g_ref[d + nb - 1, 1] = gi * wgt
        gn_ref[d + nb - 1] = gn * (1.0 / m_fft)


def _hyena_filter(L, w1, b1, fr, w2, b2, w3, l, *, cb=256):
    P = _hyena_block(L)
    nd = 2 * (L // P) - 1
    feats, t = _filter_tables(L)
    deltas = _decay_rates()
    cmat, smat = _dft_tables(P, BF16)
    full2 = lambda s: (0, 0)
    return pl.pallas_call(
        functools.partial(_filter_kernel, L=L, P=P),
        grid=(2 * (HY_W // cb),),
        in_specs=[
            pl.BlockSpec(feats.shape, full2),
            pl.BlockSpec(t.shape, full2),
            pl.BlockSpec((1, cb), lambda s: (0, s // 2)),
            _layer_spec(w1, l), _layer_spec(b1, l), _layer_spec(fr, l), _layer_spec(w2, l), _layer_spec(b2, l),
            pl.BlockSpec((None, None, 2, FILTER_HIDDEN, cb), lambda s: (l, s % 2, 0, 0, s // 2)),
            pl.BlockSpec((P, P), full2),
            pl.BlockSpec((P, P), full2),
        ],
        out_specs=[
            pl.BlockSpec((None, nd, 2, P, cb), lambda s: (s % 2, 0, 0, 0, s // 2)),
            pl.BlockSpec((None, nd, 1, cb), lambda s: (s % 2, 0, 0, s // 2)),
        ],
        out_shape=[
            jax.ShapeDtypeStruct((2, nd, 2, P, HY_W), F32),
            jax.ShapeDtypeStruct((2, nd, 1, HY_W), F32),
        ],
        scratch_shapes=[pltpu.VMEM((L, FILTER_HIDDEN), F32)],
        compiler_params=_cparams(("arbitrary",)),
        name=f"hyena_filter_{L}",
    )(feats, t, deltas, w1, b1, fr, w2, b2, w3, cmat, smat)


_PAD = 8


_MAC_ROWS = 32


def _hyena_kernel(v_ref, x1_ref, x2_ref, gate_ref, cw_ref, cb_ref, hb_ref, g_ref, gn_ref,
                  c_ref, s_ref, o_ref, pad_ref, z_ref, zb_ref, xc_ref, zf_ref, yr_ref, yi_ref, *, L, P):
    cb = z_ref.shape[1]
    nb = L // P
    zeros = jnp.zeros((_PAD, cb), F32)
    pad_ref[pl.ds(0, _PAD), :] = zeros
    pad_ref[pl.ds(L + _PAD, _PAD), :] = zeros

    def short_conv(src_ref, p):
        pad_ref[pl.ds(_PAD, L), :] = src_ref[...].astype(F32)
        w = cw_ref[:, p, :]
        return (pad_ref[pl.ds(_PAD - 1, L), :] * w[0:1] + pad_ref[pl.ds(_PAD, L), :] * w[1:2]
                + pad_ref[pl.ds(_PAD + 1, L), :] * w[2:3] + cb_ref[p:p + 1, :])

    sign = jnp.where(lax.broadcasted_iota(jnp.int32, (P, 1), 0) % 2 == 0, 1.0, -1.0).astype(F32)
    cm = c_ref[...]
    sm = s_ref[...]

    z_ref[...] = short_conv(v_ref, 0)
    for o in range(2):
        xc_ref[...] = short_conv(x1_ref if o == 0 else x2_ref, 1 + o)
        zb_ref[...] = z_ref[...].astype(BF16)
        zn = []
        for j in range(nb):
            rows = pl.ds(j * P, P)
            zf_ref[0, j] = _dot(cm, zb_ref[rows, :])
            zf_ref[1, j] = _dot(sm, zb_ref[rows, :])
            zn.append(jnp.sum(z_ref[rows, :] * sign, axis=0, keepdims=True))
        bias = hb_ref[o:o + 1, :]
        for i in range(nb):
            def mac(r, carry):
                rr = pl.ds(pl.multiple_of(r * _MAC_ROWS, _MAC_ROWS), _MAC_ROWS)
                yr = yi = None
                for j in range(nb):
                    d = i - j + nb - 1
                    gr, gi = g_ref[o, d, 0, rr, :], g_ref[o, d, 1, rr, :]
                    zr, zi = zf_ref[0, j, rr, :], zf_ref[1, j, rr, :]
                    pr, pi = gr * zr - gi * zi, gr * zi + gi * zr
                    yr, yi = (pr, pi) if yr is None else (yr + pr, yi + pi)
                yr_ref[rr, :] = yr.astype(BF16)
                yi_ref[rr, :] = yi.astype(BF16)
                return carry

            lax.fori_loop(0, P // _MAC_ROWS, mac, 0)
            yn = gn_ref[o, i + nb - 1] * zn[0]
            for j in range(1, nb):
                yn = yn + gn_ref[o, i - j + nb - 1] * zn[j]
            rows = pl.ds(i * P, P)
            y = _dot(cm, yr_ref[...]) + _dot(sm, yi_ref[...])
            y = y + sign * yn + z_ref[rows, :] * bias
            y = xc_ref[rows, :] * y
            if o == 0:
                z_ref[rows, :] = y
            else:
                o_ref[rows, :] = (y * _silu(gate_ref[rows, :].astype(F32))).astype(o_ref.dtype)


def _hyena(u_all, conv_w, conv_b, hy_bias, g, gn, l, *, L, row_blk0, cb=256):
    P = _hyena_block(L)
    nb = L // P
    nd = 2 * nb - 1
    ncb = HY_W // cb
    cmat, smat = _dft_tables(P, BF16)

    def ublk(part):
        return pl.BlockSpec((L, cb), lambda j, b: (row_blk0 + b, part * ncb + j))

    in_specs = [
        ublk(0), ublk(1), ublk(2), ublk(3),
        pl.BlockSpec((None, 3, 3, cb), lambda j, b: (l, 0, 0, j)),
        pl.BlockSpec((None, 3, cb), lambda j, b: (l, 0, j)),
        pl.BlockSpec((None, 2, cb), lambda j, b: (l, 0, j)),
        pl.BlockSpec((2, nd, 2, P, cb), lambda j, b: (0, 0, 0, 0, j), pipeline_mode=pl.Buffered(1)),
        pl.BlockSpec((2, nd, 1, cb), lambda j, b: (0, 0, 0, j)),
        pl.BlockSpec((P, P), lambda j, b: (0, 0)),
        pl.BlockSpec((P, P), lambda j, b: (0, 0)),
    ]
    args = [u_all, u_all, u_all, u_all, conv_w, conv_b, hy_bias, g, gn, cmat, smat]
    return pl.pallas_call(
        functools.partial(_hyena_kernel, L=L, P=P),
        grid=(ncb, BATCH),
        in_specs=in_specs,
        out_specs=pl.BlockSpec((L, cb), lambda j, b: (b, j)),
        out_shape=jax.ShapeDtypeStruct((BATCH * L, HY_W), BF16),
        scratch_shapes=[
            pltpu.VMEM((L + 2 * _PAD, cb), F32),
            pltpu.VMEM((L, cb), F32),
            pltpu.VMEM((L, cb), BF16),
            pltpu.VMEM((L, cb), F32),
            pltpu.VMEM((2, nb, P, cb), F32),
            pltpu.VMEM((P, cb), BF16),
            pltpu.VMEM((P, cb), BF16),
        ],
        compiler_params=_cparams(("parallel", "parallel")),
        name=f"hyena_{L}",
    )(*args)


_Q_SCALE = ATTN_SCALE * math.log2(math.e)
_VT_ROWS = MLA_DV + 16


def _qkv_kernel(uq_ref, ukv_ref, k1_ref, k2_ref, qg_ref, kg_ref, wq_ref, wk_ref, wvt_ref, cos_ref, sin_ref,
                q_ref, k_ref, vt_ref):
    def rms(x, g):
        x = x.astype(F32)
        return (x * lax.rsqrt(jnp.mean(x * x, axis=-1, keepdims=True) + EPS) * g).astype(BF16)

    cos = cos_ref[...]
    sin = sin_ref[...]
    lane = lax.broadcasted_iota(jnp.int32, (1, LANE), 1)
    half_mask = [(lane < MLA_DR), (lane >= MLA_DR)]

    qa = _dot(rms(uq_ref[...], qg_ref[...]), wq_ref[...])
    kvn = rms(ukv_ref[...], kg_ref[...])
    kn = _dot(kvn, wk_ref[...])
    vt = _dot_nt(wvt_ref[...], kvn)
    k_rope = (k1_ref[...].astype(F32) * cos + k2_ref[...].astype(F32) * sin).astype(BF16)
    n_rot = MLA_HEADS * MLA_DR
    for h in range(MLA_HEADS):
        c = h // 2
        qr = qa[:, MLA_W + c * LANE:MLA_W + (c + 1) * LANE]
        qs = qa[:, MLA_W + n_rot + c * LANE:MLA_W + n_rot + (c + 1) * LANE]
        rot = jnp.where(half_mask[h % 2], qr * cos + qs * sin, 0.0)
        q_ref[h, :, 0:LANE] = (qa[:, h * LANE:(h + 1) * LANE] * _Q_SCALE).astype(BF16)
        q_ref[h, :, LANE:2 * LANE] = (rot * _Q_SCALE).astype(BF16)
        k_ref[h, :, 0:LANE] = kn[:, h * LANE:(h + 1) * LANE].astype(BF16)
        k_ref[h, :, LANE:2 * LANE] = k_rope
        vt_ref[h, 0:MLA_DV, :] = vt[h * MLA_DV:(h + 1) * MLA_DV, :].astype(BF16)
        vt_ref[h, MLA_DV:_VT_ROWS, :] = jnp.ones((_VT_ROWS - MLA_DV, vt.shape[1]), BF16)


def _qkv(u_all, qg, kg, wq, wk, wvt, cos_t, sin_t, l, *, tr=512):
    n_lat = ROWS_L // tr
    per_seq = SEQ // tr

    def tab(i):
        return (jnp.where(i < n_lat, i % per_seq, per_seq), 0)

    return pl.pallas_call(
        _qkv_kernel,
        grid=(ROWS_ALL // tr,),
        in_specs=[
            pl.BlockSpec((tr, Q_RANK), lambda i: (i, U_Q // Q_RANK)),
            pl.BlockSpec((tr, KV_RANK), lambda i: (i, U_KV // KV_RANK)),
            pl.BlockSpec((tr, LANE), lambda i: (i, U_KR1 // LANE)),
            pl.BlockSpec((tr, LANE), lambda i: (i, U_KR2 // LANE)),
            _layer_spec(qg, l), _layer_spec(kg, l), _layer_spec(wq, l), _layer_spec(wk, l), _layer_spec(wvt, l),
            pl.BlockSpec((tr, LANE), tab),
            pl.BlockSpec((tr, LANE), tab),
        ],
        out_specs=[
            pl.BlockSpec((MLA_HEADS, tr, 2 * LANE), lambda i: (0, i, 0)),
            pl.BlockSpec((MLA_HEADS, tr, 2 * LANE), lambda i: (0, i, 0)),
            pl.BlockSpec((MLA_HEADS, _VT_ROWS, tr), lambda i: (0, 0, i)),
        ],
        out_shape=[
            jax.ShapeDtypeStruct((MLA_HEADS, ROWS_ALL, 2 * LANE), BF16),
            jax.ShapeDtypeStruct((MLA_HEADS, ROWS_ALL, 2 * LANE), BF16),
            jax.ShapeDtypeStruct((MLA_HEADS, _VT_ROWS, ROWS_ALL), BF16),
        ],
        compiler_params=_cparams(("parallel",)),
        name="qkv",
    )(u_all, u_all, u_all, u_all, qg, kg, wq, wk, wvt, cos_t, sin_t)


def _fold_max(x):
    rows = x.shape[0]
    while rows > 8 and rows % 16 == 0:
        rows //= 2
        x = jnp.maximum(x[:rows], x[rows:])
    return x.max(axis=0, keepdims=True)


def _attn_kernel(*refs, n_src):
    q_ref = refs[0]
    k_refs = refs[1:1 + n_src]
    vt_refs = refs[1 + n_src:1 + 2 * n_src]
    g_ref = refs[1 + 2 * n_src]
    o_ref = refs[2 + 2 * n_src]
    def scores(h):
        return [_dot_nt(k[h], q_ref[h]) for k in k_refs]

    s_next = scores(0)
    for h in range(MLA_HEADS):
        s = s_next
        if h + 1 < MLA_HEADS:
            s_next = scores(h + 1)
        m = _fold_max(s[0])
        for si in s[1:]:
            m = jnp.maximum(m, _fold_max(si))
        p = [jnp.exp2((si - m).astype(BF16)) for si in s]
        acc = _dot(vt_refs[0][h], p[0])
        for pi, vt in zip(p[1:], vt_refs[1:]):
            acc = acc + _dot(vt[h], pi)
        out = (acc[0:MLA_DV] / acc[MLA_DV:MLA_DV + 1]).T
        gate = _silu(g_ref[:, h * LANE:(h + 1) * LANE].astype(F32))
        o_ref[:, h * LANE:(h + 1) * LANE] = (out * gate).astype(o_ref.dtype)


def _attention(q, k, vt, u_all, *, latent, tq=256):
    H = MLA_HEADS
    ctx_blk0 = ROWS_L // CTX_LEN
    if latent:
        n_q = SEQ // tq
        q_off = 0
        k_specs = [pl.BlockSpec((H, SEQ, 2 * LANE), lambda b, i: (0, b, 0)),
                   pl.BlockSpec((H, CTX_LEN, 2 * LANE), lambda b, i: (0, ctx_blk0 + b, 0))]
        v_specs = [pl.BlockSpec((H, _VT_ROWS, SEQ), lambda b, i: (0, 0, b)),
                   pl.BlockSpec((H, _VT_ROWS, CTX_LEN), lambda b, i: (0, 0, ctx_blk0 + b))]
    else:
        n_q = CTX_LEN // tq
        q_off = ROWS_L // tq
        k_specs = [pl.BlockSpec((H, CTX_LEN, 2 * LANE), lambda b, i: (0, ctx_blk0 + b, 0))]
        v_specs = [pl.BlockSpec((H, _VT_ROWS, CTX_LEN), lambda b, i: (0, 0, ctx_blk0 + b))]
    n_src = len(k_specs)
    in_specs = ([pl.BlockSpec((H, tq, 2 * LANE), lambda b, i: (0, q_off + b * n_q + i, 0))] + k_specs + v_specs
                + [pl.BlockSpec((tq, MLA_W), lambda b, i: (q_off + b * n_q + i, U_MLA_G // MLA_W))])
    args = [q] + [k] * n_src + [vt] * n_src + [u_all]
    return pl.pallas_call(
        functools.partial(_attn_kernel, n_src=n_src),
        grid=(BATCH, n_q),
        in_specs=in_specs,
        out_specs=pl.BlockSpec((tq, MLA_W), lambda b, i: (b * n_q + i, 0)),
        out_shape=jax.ShapeDtypeStruct((BATCH * n_q * tq, MLA_W), BF16),
        compiler_params=_cparams(("parallel", "parallel")),
        name="attn_latent" if latent else "attn_ctx",
    )(*args)


_POOL_PAD = 16


def _pool_kernel(x_ref, g_ref, w_ref, sc_ref, o_ref, pad_ref, *, L):
    zeros = jnp.zeros((_POOL_PAD, POOL_GROUP), F32)
    pad_ref[pl.ds(0, _POOL_PAD), :] = zeros
    pad_ref[pl.ds(L + _POOL_PAD, _POOL_PAD), :] = zeros
    t = lax.broadcasted_iota(jnp.int32, (L, 1), 0)
    for gi, win in enumerate(POOL_WINDOWS):
        half = win // 2
        cols = slice(gi * POOL_GROUP, (gi + 1) * POOL_GROUP)
        x = x_ref[:, cols].astype(F32)
        pad_ref[pl.ds(_POOL_PAD, L), :] = x
        acc = pad_ref[pl.ds(_POOL_PAD - half, L), :]
        for j in range(-half + 1, half):
            acc = acc + pad_ref[pl.ds(_POOL_PAD + j, L), :]
        cnt = (jnp.minimum(t + half, L) - jnp.maximum(t - half, 0)).astype(F32)
        dlt = (acc / cnt - x).astype(BF16)
        y = _dot(dlt, w_ref[gi]) * sc_ref[:, cols]
        o_ref[:, cols] = (y * _silu(g_ref[:, cols].astype(F32))).astype(o_ref.dtype)


def _pool(u_all, w_pool, pool_scale, l, *, L, row_blk0):
    return pl.pallas_call(
        functools.partial(_pool_kernel, L=L),
        grid=(BATCH,),
        in_specs=[
            pl.BlockSpec((L, POOL_W), lambda b: (row_blk0 + b, U_POOL // POOL_W)),
            pl.BlockSpec((L, POOL_W), lambda b: (row_blk0 + b, U_POOL_G // POOL_W)),
            _layer_spec(w_pool, l),
            _layer_spec(pool_scale, l),
        ],
        out_specs=pl.BlockSpec((L, POOL_W), lambda b: (b, 0)),
        out_shape=jax.ShapeDtypeStruct((BATCH * L, POOL_W), BF16),
        scratch_shapes=[pltpu.VMEM((L + 2 * _POOL_PAD, POOL_GROUP), F32)],
        compiler_params=_cparams(("parallel",)),
        name=f"pool_{L}",
    )(u_all, u_all, w_pool, pool_scale)


def _outproj_kernel(x_ref, *refs, n_lat_tiles, n_streams, final):
    mixers = [refs[3 * s:3 * s + 3] for s in range(n_streams)]
    w_ref, gt_ref, fg_ref, o_ref = refs[3 * n_streams:]

    def run(hy_ref, at_ref, po_ref):
        acc = (_dot(hy_ref[...], w_ref[0:HY_W, :]) + _dot(at_ref[...], w_ref[HY_W:HY_W + MLA_W, :])
               + _dot(po_ref[...], w_ref[HY_W + MLA_W:, :]))
        y = x_ref[...] + gt_ref[...] * acc
        if final:
            y = y * lax.rsqrt(jnp.mean(y * y, axis=-1, keepdims=True) + EPS) * fg_ref[...]
        o_ref[...] = y

    if n_streams == 1:
        run(*mixers[0])
    else:
        is_latent = pl.program_id(0) < n_lat_tiles
        pl.when(is_latent)(lambda: run(*mixers[0]))
        pl.when(jnp.logical_not(is_latent))(lambda: run(*mixers[1]))


def _outproj(x_all, mixers, w_out, mod, fg, l, *, final, tm=512):
    d = x_all.shape[1]
    tiles_per_batch = SEQ // tm
    n_lat_tiles = ROWS_L // tm
    n_row_tiles = sum(m[0].shape[0] for m in mixers) // tm

    def mod_row(i):
        return jnp.minimum(i // tiles_per_batch, BATCH)

    lat_blk = lambda i: (jnp.minimum(i, n_lat_tiles - 1), 0)
    ctx_blk = lambda i: (jnp.maximum(i - n_lat_tiles, 0), 0)
    mixer_specs, mixer_args = [], []
    for blk, (hy, att, po) in zip((lat_blk, ctx_blk), mixers):
        mixer_specs += [pl.BlockSpec((tm, HY_W), blk), pl.BlockSpec((tm, MLA_W), blk), pl.BlockSpec((tm, POOL_W), blk)]
        mixer_args += [hy, att, po]

    return pl.pallas_call(
        functools.partial(_outproj_kernel, n_lat_tiles=n_lat_tiles, n_streams=len(mixers), final=final),
        grid=(n_row_tiles,),
        in_specs=[pl.BlockSpec((tm, d), lambda i: (i, 0))] + mixer_specs + [
            pl.BlockSpec((None,) + w_out.shape[1:], lambda i: (l, 0, 0), pipeline_mode=pl.Buffered(1)),
            pl.BlockSpec((None, None, 1, d), lambda i: (l, mod_row(i), 0, 2)),
            pl.BlockSpec((1, d), lambda i: (0, 0)),
        ],
        out_specs=pl.BlockSpec((tm, d), lambda i: (i, 0)),
        out_shape=jax.ShapeDtypeStruct((n_row_tiles * tm, d), F32),
        compiler_params=_cparams(("parallel",)),
        name="outproj_final" if final else "outproj",
    )(x_all, *mixer_args, w_out, mod, fg)


def _rope_lane_tables(ident_rows):
    n_rows = SEQ // GRID_W
    row = np.repeat(np.arange(n_rows, dtype=np.float64), GRID_W)
    col = np.tile(np.arange(GRID_W, dtype=np.float64), n_rows)
    n_freq = MLA_DR // 4
    inv = ROPE_BASE ** (-np.arange(n_freq, dtype=np.float64) / n_freq)
    ang = np.concatenate([row[:, None] * inv, col[:, None] * inv], axis=-1)
    cos, sin = np.cos(ang), np.sin(ang)
    cos_t = np.concatenate([cos, cos, cos, cos], axis=-1)
    sin_t = np.concatenate([-sin, sin, -sin, sin], axis=-1)
    cos_t = np.concatenate([cos_t, np.ones((ident_rows, LANE))], axis=0)
    sin_t = np.concatenate([sin_t, np.zeros((ident_rows, LANE))], axis=0)
    return jnp.asarray(cos_t, F32), jnp.asarray(sin_t, F32)


def _dft_tables(P, dtype):
    idx = np.arange(P, dtype=np.int64)
    ang = ((idx[:, None] * idx[None, :]) % (2 * P)).astype(np.float64) * (math.pi / P)
    return jnp.asarray(np.cos(ang), F32).astype(dtype), jnp.asarray(np.sin(ang), F32).astype(dtype)


def _filter_tables(L):
    t = np.linspace(0.0, 1.0, L)[:, None]
    wpos = (2.0 * math.pi / L) * np.arange(L, dtype=np.float64)[:, None]
    bands = np.linspace(1e-4, FILTER_BANDS - 1, FILTER_BANDS)[None, :]
    feats = np.concatenate([t, np.cos(bands * wpos), -np.sin(bands * wpos)], axis=-1)
    feats = np.pad(feats, ((0, 0), (0, LANE - FILTER_EMB)))
    return jnp.asarray(feats, F32), jnp.asarray(t, F32)


def _decay_rates():
    d = np.abs(np.linspace(math.log(DECAY_TARGET) / SLOW_DECAY, math.log(DECAY_TARGET) / FAST_DECAY, HY_W))
    return jnp.asarray(d[None, :], F32)


def _kr_permutation():
    p = np.zeros((2 * LANE, MLA_DR), np.float32)
    half = MLA_DR // 2
    for grp, odd in enumerate((0, 1, 0, 1, 1, 0, 1, 0)):
        for i in range(half):
            p[grp * half + i, 2 * i + odd] = 1.0
    return jnp.asarray(p, BF16)


_PACK_UNIT = MLA_DR
_PACK_BLK = 4 * _PACK_UNIT
_PACK_KR_STEP = R_OFF_KR // _PACK_BLK


def _pack_w_in_kernel(a0_ref, a1_ref, a2_ref, a3_ref, p_ref, o_ref):
    i = pl.program_id(1)

    @pl.when(i != _PACK_KR_STEP)
    def _():
        for k, a_ref in enumerate((a0_ref, a1_ref, a2_ref, a3_ref)):
            o_ref[pl.ds(k * _PACK_UNIT, _PACK_UNIT), :] = a_ref[...].astype(BF16)

    @pl.when(i == _PACK_KR_STEP)
    def _():
        o_ref[...] = _dot(p_ref[...], a0_ref[...].astype(BF16)).astype(BF16)


def _pack_w_in(w_in_t):
    depth, n, d = w_in_t.shape
    assert R_OFF_KR % _PACK_BLK == 0 and U_KR1 == R_OFF_KR and U_MLA_G == U_KR1 + _PACK_BLK
    perm = _kr_permutation()

    def src(k):
        def index_map(l, i):
            unit = jnp.where(i < _PACK_KR_STEP, 4 * i + k,
                             jnp.where(i == _PACK_KR_STEP, R_OFF_KR // _PACK_UNIT, 4 * i + k - 3))
            return (l, unit, 0)
        return pl.BlockSpec((None, _PACK_UNIT, d), index_map)

    return pl.pallas_call(
        _pack_w_in_kernel,
        grid=(depth, U_W // _PACK_BLK),
        in_specs=[src(0), src(1), src(2), src(3), pl.BlockSpec(perm.shape, lambda l, i: (0, 0))],
        out_specs=pl.BlockSpec((None, _PACK_BLK, d), lambda l, i: (l, i, 0)),
        out_shape=jax.ShapeDtypeStruct((depth, U_W, d), BF16),
        compiler_params=_cparams(("parallel", "parallel")),
        name="pack_w_in",
    )(w_in_t, w_in_t, w_in_t, w_in_t, perm)


def _pack_w_uq(w_uq):
    w = w_uq.reshape(DEPTH, Q_RANK, MLA_HEADS, MLA_DN + MLA_DR)
    nope = w[..., :MLA_DN].reshape(DEPTH, Q_RANK, MLA_W)
    a, b = w[..., MLA_DN::2], w[..., MLA_DN + 1::2]
    rot = jnp.concatenate([a, b], axis=-1).reshape(DEPTH, Q_RANK, MLA_HEADS * MLA_DR)
    swp = jnp.concatenate([b, a], axis=-1).reshape(DEPTH, Q_RANK, MLA_HEADS * MLA_DR)
    return jnp.concatenate([nope, rot, swp], axis=-1).astype(BF16)


def _pack_w_ukv(w_ukv):
    w = w_ukv.reshape(DEPTH, KV_RANK, MLA_HEADS, MLA_DN + MLA_DV)
    wk = w[..., :MLA_DN].reshape(DEPTH, KV_RANK, MLA_W)
    wv = w[..., MLA_DN:].reshape(DEPTH, KV_RANK, MLA_W)
    return wk.astype(BF16), jnp.swapaxes(wv, 1, 2).astype(BF16)


def kernel(x, c, ctx, c_ctx, norm_g, w_ada, b_ada, w_in, hy_conv_w, hy_conv_b, hf_w1, hf_b1, hf_freq,
           hf_w2, hf_b2, hf_w3, hy_bias, q_norm_g, w_uq, kv_norm_g, w_ukv, w_pool, pool_scale, w_out,
           final_norm_g):
    assert x.shape == (BATCH, SEQ, D_MODEL) and ctx.shape == (BATCH, CTX_LEN, D_MODEL)

    qkv_tr = 512
    cos_t, sin_t = _rope_lane_tables(qkv_tr)

    w_in_p = _pack_w_in(jnp.swapaxes(w_in, 1, 2))
    w_q_p = _pack_w_uq(w_uq)
    w_k_p, w_vt_p = _pack_w_ukv(w_ukv)
    w_out_b = w_out.astype(BF16)
    w_pool_b = w_pool.astype(BF16)
    w1_p = jnp.pad(hf_w1, ((0, 0), (0, LANE - FILTER_EMB), (0, 0)))
    w3_p = hf_w3.reshape(DEPTH, FILTER_HIDDEN, 2, 2, HY_W).transpose(0, 2, 3, 1, 4)
    conv_w = hy_conv_w.reshape(DEPTH, 3, 3, HY_W)
    conv_b = hy_conv_b.reshape(DEPTH, 3, HY_W)
    rows = lambda a: a[:, None, :]
    norm_g3, qg3, kg3, ps3 = rows(norm_g), rows(q_norm_g), rows(kv_norm_g), rows(pool_scale)
    b1_3, fr_3, b2_3 = rows(hf_b1), rows(hf_freq), rows(hf_b2)

    cond = jnp.concatenate([c, c_ctx[None], jnp.zeros((8 - BATCH - 1, D_MODEL), F32)], axis=0)
    mod = _adaln(cond, w_ada, b_ada[:, None, :])
    mod = mod.reshape(DEPTH, 8, 1, 3 * D_MODEL)

    x_all = jnp.concatenate([x.reshape(ROWS_L, D_MODEL), ctx.reshape(ROWS_C, D_MODEL)], axis=0)

    for l in range(DEPTH):
        last = l == DEPTH - 1
        u_all = _inproj(x_all, norm_g3, mod, w_in_p, l, n_row_tiles=ROWS_ALL // 1024)
        q, k, vt = _qkv(u_all, qg3, kg3, w_q_p, w_k_p, w_vt_p, cos_t, sin_t, l, tr=qkv_tr)
        mixers = []
        for L, row_blk0, latent in ((SEQ, 0, True), (CTX_LEN, ROWS_L // CTX_LEN, False)):
            if last and not latent:
                continue
            g, gn = _hyena_filter(L, w1_p, b1_3, fr_3, hf_w2, b2_3, w3_p, l)
            hy = _hyena(u_all, conv_w, conv_b, hy_bias, g, gn, l, L=L, row_blk0=row_blk0)
            att = _attention(q, k, vt, u_all, latent=latent, tq=256)
            po = _pool(u_all, w_pool_b, ps3, l, L=L, row_blk0=row_blk0)
            mixers.append((hy, att, po))
        x_all = _outproj(x_all, mixers, w_out_b, mod, final_norm_g[None, :], l, final=last)
    return x_all.reshape(BATCH, SEQ, D_MODEL)
```

```python
import functools
import math

import jax
import jax.numpy as jnp
import numpy as np
from jax import lax
from jax.experimental import pallas as pl
from jax.experimental.pallas import tpu as pltpu

F32 = jnp.float32
BF16 = jnp.bfloat16

D_MODEL = 2048
BATCH = 4
SEQ = 2048
DEPTH = 4
CTX_LEN = 256
GRID_W = 64
EPS = 1e-6

HY_W = 512
MLA_HEADS = 8
MLA_DN = 128
MLA_DR = 64
MLA_DV = 128
MLA_W = MLA_HEADS * MLA_DV
Q_RANK = 512
KV_RANK = 256
POOL_W = 512
POOL_WINDOWS = (2, 4, 8, 16)
POOL_GROUP = 128

FILTER_EMB = 33
FILTER_BANDS = 16
FILTER_HIDDEN = 64
DECAY_TARGET = 1e-2
FAST_DECAY = 0.3
SLOW_DECAY = 1.5
ROPE_BASE = 10000.0
ATTN_SCALE = (MLA_DN + MLA_DR) ** -0.5

R_OFF_KR = 2816
R_OFF_MLA_G = 2880

U_HY = 0
U_Q = 2048
U_KV = 2560
U_KR1 = 2816
U_KR2 = 2944
U_MLA_G = 3072
U_POOL = 4096
U_POOL_G = 4608
U_W = 5120

ROWS_L = BATCH * SEQ
ROWS_C = BATCH * CTX_LEN
ROWS_ALL = ROWS_L + ROWS_C

VMEM_LIMIT_BYTES = 56 * 1024 * 1024
LANE = 128


def _cparams(sem):
    return pltpu.CompilerParams(dimension_semantics=sem, vmem_limit_bytes=VMEM_LIMIT_BYTES)


def _silu(x):
    return x * jax.nn.sigmoid(x)


def _dot(a, b):
    return jnp.dot(a, b, preferred_element_type=F32)


def _dot_nt(a, b):
    return lax.dot_general(a, b, (((1,), (1,)), ((), ())), preferred_element_type=F32)


def _adaln_kernel(c_ref, w_ref, b_ref, o_ref):
    a = _silu(c_ref[...]).astype(BF16)
    o_ref[...] = _dot(a, w_ref[...].astype(BF16)) + b_ref[...]


def _adaln(cond, w_ada, b_ada, tn=1024):
    depth, d, n = w_ada.shape
    return pl.pallas_call(
        _adaln_kernel,
        grid=(depth, n // tn),
        in_specs=[
            pl.BlockSpec((8, d), lambda l, j: (0, 0)),
            pl.BlockSpec((None, d, tn), lambda l, j: (l, 0, j)),
            pl.BlockSpec((None, 1, tn), lambda l, j: (l, 0, j)),
        ],
        out_specs=pl.BlockSpec((None, 8, tn), lambda l, j: (l, 0, j)),
        out_shape=jax.ShapeDtypeStruct((depth, 8, n), F32),
        compiler_params=_cparams(("parallel", "parallel")),
        name="adaln",
    )(cond, w_ada, b_ada)


def _inproj_kernel(x_ref, g_ref, sh_ref, sc_ref, w_ref, o_ref, h_ref, *, sub):
    @pl.when(pl.program_id(1) == 0)
    def _():
        gs = g_ref[...] * (1.0 + sc_ref[...])
        sh = sh_ref[...]
        for r in range(0, x_ref.shape[0], sub):
            x = x_ref[pl.ds(r, sub), :]
            ms = jnp.mean(x * x, axis=-1, keepdims=True)
            h_ref[pl.ds(r, sub), :] = (x * lax.rsqrt(ms + EPS) * gs + sh).astype(BF16)

    o_ref[...] = _dot_nt(h_ref[...], w_ref[...]).astype(o_ref.dtype)


def _layer_spec(arr, l):
    zeros = (0,) * (arr.ndim - 1)
    return pl.BlockSpec((None,) + arr.shape[1:], lambda *_: (l,) + zeros)


def _inproj(x_all, g, mod, w_t, l, *, n_row_tiles, tm=1024, tn=1280):
    d = x_all.shape[1]
    n = w_t.shape[1]
    tiles_per_batch = SEQ // tm

    def mod_row(i):
        return jnp.minimum(i // tiles_per_batch, BATCH)

    return pl.pallas_call(
        functools.partial(_inproj_kernel, sub=256),
        grid=(n_row_tiles, n // tn),
        in_specs=[
            pl.BlockSpec((tm, d), lambda i, j: (i, 0)),
            _layer_spec(g, l),
            pl.BlockSpec((None, None, 1, d), lambda i, j: (l, mod_row(i), 0, 0)),
            pl.BlockSpec((None, None, 1, d), lambda i, j: (l, mod_row(i), 0, 1)),
            pl.BlockSpec((None, tn, d), lambda i, j: (l, j, 0)),
        ],
        out_specs=pl.BlockSpec((tm, tn), lambda i, j: (i, j)),
        out_shape=jax.ShapeDtypeStruct((x_all.shape[0], n), BF16),
        scratch_shapes=[pltpu.VMEM((tm, d), BF16)],
        compiler_params=_cparams(("parallel", "arbitrary")),
        name="inproj",
    )(x_all, g, mod, mod, w_t)


def _hyena_block(L):
    return min(L, 512)


def _filter_kernel(feats_ref, t_ref, dl_ref, w1_ref, b1_ref, fr_ref, w2_ref, b2_ref, w3_ref,
                   c_ref, s_ref, g_ref, gn_ref, h_ref, *, L, P):
    hp = lax.Precision.HIGHEST
    hdot = lambda a, b: jnp.dot(a, b, precision=hp, preferred_element_type=F32)
    nb = L // P
    m_fft = 2 * P

    @pl.when(pl.program_id(0) == 0)
    def _():
        fr = fr_ref[...]
        h1 = jnp.sin(fr * (hdot(feats_ref[...], w1_ref[...]) + b1_ref[...]))
        h_ref[...] = jnp.sin(fr * (hdot(h1, w2_ref[...]) + b2_ref[...]))

    h = h_ref[...]
    decay = jnp.exp(-t_ref[...] * dl_ref[...])
    row = lax.broadcasted_iota(jnp.int32, (L, 1), 0)
    rp = lax.broadcasted_iota(jnp.int32, (P, 1), 0)
    sgn = jnp.where(rp % 2 == 0, 1.0, -1.0).astype(F32)
    wgt = jnp.where(rp == 0, 1.0 / m_fft, 2.0 / m_fft).astype(F32)
    cm = c_ref[...]
    sm = s_ref[...]
    hf = hdot(h, w3_ref[0]) * decay
    hb = hdot(h, w3_ref[1]) * decay
    hb = jnp.where(row == 0, 0.0, hb)
    nrm = lax.rsqrt(jnp.sum(hf * hf + hb * hb, axis=0, keepdims=True) + EPS)
    F, B = [], []
    for arr, out in ((hf * nrm, F), (hb * nrm, B)):
        for j in range(nb):
            blk = arr[j * P:(j + 1) * P].astype(BF16)
            b32 = blk.astype(F32)
            out.append((_dot(cm, blk), _dot(sm, blk), jnp.sum(b32 * sgn, axis=0, keepdims=True), b32[0:1]))
    for d in range(-(nb - 1), nb):
        if d >= 1:
            gr = F[d][0] + sgn * (F[d - 1][0] - F[d - 1][3])
            gi = F[d][1] + sgn * F[d - 1][1]
            gn = F[d][2] + F[d - 1][2] - F[d - 1][3]
        elif d == 0:
            gr = F[0][0] + B[0][0]
            gi = F[0][1] - B[0][1]
            gn = F[0][2] + B[0][2]
        else:
            e = -d
            gr = B[e][0] + sgn * (B[e - 1][0] - B[e - 1][3])
            gi = -B[e][1] - sgn * B[e - 1][1]
            gn = B[e][2] + B[e - 1][2] - B[e - 1][3]
        g_ref[d + nb - 1, 0] = gr * wgt
        g_ref[d + nb - 1, 1] = gi * wgt
        gn_ref[d + nb - 1] = gn * (1.0 / m_fft)


def _hyena_filter(L, w1, b1, fr, w2, b2, w3, l, *, cb=256):
    P = _hyena_block(L)
    nd = 2 * (L // P) - 1
    feats, t = _filter_tables(L)
    deltas = _decay_rates()
    cmat, smat = _dft_tables(P, BF16)
    full2 = lambda s: (0, 0)
    return pl.pallas_call(
        functools.partial(_filter_kernel, L=L, P=P),
        grid=(2 * (HY_W // cb),),
        in_specs=[
            pl.BlockSpec(feats.shape, full2),
            pl.BlockSpec(t.shape, full2),
            pl.BlockSpec((1, cb), lambda s: (0, s // 2)),
            _layer_spec(w1, l), _layer_spec(b1, l), _layer_spec(fr, l), _layer_spec(w2, l), _layer_spec(b2, l),
            pl.BlockSpec((None, None, 2, FILTER_HIDDEN, cb), lambda s: (l, s % 2, 0, 0, s // 2)),
            pl.BlockSpec((P, P), full2),
            pl.BlockSpec((P, P), full2),
        ],
        out_specs=[
            pl.BlockSpec((None, nd, 2, P, cb), lambda s: (s % 2, 0, 0, 0, s // 2)),
            pl.BlockSpec((None, nd, 1, cb), lambda s: (s % 2, 0, 0, s // 2)),
        ],
        out_shape=[
            jax.ShapeDtypeStruct((2, nd, 2, P, HY_W), F32),
            jax.ShapeDtypeStruct((2, nd, 1, HY_W), F32),
        ],
        scratch_shapes=[pltpu.VMEM((L, FILTER_HIDDEN), F32)],
        compiler_params=_cparams(("arbitrary",)),
        name=f"hyena_filter_{L}",
    )(feats, t, deltas, w1, b1, fr, w2, b2, w3, cmat, smat)


_MAC_ROWS = 32


def _hyena_kernel(v_ref, x1_ref, x2_ref, gate_ref, cw_ref, cb_ref, hb_ref, g_ref, gn_ref,
                  c_ref, s_ref, o_ref, z_ref, zb_ref, xc_ref, zf_ref, yr_ref, yi_ref, *, L, P):
    nb = L // P

    def short_conv(dst_ref, src_ref, p):
        x = src_ref[...].astype(F32)
        w = cw_ref[:, p, :]
        bias = cb_ref[p:p + 1, :]
        prev = pltpu.roll(x, 1, axis=0)
        nxt = pltpu.roll(x, L - 1, axis=0)
        dst_ref[...] = prev * w[0:1] + x * w[1:2] + nxt * w[2:3] + bias
        dst_ref[0:1, :] = x[0:1] * w[1:2] + x[1:2] * w[2:3] + bias
        dst_ref[L - 1:L, :] = x[L - 2:L - 1] * w[0:1] + x[L - 1:L] * w[1:2] + bias

    sign = jnp.where(lax.broadcasted_iota(jnp.int32, (P, 1), 0) % 2 == 0, 1.0, -1.0).astype(F32)
    cm = c_ref[...]
    sm = s_ref[...]

    short_conv(z_ref, v_ref, 0)
    for o in range(2):
        short_conv(xc_ref, x1_ref if o == 0 else x2_ref, 1 + o)
        zb_ref[...] = z_ref[...].astype(BF16)
        zn = []
        for j in range(nb):
            rows = pl.ds(j * P, P)
            zf_ref[0, j] = _dot(cm, zb_ref[rows, :])
            zf_ref[1, j] = _dot(sm, zb_ref[rows, :])
            zn.append(jnp.sum(z_ref[rows, :] * sign, axis=0, keepdims=True))
        bias = hb_ref[o:o + 1, :]
        for i in range(nb):
            def mac(r, carry):
                rr = pl.ds(pl.multiple_of(r * _MAC_ROWS, _MAC_ROWS), _MAC_ROWS)
                yr = yi = None
                for j in range(nb):
                    d = i - j + nb - 1
                    gr, gi = g_ref[o, d, 0, rr, :], g_ref[o, d, 1, rr, :]
                    zr, zi = zf_ref[0, j, rr, :], zf_ref[1, j, rr, :]
                    pr, pi = gr * zr - gi * zi, gr * zi + gi * zr
                    yr, yi = (pr, pi) if yr is None else (yr + pr, yi + pi)
                yr_ref[rr, :] = yr.astype(BF16)
                yi_ref[rr, :] = yi.astype(BF16)
                return carry

            lax.fori_loop(0, P // _MAC_ROWS, mac, 0)
            yn = gn_ref[o, i + nb - 1] * zn[0]
            for j in range(1, nb):
                yn = yn + gn_ref[o, i - j + nb - 1] * zn[j]
            rows = pl.ds(i * P, P)
            y = _dot(cm, yr_ref[...]) + _dot(sm, yi_ref[...])
            y = y + sign * yn + z_ref[rows, :] * bias
            y = xc_ref[rows, :] * y
            if o == 0:
                z_ref[rows, :] = y
            else:
                o_ref[rows, :] = (y * _silu(gate_ref[rows, :].astype(F32))).astype(o_ref.dtype)


def _hyena(u_all, conv_w, conv_b, hy_bias, g, gn, l, *, L, row_blk0, cb=256):
    P = _hyena_block(L)
    nb = L // P
    nd = 2 * nb - 1
    ncb = HY_W // cb
    cmat, smat = _dft_tables(P, BF16)

    def ublk(part):
        return pl.BlockSpec((L, cb), lambda j, b: (row_blk0 + b, part * ncb + j))

    in_specs = [
        ublk(0), ublk(1), ublk(2), ublk(3),
        pl.BlockSpec((None, 3, 3, cb), lambda j, b: (l, 0, 0, j)),
        pl.BlockSpec((None, 3, cb), lambda j, b: (l, 0, j)),
        pl.BlockSpec((None, 2, cb), lambda j, b: (l, 0, j)),
        pl.BlockSpec((2, nd, 2, P, cb), lambda j, b: (0, 0, 0, 0, j)),
        pl.BlockSpec((2, nd, 1, cb), lambda j, b: (0, 0, 0, j)),
        pl.BlockSpec((P, P), lambda j, b: (0, 0)),
        pl.BlockSpec((P, P), lambda j, b: (0, 0)),
    ]
    args = [u_all, u_all, u_all, u_all, conv_w, conv_b, hy_bias, g, gn, cmat, smat]
    return pl.pallas_call(
        functools.partial(_hyena_kernel, L=L, P=P),
        grid=(ncb, BATCH),
        in_specs=in_specs,
        out_specs=pl.BlockSpec((L, cb), lambda j, b: (b, j)),
        out_shape=jax.ShapeDtypeStruct((BATCH * L, HY_W), BF16),
        scratch_shapes=[
            pltpu.VMEM((L, cb), F32),
            pltpu.VMEM((L, cb), BF16),
            pltpu.VMEM((L, cb), F32),
            pltpu.VMEM((2, nb, P, cb), F32),
            pltpu.VMEM((P, cb), BF16),
            pltpu.VMEM((P, cb), BF16),
        ],
        compiler_params=_cparams(("parallel", "parallel")),
        name=f"hyena_{L}",
    )(*args)


_Q_SCALE = ATTN_SCALE * math.log2(math.e)
_VT_ROWS = MLA_DV + 16


def _qkv_kernel(uq_ref, ukv_ref, k1_ref, k2_ref, qg_ref, kg_ref, wq_ref, wk_ref, wvt_ref, cos_ref, sin_ref,
                q_ref, k_ref, vt_ref):
    def rms(x, g):
        x = x.astype(F32)
        return (x * lax.rsqrt(jnp.mean(x * x, axis=-1, keepdims=True) + EPS) * g).astype(BF16)

    cos = cos_ref[...]
    sin = sin_ref[...]
    lane = lax.broadcasted_iota(jnp.int32, (1, LANE), 1)
    half_mask = [(lane < MLA_DR), (lane >= MLA_DR)]

    qa = _dot(rms(uq_ref[...], qg_ref[...]), wq_ref[...])
    kvn = rms(ukv_ref[...], kg_ref[...])
    kn = _dot(kvn, wk_ref[...])
    vt = _dot_nt(wvt_ref[...], kvn)
    k_rope = (k1_ref[...].astype(F32) * cos + k2_ref[...].astype(F32) * sin).astype(BF16)
    n_rot = MLA_HEADS * MLA_DR
    for h in range(MLA_HEADS):
        c = h // 2
        qr = qa[:, MLA_W + c * LANE:MLA_W + (c + 1) * LANE]
        qs = qa[:, MLA_W + n_rot + c * LANE:MLA_W + n_rot + (c + 1) * LANE]
        rot = jnp.where(half_mask[h % 2], qr * cos + qs * sin, 0.0)
        q_ref[h, :, 0:LANE] = (qa[:, h * LANE:(h + 1) * LANE] * _Q_SCALE).astype(BF16)
        q_ref[h, :, LANE:2 * LANE] = (rot * _Q_SCALE).astype(BF16)
        k_ref[h, :, 0:LANE] = kn[:, h * LANE:(h + 1) * LANE].astype(BF16)
        k_ref[h, :, LANE:2 * LANE] = k_rope
        vt_ref[h, 0:MLA_DV, :] = vt[h * MLA_DV:(h + 1) * MLA_DV, :].astype(BF16)
        vt_ref[h, MLA_DV:_VT_ROWS, :] = jnp.ones((_VT_ROWS - MLA_DV, vt.shape[1]), BF16)


def _qkv(u_all, qg, kg, wq, wk, wvt, cos_t, sin_t, l, *, tr=512):
    n_lat = ROWS_L // tr
    per_seq = SEQ // tr

    def tab(i):
        return (jnp.where(i < n_lat, i % per_seq, per_seq), 0)

    return pl.pallas_call(
        _qkv_kernel,
        grid=(ROWS_ALL // tr,),
        in_specs=[
            pl.BlockSpec((tr, Q_RANK), lambda i: (i, U_Q // Q_RANK)),
            pl.BlockSpec((tr, KV_RANK), lambda i: (i, U_KV // KV_RANK)),
            pl.BlockSpec((tr, LANE), lambda i: (i, U_KR1 // LANE)),
            pl.BlockSpec((tr, LANE), lambda i: (i, U_KR2 // LANE)),
            _layer_spec(qg, l), _layer_spec(kg, l), _layer_spec(wq, l), _layer_spec(wk, l), _layer_spec(wvt, l),
            pl.BlockSpec((tr, LANE), tab),
            pl.BlockSpec((tr, LANE), tab),
        ],
        out_specs=[
            pl.BlockSpec((MLA_HEADS, tr, 2 * LANE), lambda i: (0, i, 0)),
            pl.BlockSpec((MLA_HEADS, tr, 2 * LANE), lambda i: (0, i, 0)),
            pl.BlockSpec((MLA_HEADS, _VT_ROWS, tr), lambda i: (0, 0, i)),
        ],
        out_shape=[
            jax.ShapeDtypeStruct((MLA_HEADS, ROWS_ALL, 2 * LANE), BF16),
            jax.ShapeDtypeStruct((MLA_HEADS, ROWS_ALL, 2 * LANE), BF16),
            jax.ShapeDtypeStruct((MLA_HEADS, _VT_ROWS, ROWS_ALL), BF16),
        ],
        compiler_params=_cparams(("parallel",)),
        name="qkv",
    )(u_all, u_all, u_all, u_all, qg, kg, wq, wk, wvt, cos_t, sin_t)


_KEY_CHUNK = 1024
_SCORE_LEAD = 1


def _fold_max(x):
    rows = x.shape[0]
    while rows > 8 and rows % 16 == 0:
        rows //= 2
        x = jnp.maximum(x[:rows], x[rows:])
    return x.max(axis=0, keepdims=True)


def _attn_kernel(*refs, n_src):
    q_ref = refs[0]
    k_refs = refs[1:1 + n_src]
    vt_refs = refs[1 + n_src:1 + 2 * n_src]
    g_ref = refs[1 + 2 * n_src]
    o_ref = refs[2 + 2 * n_src]
    chunks = [(k, vt, r0, min(_KEY_CHUNK, k.shape[1] - r0))
              for k, vt in zip(k_refs, vt_refs) for r0 in range(0, k.shape[1], _KEY_CHUNK)]

    def scores(h):
        return [_dot_nt(k[h, r0:r0 + n, :], q_ref[h]) for k, _, r0, n in chunks]

    pending = [scores(h) for h in range(_SCORE_LEAD)]
    for h in range(MLA_HEADS):
        if h + _SCORE_LEAD < MLA_HEADS:
            pending.append(scores(h + _SCORE_LEAD))
        s = pending.pop(0)
        m = _fold_max(s[0])
        for si in s[1:]:
            m = jnp.maximum(m, _fold_max(si))
        acc = None
        for si, (_, vt, r0, n) in zip(s, chunks):
            pv = _dot(vt[h, :, r0:r0 + n], jnp.exp2((si - m).astype(BF16)))
            acc = pv if acc is None else acc + pv
        out = (acc[0:MLA_DV] / acc[MLA_DV:MLA_DV + 1]).T
        gate = _silu(g_ref[:, h * LANE:(h + 1) * LANE].astype(F32))
        o_ref[:, h * LANE:(h + 1) * LANE] = (out * gate).astype(o_ref.dtype)


def _attention(q, k, vt, u_all, *, latent, tq=256):
    H = MLA_HEADS
    ctx_blk0 = ROWS_L // CTX_LEN
    if latent:
        n_q = SEQ // tq
        q_off = 0
        k_specs = [pl.BlockSpec((H, SEQ, 2 * LANE), lambda b, i: (0, b, 0)),
                   pl.BlockSpec((H, CTX_LEN, 2 * LANE), lambda b, i: (0, ctx_blk0 + b, 0))]
        v_specs = [pl.BlockSpec((H, _VT_ROWS, SEQ), lambda b, i: (0, 0, b)),
                   pl.BlockSpec((H, _VT_ROWS, CTX_LEN), lambda b, i: (0, 0, ctx_blk0 + b))]
    else:
        n_q = CTX_LEN // tq
        q_off = ROWS_L // tq
        k_specs = [pl.BlockSpec((H, CTX_LEN, 2 * LANE), lambda b, i: (0, ctx_blk0 + b, 0))]
        v_specs = [pl.BlockSpec((H, _VT_ROWS, CTX_LEN), lambda b, i: (0, 0, ctx_blk0 + b))]
    n_src = len(k_specs)
    in_specs = ([pl.BlockSpec((H, tq, 2 * LANE), lambda b, i: (0, q_off + b * n_q + i, 0))] + k_specs + v_specs
                + [pl.BlockSpec((tq, MLA_W), lambda b, i: (q_off + b * n_q + i, U_MLA_G // MLA_W))])
    args = [q] + [k] * n_src + [vt] * n_src + [u_all]
    return pl.pallas_call(
        functools.partial(_attn_kernel, n_src=n_src),
        grid=(BATCH, n_q),
        in_specs=in_specs,
        out_specs=pl.BlockSpec((tq, MLA_W), lambda b, i: (b * n_q + i, 0)),
        out_shape=jax.ShapeDtypeStruct((BATCH * n_q * tq, MLA_W), BF16),
        compiler_params=_cparams(("parallel", "parallel")),
        name="attn_latent" if latent else "attn_ctx",
    )(*args)


_POOL_PAD = 16


def _pool_kernel(x_ref, g_ref, w_ref, sc_ref, o_ref, pad_ref, *, L):
    zeros = jnp.zeros((_POOL_PAD, POOL_GROUP), F32)
    pad_ref[pl.ds(0, _POOL_PAD), :] = zeros
    pad_ref[pl.ds(L + _POOL_PAD, _POOL_PAD), :] = zeros
    t = lax.broadcasted_iota(jnp.int32, (L, 1), 0)
    for gi, win in enumerate(POOL_WINDOWS):
        half = win // 2
        cols = slice(gi * POOL_GROUP, (gi + 1) * POOL_GROUP)
        x = x_ref[:, cols].astype(F32)
        pad_ref[pl.ds(_POOL_PAD, L), :] = x
        acc = pad_ref[pl.ds(_POOL_PAD - half, L), :]
        for j in range(-half + 1, half):
            acc = acc + pad_ref[pl.ds(_POOL_PAD + j, L), :]
        cnt = (jnp.minimum(t + half, L) - jnp.maximum(t - half, 0)).astype(F32)
        dlt = (acc / cnt - x).astype(BF16)
        y = _dot(dlt, w_ref[gi]) * sc_ref[:, cols]
        o_ref[:, cols] = (y * _silu(g_ref[:, cols].astype(F32))).astype(o_ref.dtype)


def _pool(u_all, w_pool, pool_scale, l, *, L, row_blk0):
    return pl.pallas_call(
        functools.partial(_pool_kernel, L=L),
        grid=(BATCH,),
        in_specs=[
            pl.BlockSpec((L, POOL_W), lambda b: (row_blk0 + b, U_POOL // POOL_W)),
            pl.BlockSpec((L, POOL_W), lambda b: (row_blk0 + b, U_POOL_G // POOL_W)),
            _layer_spec(w_pool, l),
            _layer_spec(pool_scale, l),
        ],
        out_specs=pl.BlockSpec((L, POOL_W), lambda b: (b, 0)),
        out_shape=jax.ShapeDtypeStruct((BATCH * L, POOL_W), BF16),
        scratch_shapes=[pltpu.VMEM((L + 2 * _POOL_PAD, POOL_GROUP), F32)],
        compiler_params=_cparams(("parallel",)),
        name=f"pool_{L}",
    )(u_all, u_all, w_pool, pool_scale)


def _outproj_kernel(x_ref, *refs, n_lat_tiles, n_streams, final):
    mixers = [refs[3 * s:3 * s + 3] for s in range(n_streams)]
    w_ref, gt_ref, fg_ref, o_ref = refs[3 * n_streams:]

    def run(hy_ref, at_ref, po_ref):
        acc = (_dot(hy_ref[...], w_ref[0:HY_W, :]) + _dot(at_ref[...], w_ref[HY_W:HY_W + MLA_W, :])
               + _dot(po_ref[...], w_ref[HY_W + MLA_W:, :]))
        y = x_ref[...] + gt_ref[...] * acc
        if final:
            y = y * lax.rsqrt(jnp.mean(y * y, axis=-1, keepdims=True) + EPS) * fg_ref[...]
        o_ref[...] = y

    if n_streams == 1:
        run(*mixers[0])
    else:
        is_latent = pl.program_id(0) < n_lat_tiles
        pl.when(is_latent)(lambda: run(*mixers[0]))
        pl.when(jnp.logical_not(is_latent))(lambda: run(*mixers[1]))


def _outproj(x_all, mixers, w_out, mod, fg, l, *, final, tm=512):
    d = x_all.shape[1]
    tiles_per_batch = SEQ // tm
    n_lat_tiles = ROWS_L // tm
    n_row_tiles = sum(m[0].shape[0] for m in mixers) // tm

    def mod_row(i):
        return jnp.minimum(i // tiles_per_batch, BATCH)

    lat_blk = lambda i: (jnp.minimum(i, n_lat_tiles - 1), 0)
    ctx_blk = lambda i: (jnp.maximum(i - n_lat_tiles, 0), 0)
    mixer_specs, mixer_args = [], []
    for blk, (hy, att, po) in zip((lat_blk, ctx_blk), mixers):
        mixer_specs += [pl.BlockSpec((tm, HY_W), blk), pl.BlockSpec((tm, MLA_W), blk), pl.BlockSpec((tm, POOL_W), blk)]
        mixer_args += [hy, att, po]

    return pl.pallas_call(
        functools.partial(_outproj_kernel, n_lat_tiles=n_lat_tiles, n_streams=len(mixers), final=final),
        grid=(n_row_tiles,),
        in_specs=[pl.BlockSpec((tm, d), lambda i: (i, 0))] + mixer_specs + [
            pl.BlockSpec((None,) + w_out.shape[1:], lambda i: (l, 0, 0), pipeline_mode=pl.Buffered(1)),
            pl.BlockSpec((None, None, 1, d), lambda i: (l, mod_row(i), 0, 2)),
            pl.BlockSpec((1, d), lambda i: (0, 0)),
        ],
        out_specs=pl.BlockSpec((tm, d), lambda i: (i, 0)),
        out_shape=jax.ShapeDtypeStruct((n_row_tiles * tm, d), F32),
        compiler_params=_cparams(("parallel",)),
        name="outproj_final" if final else "outproj",
    )(x_all, *mixer_args, w_out, mod, fg)


def _rope_lane_tables(ident_rows):
    n_rows = SEQ // GRID_W
    row = np.repeat(np.arange(n_rows, dtype=np.float64), GRID_W)
    col = np.tile(np.arange(GRID_W, dtype=np.float64), n_rows)
    n_freq = MLA_DR // 4
    inv = ROPE_BASE ** (-np.arange(n_freq, dtype=np.float64) / n_freq)
    ang = np.concatenate([row[:, None] * inv, col[:, None] * inv], axis=-1)
    cos, sin = np.cos(ang), np.sin(ang)
    cos_t = np.concatenate([cos, cos, cos, cos], axis=-1)
    sin_t = np.concatenate([-sin, sin, -sin, sin], axis=-1)
    cos_t = np.concatenate([cos_t, np.ones((ident_rows, LANE))], axis=0)
    sin_t = np.concatenate([sin_t, np.zeros((ident_rows, LANE))], axis=0)
    return jnp.asarray(cos_t, F32), jnp.asarray(sin_t, F32)


def _dft_tables(P, dtype):
    idx = np.arange(P, dtype=np.int64)
    ang = ((idx[:, None] * idx[None, :]) % (2 * P)).astype(np.float64) * (math.pi / P)
    return jnp.asarray(np.cos(ang), F32).astype(dtype), jnp.asarray(np.sin(ang), F32).astype(dtype)


def _filter_tables(L):
    t = np.linspace(0.0, 1.0, L)[:, None]
    wpos = (2.0 * math.pi / L) * np.arange(L, dtype=np.float64)[:, None]
    bands = np.linspace(1e-4, FILTER_BANDS - 1, FILTER_BANDS)[None, :]
    feats = np.concatenate([t, np.cos(bands * wpos), -np.sin(bands * wpos)], axis=-1)
    feats = np.pad(feats, ((0, 0), (0, LANE - FILTER_EMB)))
    return jnp.asarray(feats, F32), jnp.asarray(t, F32)


def _decay_rates():
    d = np.abs(np.linspace(math.log(DECAY_TARGET) / SLOW_DECAY, math.log(DECAY_TARGET) / FAST_DECAY, HY_W))
    return jnp.asarray(d[None, :], F32)


def _kr_permutation():
    p = np.zeros((2 * LANE, MLA_DR), np.float32)
    half = MLA_DR // 2
    for grp, odd in enumerate((0, 1, 0, 1, 1, 0, 1, 0)):
        for i in range(half):
            p[grp * half + i, 2 * i + odd] = 1.0
    return jnp.asarray(p, BF16)


_PACK_UNIT = MLA_DR
_PACK_BLK = 4 * _PACK_UNIT
_PACK_KR_STEP = R_OFF_KR // _PACK_BLK


def _pack_w_in_kernel(a0_ref, a1_ref, a2_ref, a3_ref, p_ref, o_ref):
    i = pl.program_id(1)

    @pl.when(i != _PACK_KR_STEP)
    def _():
        for k, a_ref in enumerate((a0_ref, a1_ref, a2_ref, a3_ref)):
            o_ref[pl.ds(k * _PACK_UNIT, _PACK_UNIT), :] = a_ref[...].astype(BF16)

    @pl.when(i == _PACK_KR_STEP)
    def _():
        o_ref[...] = _dot(p_ref[...], a0_ref[...].astype(BF16)).astype(BF16)


def _pack_w_in(w_in_t):
    depth, n, d = w_in_t.shape
    assert R_OFF_KR % _PACK_BLK == 0 and U_KR1 == R_OFF_KR and U_MLA_G == U_KR1 + _PACK_BLK
    perm = _kr_permutation()

    def src(k):
        def index_map(l, i):
            unit = jnp.where(i < _PACK_KR_STEP, 4 * i + k,
                             jnp.where(i == _PACK_KR_STEP, R_OFF_KR // _PACK_UNIT, 4 * i + k - 3))
            return (l, unit, 0)
        return pl.BlockSpec((None, _PACK_UNIT, d), index_map)

    return pl.pallas_call(
        _pack_w_in_kernel,
        grid=(depth, U_W // _PACK_BLK),
        in_specs=[src(0), src(1), src(2), src(3), pl.BlockSpec(perm.shape, lambda l, i: (0, 0))],
        out_specs=pl.BlockSpec((None, _PACK_BLK, d), lambda l, i: (l, i, 0)),
        out_shape=jax.ShapeDtypeStruct((depth, U_W, d), BF16),
        compiler_params=_cparams(("parallel", "parallel")),
        name="pack_w_in",
    )(w_in_t, w_in_t, w_in_t, w_in_t, perm)


def _pack_w_uq(w_uq):
    w = w_uq.reshape(DEPTH, Q_RANK, MLA_HEADS, MLA_DN + MLA_DR)
    nope = w[..., :MLA_DN].reshape(DEPTH, Q_RANK, MLA_W)
    a, b = w[..., MLA_DN::2], w[..., MLA_DN + 1::2]
    rot = jnp.concatenate([a, b], axis=-1).reshape(DEPTH, Q_RANK, MLA_HEADS * MLA_DR)
    swp = jnp.concatenate([b, a], axis=-1).reshape(DEPTH, Q_RANK, MLA_HEADS * MLA_DR)
    return jnp.concatenate([nope, rot, swp], axis=-1).astype(BF16)


def _pack_w_ukv(w_ukv):
    w = w_ukv.reshape(DEPTH, KV_RANK, MLA_HEADS, MLA_DN + MLA_DV)
    wk = w[..., :MLA_DN].reshape(DEPTH, KV_RANK, MLA_W)
    wv = w[..., MLA_DN:].reshape(DEPTH, KV_RANK, MLA_W)
    return wk.astype(BF16), jnp.swapaxes(wv, 1, 2).astype(BF16)


def kernel(x, c, ctx, c_ctx, norm_g, w_ada, b_ada, w_in, hy_conv_w, hy_conv_b, hf_w1, hf_b1, hf_freq,
           hf_w2, hf_b2, hf_w3, hy_bias, q_norm_g, w_uq, kv_norm_g, w_ukv, w_pool, pool_scale, w_out,
           final_norm_g):
    assert x.shape == (BATCH, SEQ, D_MODEL) and ctx.shape == (BATCH, CTX_LEN, D_MODEL)

    qkv_tr = 512
    cos_t, sin_t = _rope_lane_tables(qkv_tr)

    w_in_p = _pack_w_in(jnp.swapaxes(w_in, 1, 2))
    w_q_p = _pack_w_uq(w_uq)
    w_k_p, w_vt_p = _pack_w_ukv(w_ukv)
    w_out_b = w_out.astype(BF16)
    w_pool_b = w_pool.astype(BF16)
    w1_p = jnp.pad(hf_w1, ((0, 0), (0, LANE - FILTER_EMB), (0, 0)))
    w3_p = hf_w3.reshape(DEPTH, FILTER_HIDDEN, 2, 2, HY_W).transpose(0, 2, 3, 1, 4)
    conv_w = hy_conv_w.reshape(DEPTH, 3, 3, HY_W)
    conv_b = hy_conv_b.reshape(DEPTH, 3, HY_W)
    rows = lambda a: a[:, None, :]
    norm_g3, qg3, kg3, ps3 = rows(norm_g), rows(q_norm_g), rows(kv_norm_g), rows(pool_scale)
    b1_3, fr_3, b2_3 = rows(hf_b1), rows(hf_freq), rows(hf_b2)

    cond = jnp.concatenate([c, c_ctx[None], jnp.zeros((8 - BATCH - 1, D_MODEL), F32)], axis=0)
    mod = _adaln(cond, w_ada, b_ada[:, None, :])
    mod = mod.reshape(DEPTH, 8, 1, 3 * D_MODEL)

    x_all = jnp.concatenate([x.reshape(ROWS_L, D_MODEL), ctx.reshape(ROWS_C, D_MODEL)], axis=0)

    for l in range(DEPTH):
        last = l == DEPTH - 1
        u_all = _inproj(x_all, norm_g3, mod, w_in_p, l, n_row_tiles=ROWS_ALL // 1024)
        q, k, vt = _qkv(u_all, qg3, kg3, w_q_p, w_k_p, w_vt_p, cos_t, sin_t, l, tr=qkv_tr)
        mixers = []
        for L, row_blk0, latent in ((SEQ, 0, True), (CTX_LEN, ROWS_L // CTX_LEN, False)):
            if last and not latent:
                continue
            g, gn = _hyena_filter(L, w1_p, b1_3, fr_3, hf_w2, b2_3, w3_p, l)
            hy = _hyena(u_all, conv_w, conv_b, hy_bias, g, gn, l, L=L, row_blk0=row_blk0)
            att = _attention(q, k, vt, u_all, latent=latent, tq=256)
            po = _pool(u_all, w_pool_b, ps3, l, L=L, row_blk0=row_blk0)
            mixers.append((hy, att, po))
        x_all = _outproj(x_all, mixers, w_out_b, mod, final_norm_g[None, :], l, final=last)
    return x_all.reshape(BATCH, SEQ, D_MODEL)
```

```python
import functools
import math

import jax
import jax.numpy as jnp
import numpy as np
from jax import lax
from jax.experimental import pallas as pl
from jax.experimental.pallas import tpu as pltpu

F32 = jnp.float32
BF16 = jnp.bfloat16

D_MODEL = 2048
BATCH = 4
SEQ = 2048
DEPTH = 4
CTX_LEN = 256
GRID_W = 64
EPS = 1e-6

HY_W = 512
MLA_HEADS = 8
MLA_DN = 128
MLA_DR = 64
MLA_DV = 128
MLA_W = MLA_HEADS * MLA_DV
Q_RANK = 512
KV_RANK = 256
POOL_W = 512
POOL_WINDOWS = (2, 4, 8, 16)
POOL_GROUP = 128

FILTER_EMB = 33
FILTER_BANDS = 16
FILTER_HIDDEN = 64
DECAY_TARGET = 1e-2
FAST_DECAY = 0.3
SLOW_DECAY = 1.5
ROPE_BASE = 10000.0
ATTN_SCALE = (MLA_DN + MLA_DR) ** -0.5

R_OFF_KR = 2816
R_OFF_MLA_G = 2880

U_HY = 0
U_Q = 2048
U_KV = 2560
U_KR1 = 2816
U_KR2 = 2944
U_MLA_G = 3072
U_POOL = 4096
U_POOL_G = 4608
U_W = 5120

ROWS_L = BATCH * SEQ
ROWS_C = BATCH * CTX_LEN
ROWS_ALL = ROWS_L + ROWS_C

VMEM_LIMIT_BYTES = 56 * 1024 * 1024
LANE = 128


def _cparams(sem):
    return pltpu.CompilerParams(dimension_semantics=sem, vmem_limit_bytes=VMEM_LIMIT_BYTES)


def _silu(x):
    return x * jax.nn.sigmoid(x)


def _dot(a, b):
    return jnp.dot(a, b, preferred_element_type=F32)


def _dot_nt(a, b):
    return lax.dot_general(a, b, (((1,), (1,)), ((), ())), preferred_element_type=F32)


def _adaln_kernel(c_ref, w_ref, b_ref, o_ref):
    a = _silu(c_ref[...]).astype(BF16)
    o_ref[...] = _dot(a, w_ref[...].astype(BF16)) + b_ref[...]


def _adaln(cond, w_ada, b_ada, tn=1024):
    depth, d, n = w_ada.shape
    return pl.pallas_call(
        _adaln_kernel,
        grid=(depth, n // tn),
        in_specs=[
            pl.BlockSpec((8, d), lambda l, j: (0, 0)),
            pl.BlockSpec((None, d, tn), lambda l, j: (l, 0, j)),
            pl.BlockSpec((None, 1, tn), lambda l, j: (l, 0, j)),
        ],
        out_specs=pl.BlockSpec((None, 8, tn), lambda l, j: (l, 0, j)),
        out_shape=jax.ShapeDtypeStruct((depth, 8, n), F32),
        compiler_params=_cparams(("parallel", "parallel")),
        name="adaln",
    )(cond, w_ada, b_ada)


def _inproj_kernel(x_ref, g_ref, sh_ref, sc_ref, w_ref, o_ref, h_ref, *, sub):
    first = pl.program_id(1) == 0

    @pl.when(first)
    def _():
        gs = g_ref[...] * (1.0 + sc_ref[...])
        sh = sh_ref[...]
        for r in range(0, x_ref.shape[0], sub):
            rows = pl.ds(r, sub)
            x = x_ref[rows, :]
            ms = jnp.mean(x * x, axis=-1, keepdims=True)
            h = (x * lax.rsqrt(ms + EPS) * gs + sh).astype(BF16)
            h_ref[rows, :] = h
            o_ref[rows, :] = _dot_nt(h, w_ref[...]).astype(o_ref.dtype)

    @pl.when(jnp.logical_not(first))
    def _():
        o_ref[...] = _dot_nt(h_ref[...], w_ref[...]).astype(o_ref.dtype)


def _layer_spec(arr, l):
    zeros = (0,) * (arr.ndim - 1)
    return pl.BlockSpec((None,) + arr.shape[1:], lambda *_: (l,) + zeros)


def _inproj(x_all, g, mod, w_t, l, *, n_row_tiles, tm=1024, tn=1280):
    d = x_all.shape[1]
    n = w_t.shape[1]
    tiles_per_batch = SEQ // tm

    def mod_row(i):
        return jnp.minimum(i // tiles_per_batch, BATCH)

    return pl.pallas_call(
        functools.partial(_inproj_kernel, sub=256),
        grid=(n_row_tiles, n // tn),
        in_specs=[
            pl.BlockSpec((tm, d), lambda i, j: (i, 0)),
            _layer_spec(g, l),
            pl.BlockSpec((None, None, 1, d), lambda i, j: (l, mod_row(i), 0, 0)),
            pl.BlockSpec((None, None, 1, d), lambda i, j: (l, mod_row(i), 0, 1)),
            pl.BlockSpec((None, tn, d), lambda i, j: (l, j, 0)),
        ],
        out_specs=pl.BlockSpec((tm, tn), lambda i, j: (i, j)),
        out_shape=jax.ShapeDtypeStruct((x_all.shape[0], n), BF16),
        scratch_shapes=[pltpu.VMEM((tm, d), BF16)],
        compiler_params=_cparams(("parallel", "arbitrary")),
        name="inproj",
    )(x_all, g, mod, mod, w_t)


def _hyena_block(L):
    return min(L, 512)


def _filter_kernel(feats_ref, t_ref, dl_ref, w1_ref, b1_ref, fr_ref, w2_ref, b2_ref, w3_ref,
                   c_ref, s_ref, g_ref, gn_ref, h_ref, *, L, P):
    hp = lax.Precision.HIGHEST
    hdot = lambda a, b: jnp.dot(a, b, precision=hp, preferred_element_type=F32)
    nb = L // P
    m_fft = 2 * P

    @pl.when(pl.program_id(0) == 0)
    def _():
        fr = fr_ref[...]
        h1 = jnp.sin(fr * (hdot(feats_ref[...], w1_ref[...]) + b1_ref[...]))
        h_ref[...] = jnp.sin(fr * (hdot(h1, w2_ref[...]) + b2_ref[...]))

    h = h_ref[...]
    decay = jnp.exp(-t_ref[...] * dl_ref[...])
    row = lax.broadcasted_iota(jnp.int32, (L, 1), 0)
    rp = lax.broadcasted_iota(jnp.int32, (P, 1), 0)
    sgn = jnp.where(rp % 2 == 0, 1.0, -1.0).astype(F32)
    wgt = jnp.where(rp == 0, 1.0 / m_fft, 2.0 / m_fft).astype(F32)
    cm = c_ref[...]
    sm = s_ref[...]
    hf = hdot(h, w3_ref[0]) * decay
    hb = hdot(h, w3_ref[1]) * decay
    hb = jnp.where(row == 0, 0.0, hb)
    nrm = lax.rsqrt(jnp.sum(hf * hf + hb * hb, axis=0, keepdims=True) + EPS)
    F, B = [], []
    for arr, out in ((hf * nrm, F), (hb * nrm, B)):
        for j in range(nb):
            blk = arr[j * P:(j + 1) * P].astype(BF16)
            b32 = blk.astype(F32)
            out.append((_dot(cm, blk), _dot(sm, blk), jnp.sum(b32 * sgn, axis=0, keepdims=True), b32[0:1]))
    for d in range(-(nb - 1), nb):
        if d >= 1:
            gr = F[d][0] + sgn * (F[d - 1][0] - F[d - 1][3])
            gi = F[d][1] + sgn * F[d - 1][1]
            gn = F[d][2] + F[d - 1][2] - F[d - 1][3]
        elif d == 0:
            gr = F[0][0] + B[0][0]
            gi = F[0][1] - B[0][1]
            gn = F[0][2] + B[0][2]
        else:
            e = -d
            gr = B[e][0] + sgn * (B[e - 1][0] - B[e - 1][3])
            gi = -B[e][1] - sgn * B[e - 1][1]
            gn = B[e][2] + B[e - 1][2] - B[e - 1][3]
        g_ref[d + nb - 1, 0] = gr * wgt
        g_ref[d + nb - 1, 1] = gi * wgt
        gn_ref[d + nb - 1] = gn * (1.0 / m_fft)


def _hyena_filter(L, w1, b1, fr, w2, b2, w3, l, *, cb=256):
    P = _hyena_block(L)
    nd = 2 * (L // P) - 1
    feats, t = _filter_tables(L)
    deltas = _decay_rates()
    cmat, smat = _dft_tables(P, BF16)
    full2 = lambda s: (0, 0)
    return pl.pallas_call(
        functools.partial(_filter_kernel, L=L, P=P),
        grid=(2 * (HY_W // cb),),
        in_specs=[
            pl.BlockSpec(feats.shape, full2),
            pl.BlockSpec(t.shape, full2),
            pl.BlockSpec((1, cb), lambda s: (0, s // 2)),
            _layer_spec(w1, l), _layer_spec(b1, l), _layer_spec(fr, l), _layer_spec(w2, l), _layer_spec(b2, l),
            pl.BlockSpec((None, None, 2, FILTER_HIDDEN, cb), lambda s: (l, s % 2, 0, 0, s // 2)),
            pl.BlockSpec((P, P), full2),
            pl.BlockSpec((P, P), full2),
        ],
        out_specs=[
            pl.BlockSpec((None, nd, 2, P, cb), lambda s: (s % 2, 0, 0, 0, s // 2)),
            pl.BlockSpec((None, nd, 1, cb), lambda s: (s % 2, 0, 0, s // 2)),
        ],
        out_shape=[
            jax.ShapeDtypeStruct((2, nd, 2, P, HY_W), F32),
            jax.ShapeDtypeStruct((2, nd, 1, HY_W), F32),
        ],
        scratch_shapes=[pltpu.VMEM((L, FILTER_HIDDEN), F32)],
        compiler_params=_cparams(("arbitrary",)),
        name=f"hyena_filter_{L}",
    )(feats, t, deltas, w1, b1, fr, w2, b2, w3, cmat, smat)


_MAC_ROWS = 32


def _hyena_kernel(v_ref, x1_ref, x2_ref, gate_ref, cw_ref, cb_ref, hb_ref, g_ref, gn_ref,
                  c_ref, s_ref, o_ref, z_ref, zb_ref, xc_ref, zf_ref, yr_ref, yi_ref, *, L, P):
    nb = L // P

    def short_conv(dst_ref, src_ref, p):
        x = src_ref[...].astype(F32)
        w = cw_ref[:, p, :]
        bias = cb_ref[p:p + 1, :]
        prev = pltpu.roll(x, 1, axis=0)
        nxt = pltpu.roll(x, L - 1, axis=0)
        dst_ref[...] = prev * w[0:1] + x * w[1:2] + nxt * w[2:3] + bias
        dst_ref[0:1, :] = x[0:1] * w[1:2] + x[1:2] * w[2:3] + bias
        dst_ref[L - 1:L, :] = x[L - 2:L - 1] * w[0:1] + x[L - 1:L] * w[1:2] + bias

    sign = jnp.where(lax.broadcasted_iota(jnp.int32, (P, 1), 0) % 2 == 0, 1.0, -1.0).astype(F32)
    cm = c_ref[...]
    sm = s_ref[...]

    short_conv(z_ref, v_ref, 0)
    for o in range(2):
        short_conv(xc_ref, x1_ref if o == 0 else x2_ref, 1 + o)
        zb_ref[...] = z_ref[...].astype(BF16)
        zn = []
        for j in range(nb):
            rows = pl.ds(j * P, P)
            zf_ref[0, j] = _dot(cm, zb_ref[rows, :])
            zf_ref[1, j] = _dot(sm, zb_ref[rows, :])
            zn.append(jnp.sum(z_ref[rows, :] * sign, axis=0, keepdims=True))
        bias = hb_ref[o:o + 1, :]
        for i in range(nb):
            def mac(r, carry):
                rr = pl.ds(pl.multiple_of(r * _MAC_ROWS, _MAC_ROWS), _MAC_ROWS)
                yr = yi = None
                for j in range(nb):
                    d = i - j + nb - 1
                    gr, gi = g_ref[o, d, 0, rr, :], g_ref[o, d, 1, rr, :]
                    zr, zi = zf_ref[0, j, rr, :], zf_ref[1, j, rr, :]
                    pr, pi = gr * zr - gi * zi, gr * zi + gi * zr
                    yr, yi = (pr, pi) if yr is None else (yr + pr, yi + pi)
                yr_ref[rr, :] = yr.astype(BF16)
                yi_ref[rr, :] = yi.astype(BF16)
                return carry

            lax.fori_loop(0, P // _MAC_ROWS, mac, 0)
            yn = gn_ref[o, i + nb - 1] * zn[0]
            for j in range(1, nb):
                yn = yn + gn_ref[o, i - j + nb - 1] * zn[j]
            rows = pl.ds(i * P, P)
            y = _dot(cm, yr_ref[...]) + _dot(sm, yi_ref[...])
            y = y + sign * yn + z_ref[rows, :] * bias
            y = xc_ref[rows, :] * y
            if o == 0:
                z_ref[rows, :] = y
            else:
                o_ref[rows, :] = (y * _silu(gate_ref[rows, :].astype(F32))).astype(o_ref.dtype)


def _hyena(u_all, conv_w, conv_b, hy_bias, g, gn, l, *, L, row_blk0, cb=256):
    P = _hyena_block(L)
    nb = L // P
    nd = 2 * nb - 1
    ncb = HY_W // cb
    cmat, smat = _dft_tables(P, BF16)

    def ublk(part):
        return pl.BlockSpec((L, cb), lambda j, b: (row_blk0 + b, part * ncb + j))

    in_specs = [
        ublk(0), ublk(1), ublk(2), ublk(3),
        pl.BlockSpec((None, 3, 3, cb), lambda j, b: (l, 0, 0, j)),
        pl.BlockSpec((None, 3, cb), lambda j, b: (l, 0, j)),
        pl.BlockSpec((None, 2, cb), lambda j, b: (l, 0, j)),
        pl.BlockSpec((2, nd, 2, P, cb), lambda j, b: (0, 0, 0, 0, j)),
        pl.BlockSpec((2, nd, 1, cb), lambda j, b: (0, 0, 0, j)),
        pl.BlockSpec((P, P), lambda j, b: (0, 0)),
        pl.BlockSpec((P, P), lambda j, b: (0, 0)),
    ]
    args = [u_all, u_all, u_all, u_all, conv_w, conv_b, hy_bias, g, gn, cmat, smat]
    return pl.pallas_call(
        functools.partial(_hyena_kernel, L=L, P=P),
        grid=(ncb, BATCH),
        in_specs=in_specs,
        out_specs=pl.BlockSpec((L, cb), lambda j, b: (b, j)),
        out_shape=jax.ShapeDtypeStruct((BATCH * L, HY_W), BF16),
        scratch_shapes=[
            pltpu.VMEM((L, cb), F32),
            pltpu.VMEM((L, cb), BF16),
            pltpu.VMEM((L, cb), F32),
            pltpu.VMEM((2, nb, P, cb), F32),
            pltpu.VMEM((P, cb), BF16),
            pltpu.VMEM((P, cb), BF16),
        ],
        compiler_params=_cparams(("parallel", "parallel")),
        name=f"hyena_{L}",
    )(*args)


_Q_SCALE = ATTN_SCALE * math.log2(math.e)
_VT_ROWS = MLA_DV + 16


def _qkv_kernel(uq_ref, ukv_ref, k1_ref, k2_ref, qg_ref, kg_ref, wq_ref, wk_ref, wvt_ref, cos_ref, sin_ref,
                q_ref, k_ref, vt_ref):
    def rms(x, g):
        x = x.astype(F32)
        return (x * lax.rsqrt(jnp.mean(x * x, axis=-1, keepdims=True) + EPS) * g).astype(BF16)

    cos = cos_ref[...]
    sin = sin_ref[...]
    lane = lax.broadcasted_iota(jnp.int32, (1, LANE), 1)
    half_mask = [(lane < MLA_DR), (lane >= MLA_DR)]

    qa = _dot(rms(uq_ref[...], qg_ref[...]), wq_ref[...])
    kvn = rms(ukv_ref[...], kg_ref[...])
    kn = _dot(kvn, wk_ref[...])
    vt = _dot_nt(wvt_ref[...], kvn)
    k_rope = (k1_ref[...].astype(F32) * cos + k2_ref[...].astype(F32) * sin).astype(BF16)
    n_rot = MLA_HEADS * MLA_DR
    for h in range(MLA_HEADS):
        c = h // 2
        qr = qa[:, MLA_W + c * LANE:MLA_W + (c + 1) * LANE]
        qs = qa[:, MLA_W + n_rot + c * LANE:MLA_W + n_rot + (c + 1) * LANE]
        rot = jnp.where(half_mask[h % 2], qr * cos + qs * sin, 0.0)
        q_ref[h, :, 0:LANE] = (qa[:, h * LANE:(h + 1) * LANE] * _Q_SCALE).astype(BF16)
        q_ref[h, :, LANE:2 * LANE] = (rot * _Q_SCALE).astype(BF16)
        k_ref[h, :, 0:LANE] = kn[:, h * LANE:(h + 1) * LANE].astype(BF16)
        k_ref[h, :, LANE:2 * LANE] = k_rope
        vt_ref[h, 0:MLA_DV, :] = vt[h * MLA_DV:(h + 1) * MLA_DV, :].astype(BF16)
        vt_ref[h, MLA_DV:_VT_ROWS, :] = jnp.ones((_VT_ROWS - MLA_DV, vt.shape[1]), BF16)


def _qkv(u_all, qg, kg, wq, wk, wvt, cos_t, sin_t, l, *, tr=512):
    n_lat = ROWS_L // tr
    per_seq = SEQ // tr

    def tab(i):
        return (jnp.where(i < n_lat, i % per_seq, per_seq), 0)

    return pl.pallas_call(
        _qkv_kernel,
        grid=(ROWS_ALL // tr,),
        in_specs=[
            pl.BlockSpec((tr, Q_RANK), lambda i: (i, U_Q // Q_RANK)),
            pl.BlockSpec((tr, KV_RANK), lambda i: (i, U_KV // KV_RANK)),
            pl.BlockSpec((tr, LANE), lambda i: (i, U_KR1 // LANE)),
            pl.BlockSpec((tr, LANE), lambda i: (i, U_KR2 // LANE)),
            _layer_spec(qg, l), _layer_spec(kg, l), _layer_spec(wq, l), _layer_spec(wk, l), _layer_spec(wvt, l),
            pl.BlockSpec((tr, LANE), tab),
            pl.BlockSpec((tr, LANE), tab),
        ],
        out_specs=[
            pl.BlockSpec((MLA_HEADS, tr, 2 * LANE), lambda i: (0, i, 0)),
            pl.BlockSpec((MLA_HEADS, tr, 2 * LANE), lambda i: (0, i, 0)),
            pl.BlockSpec((MLA_HEADS, _VT_ROWS, tr), lambda i: (0, 0, i)),
        ],
        out_shape=[
            jax.ShapeDtypeStruct((MLA_HEADS, ROWS_ALL, 2 * LANE), BF16),
            jax.ShapeDtypeStruct((MLA_HEADS, ROWS_ALL, 2 * LANE), BF16),
            jax.ShapeDtypeStruct((MLA_HEADS, _VT_ROWS, ROWS_ALL), BF16),
        ],
        compiler_params=_cparams(("parallel",)),
        name="qkv",
    )(u_all, u_all, u_all, u_all, qg, kg, wq, wk, wvt, cos_t, sin_t)


_KEY_CHUNK = 1024
_SCORE_LEAD = 1


def _fold_max(x):
    rows = x.shape[0]
    while rows > 8 and rows % 16 == 0:
        rows //= 2
        x = jnp.maximum(x[:rows], x[rows:])
    return x.max(axis=0, keepdims=True)


def _attn_kernel(*refs, n_src):
    q_ref = refs[0]
    k_refs = refs[1:1 + n_src]
    vt_refs = refs[1 + n_src:1 + 2 * n_src]
    g_ref = refs[1 + 2 * n_src]
    o_ref = refs[2 + 2 * n_src]
    chunks = [(k, vt, r0, min(_KEY_CHUNK, k.shape[1] - r0))
              for k, vt in zip(k_refs, vt_refs) for r0 in range(0, k.shape[1], _KEY_CHUNK)]

    def scores(h):
        return [_dot_nt(k[h, r0:r0 + n, :], q_ref[h]) for k, _, r0, n in chunks]

    pending = [scores(h) for h in range(_SCORE_LEAD)]
    for h in range(MLA_HEADS):
        if h + _SCORE_LEAD < MLA_HEADS:
            pending.append(scores(h + _SCORE_LEAD))
        s = pending.pop(0)
        m = _fold_max(s[0])
        for si in s[1:]:
            m = jnp.maximum(m, _fold_max(si))
        acc = None
        for si, (_, vt, r0, n) in zip(s, chunks):
            pv = _dot(vt[h, :, r0:r0 + n], jnp.exp2((si - m).astype(BF16)))
            acc = pv if acc is None else acc + pv
        out = (acc[0:MLA_DV] / acc[MLA_DV:MLA_DV + 1]).T
        gate = _silu(g_ref[:, h * LANE:(h + 1) * LANE].astype(F32))
        o_ref[:, h * LANE:(h + 1) * LANE] = (out * gate).astype(o_ref.dtype)


def _attention(q, k, vt, u_all, *, latent, tq=256):
    H = MLA_HEADS
    ctx_blk0 = ROWS_L // CTX_LEN
    if latent:
        n_q = SEQ // tq
        q_off = 0
        k_specs = [pl.BlockSpec((H, SEQ, 2 * LANE), lambda b, i: (0, b, 0)),
                   pl.BlockSpec((H, CTX_LEN, 2 * LANE), lambda b, i: (0, ctx_blk0 + b, 0))]
        v_specs = [pl.BlockSpec((H, _VT_ROWS, SEQ), lambda b, i: (0, 0, b)),
                   pl.BlockSpec((H, _VT_ROWS, CTX_LEN), lambda b, i: (0, 0, ctx_blk0 + b))]
    else:
        n_q = CTX_LEN // tq
        q_off = ROWS_L // tq
        k_specs = [pl.BlockSpec((H, CTX_LEN, 2 * LANE), lambda b, i: (0, ctx_blk0 + b, 0))]
        v_specs = [pl.BlockSpec((H, _VT_ROWS, CTX_LEN), lambda b, i: (0, 0, ctx_blk0 + b))]
    n_src = len(k_specs)
    in_specs = ([pl.BlockSpec((H, tq, 2 * LANE), lambda b, i: (0, q_off + b * n_q + i, 0))] + k_specs + v_specs
                + [pl.BlockSpec((tq, MLA_W), lambda b, i: (q_off + b * n_q + i, U_MLA_G // MLA_W))])
    args = [q] + [k] * n_src + [vt] * n_src + [u_all]
    return pl.pallas_call(
        functools.partial(_attn_kernel, n_src=n_src),
        grid=(BATCH, n_q),
        in_specs=in_specs,
        out_specs=pl.BlockSpec((tq, MLA_W), lambda b, i: (b * n_q + i, 0)),
        out_shape=jax.ShapeDtypeStruct((BATCH * n_q * tq, MLA_W), BF16),
        compiler_params=_cparams(("parallel", "parallel")),
        name="attn_latent" if latent else "attn_ctx",
    )(*args)


_POOL_PAD = 16


def _pool_kernel(x_ref, g_ref, w_ref, sc_ref, o_ref, pad_ref, *, L):
    zeros = jnp.zeros((_POOL_PAD, POOL_GROUP), F32)
    pad_ref[pl.ds(0, _POOL_PAD), :] = zeros
    pad_ref[pl.ds(L + _POOL_PAD, _POOL_PAD), :] = zeros
    t = lax.broadcasted_iota(jnp.int32, (L, 1), 0)
    for gi, win in enumerate(POOL_WINDOWS):
        half = win // 2
        cols = slice(gi * POOL_GROUP, (gi + 1) * POOL_GROUP)
        x = x_ref[:, cols].astype(F32)
        pad_ref[pl.ds(_POOL_PAD, L), :] = x
        acc = pad_ref[pl.ds(_POOL_PAD - half, L), :]
        for j in range(-half + 1, half):
            acc = acc + pad_ref[pl.ds(_POOL_PAD + j, L), :]
        cnt = (jnp.minimum(t + half, L) - jnp.maximum(t - half, 0)).astype(F32)
        dlt = (acc / cnt - x).astype(BF16)
        y = _dot(dlt, w_ref[gi]) * sc_ref[:, cols]
        o_ref[:, cols] = (y * _silu(g_ref[:, cols].astype(F32))).astype(o_ref.dtype)


def _pool(u_all, w_pool, pool_scale, l, *, L, row_blk0):
    return pl.pallas_call(
        functools.partial(_pool_kernel, L=L),
        grid=(BATCH,),
        in_specs=[
            pl.BlockSpec((L, POOL_W), lambda b: (row_blk0 + b, U_POOL // POOL_W)),
            pl.BlockSpec((L, POOL_W), lambda b: (row_blk0 + b, U_POOL_G // POOL_W)),
            _layer_spec(w_pool, l),
            _layer_spec(pool_scale, l),
        ],
        out_specs=pl.BlockSpec((L, POOL_W), lambda b: (b, 0)),
        out_shape=jax.ShapeDtypeStruct((BATCH * L, POOL_W), BF16),
        scratch_shapes=[pltpu.VMEM((L + 2 * _POOL_PAD, POOL_GROUP), F32)],
        compiler_params=_cparams(("parallel",)),
        name=f"pool_{L}",
    )(u_all, u_all, w_pool, pool_scale)


def _outproj_kernel(x_ref, *refs, n_lat_tiles, n_streams, final, sub):
    mixers = [refs[3 * s:3 * s + 3] for s in range(n_streams)]
    w_ref, gt_ref, fg_ref, o_ref = refs[3 * n_streams:]

    def run(hy_ref, at_ref, po_ref):
        for r in range(0, x_ref.shape[0], sub):
            rows = pl.ds(r, sub)
            acc = (_dot(hy_ref[rows, :], w_ref[0:HY_W, :]) + _dot(at_ref[rows, :], w_ref[HY_W:HY_W + MLA_W, :])
                   + _dot(po_ref[rows, :], w_ref[HY_W + MLA_W:, :]))
            y = x_ref[rows, :] + gt_ref[...] * acc
            if final:
                y = y * lax.rsqrt(jnp.mean(y * y, axis=-1, keepdims=True) + EPS) * fg_ref[...]
            o_ref[rows, :] = y

    if n_streams == 1:
        run(*mixers[0])
    else:
        is_latent = pl.program_id(0) < n_lat_tiles
        pl.when(is_latent)(lambda: run(*mixers[0]))
        pl.when(jnp.logical_not(is_latent))(lambda: run(*mixers[1]))


def _outproj(x_all, mixers, w_out, mod, fg, l, *, final, tm=512):
    d = x_all.shape[1]
    tiles_per_batch = SEQ // tm
    n_lat_tiles = ROWS_L // tm
    n_row_tiles = sum(m[0].shape[0] for m in mixers) // tm

    def mod_row(i):
        return jnp.minimum(i // tiles_per_batch, BATCH)

    lat_blk = lambda i: (jnp.minimum(i, n_lat_tiles - 1), 0)
    ctx_blk = lambda i: (jnp.maximum(i - n_lat_tiles, 0), 0)
    mixer_specs, mixer_args = [], []
    for blk, (hy, att, po) in zip((lat_blk, ctx_blk), mixers):
        mixer_specs += [pl.BlockSpec((tm, HY_W), blk), pl.BlockSpec((tm, MLA_W), blk), pl.BlockSpec((tm, POOL_W), blk)]
        mixer_args += [hy, att, po]

    return pl.pallas_call(
        functools.partial(_outproj_kernel, n_lat_tiles=n_lat_tiles, n_streams=len(mixers), final=final, sub=256),
        grid=(n_row_tiles,),
        in_specs=[pl.BlockSpec((tm, d), lambda i: (i, 0))] + mixer_specs + [
            pl.BlockSpec((None,) + w_out.shape[1:], lambda i: (l, 0, 0), pipeline_mode=pl.Buffered(1)),
            pl.BlockSpec((None, None, 1, d), lambda i: (l, mod_row(i), 0, 2)),
            pl.BlockSpec((1, d), lambda i: (0, 0)),
        ],
        out_specs=pl.BlockSpec((tm, d), lambda i: (i, 0)),
        out_shape=jax.ShapeDtypeStruct((n_row_tiles * tm, d), F32),
        compiler_params=_cparams(("parallel",)),
        name="outproj_final" if final else "outproj",
    )(x_all, *mixer_args, w_out, mod, fg)


def _rope_lane_tables(ident_rows):
    n_rows = SEQ // GRID_W
    row = np.repeat(np.arange(n_rows, dtype=np.float64), GRID_W)
    col = np.tile(np.arange(GRID_W, dtype=np.float64), n_rows)
    n_freq = MLA_DR // 4
    inv = ROPE_BASE ** (-np.arange(n_freq, dtype=np.float64) / n_freq)
    ang = np.concatenate([row[:, None] * inv, col[:, None] * inv], axis=-1)
    cos, sin = np.cos(ang), np.sin(ang)
    cos_t = np.concatenate([cos, cos, cos, cos], axis=-1)
    sin_t = np.concatenate([-sin, sin, -sin, sin], axis=-1)
    cos_t = np.concatenate([cos_t, np.ones((ident_rows, LANE))], axis=0)
    sin_t = np.concatenate([sin_t, np.zeros((ident_rows, LANE))], axis=0)
    return jnp.asarray(cos_t, F32), jnp.asarray(sin_t, F32)


def _dft_tables(P, dtype):
    idx = np.arange(P, dtype=np.int64)
    ang = ((idx[:, None] * idx[None, :]) % (2 * P)).astype(np.float64) * (math.pi / P)
    return jnp.asarray(np.cos(ang), F32).astype(dtype), jnp.asarray(np.sin(ang), F32).astype(dtype)


def _filter_tables(L):
    t = np.linspace(0.0, 1.0, L)[:, None]
    wpos = (2.0 * math.pi / L) * np.arange(L, dtype=np.float64)[:, None]
    bands = np.linspace(1e-4, FILTER_BANDS - 1, FILTER_BANDS)[None, :]
    feats = np.concatenate([t, np.cos(bands * wpos), -np.sin(bands * wpos)], axis=-1)
    feats = np.pad(feats, ((0, 0), (0, LANE - FILTER_EMB)))
    return jnp.asarray(feats, F32), jnp.asarray(t, F32)


def _decay_rates():
    d = np.abs(np.linspace(math.log(DECAY_TARGET) / SLOW_DECAY, math.log(DECAY_TARGET) / FAST_DECAY, HY_W))
    return jnp.asarray(d[None, :], F32)


def _kr_permutation():
    p = np.zeros((2 * LANE, MLA_DR), np.float32)
    half = MLA_DR // 2
    for grp, odd in enumerate((0, 1, 0, 1, 1, 0, 1, 0)):
        for i in range(half):
            p[grp * half + i, 2 * i + odd] = 1.0
    return jnp.asarray(p, BF16)


_PACK_UNIT = MLA_DR
_PACK_BLK = 4 * _PACK_UNIT
_PACK_KR_STEP = R_OFF_KR // _PACK_BLK


def _pack_w_in_kernel(a0_ref, a1_ref, a2_ref, a3_ref, p_ref, o_ref):
    i = pl.program_id(1)

    @pl.when(i != _PACK_KR_STEP)
    def _():
        for k, a_ref in enumerate((a0_ref, a1_ref, a2_ref, a3_ref)):
            o_ref[pl.ds(k * _PACK_UNIT, _PACK_UNIT), :] = a_ref[...].astype(BF16)

    @pl.when(i == _PACK_KR_STEP)
    def _():
        o_ref[...] = _dot(p_ref[...], a0_ref[...].astype(BF16)).astype(BF16)


def _pack_w_in(w_in_t):
    depth, n, d = w_in_t.shape
    assert R_OFF_KR % _PACK_BLK == 0 and U_KR1 == R_OFF_KR and U_MLA_G == U_KR1 + _PACK_BLK
    perm = _kr_permutation()

    def src(k):
        def index_map(l, i):
            unit = jnp.where(i < _PACK_KR_STEP, 4 * i + k,
                             jnp.where(i == _PACK_KR_STEP, R_OFF_KR // _PACK_UNIT, 4 * i + k - 3))
            return (l, unit, 0)
        return pl.BlockSpec((None, _PACK_UNIT, d), index_map)

    return pl.pallas_call(
        _pack_w_in_kernel,
        grid=(depth, U_W // _PACK_BLK),
        in_specs=[src(0), src(1), src(2), src(3), pl.BlockSpec(perm.shape, lambda l, i: (0, 0))],
        out_specs=pl.BlockSpec((None, _PACK_BLK, d), lambda l, i: (l, i, 0)),
        out_shape=jax.ShapeDtypeStruct((depth, U_W, d), BF16),
        compiler_params=_cparams(("parallel", "parallel")),
        name="pack_w_in",
    )(w_in_t, w_in_t, w_in_t, w_in_t, perm)


def _pack_w_uq(w_uq):
    w = w_uq.reshape(DEPTH, Q_RANK, MLA_HEADS, MLA_DN + MLA_DR)
    nope = w[..., :MLA_DN].reshape(DEPTH, Q_RANK, MLA_W)
    a, b = w[..., MLA_DN::2], w[..., MLA_DN + 1::2]
    rot = jnp.concatenate([a, b], axis=-1).reshape(DEPTH, Q_RANK, MLA_HEADS * MLA_DR)
    swp = jnp.concatenate([b, a], axis=-1).reshape(DEPTH, Q_RANK, MLA_HEADS * MLA_DR)
    return jnp.concatenate([nope, rot, swp], axis=-1).astype(BF16)


def _pack_w_ukv(w_ukv):
    w = w_ukv.reshape(DEPTH, KV_RANK, MLA_HEADS, MLA_DN + MLA_DV)
    wk = w[..., :MLA_DN].reshape(DEPTH, KV_RANK, MLA_W)
    wv = w[..., MLA_DN:].reshape(DEPTH, KV_RANK, MLA_W)
    return wk.astype(BF16), jnp.swapaxes(wv, 1, 2).astype(BF16)


def kernel(x, c, ctx, c_ctx, norm_g, w_ada, b_ada, w_in, hy_conv_w, hy_conv_b, hf_w1, hf_b1, hf_freq,
           hf_w2, hf_b2, hf_w3, hy_bias, q_norm_g, w_uq, kv_norm_g, w_ukv, w_pool, pool_scale, w_out,
           final_norm_g):
    assert x.shape == (BATCH, SEQ, D_MODEL) and ctx.shape == (BATCH, CTX_LEN, D_MODEL)

    qkv_tr = 512
    cos_t, sin_t = _rope_lane_tables(qkv_tr)

    w_in_p = _pack_w_in(jnp.swapaxes(w_in, 1, 2))
    w_q_p = _pack_w_uq(w_uq)
    w_k_p, w_vt_p = _pack_w_ukv(w_ukv)
    w_out_b = w_out.astype(BF16)
    w_pool_b = w_pool.astype(BF16)
    w1_p = jnp.pad(hf_w1, ((0, 0), (0, LANE - FILTER_EMB), (0, 0)))
    w3_p = hf_w3.reshape(DEPTH, FILTER_HIDDEN, 2, 2, HY_W).transpose(0, 2, 3, 1, 4)
    conv_w = hy_conv_w.reshape(DEPTH, 3, 3, HY_W)
    conv_b = hy_conv_b.reshape(DEPTH, 3, HY_W)
    rows = lambda a: a[:, None, :]
    norm_g3, qg3, kg3, ps3 = rows(norm_g), rows(q_norm_g), rows(kv_norm_g), rows(pool_scale)
    b1_3, fr_3, b2_3 = rows(hf_b1), rows(hf_freq), rows(hf_b2)

    cond = jnp.concatenate([c, c_ctx[None], jnp.zeros((8 - BATCH - 1, D_MODEL), F32)], axis=0)
    mod = _adaln(cond, w_ada, b_ada[:, None, :])
    mod = mod.reshape(DEPTH, 8, 1, 3 * D_MODEL)

    x_all = jnp.concatenate([x.reshape(ROWS_L, D_MODEL), ctx.reshape(ROWS_C, D_MODEL)], axis=0)

    for l in range(DEPTH):
        last = l == DEPTH - 1
        u_all = _inproj(x_all, norm_g3, mod, w_in_p, l, n_row_tiles=ROWS_ALL // 1024)
        q, k, vt = _qkv(u_all, qg3, kg3, w_q_p, w_k_p, w_vt_p, cos_t, sin_t, l, tr=qkv_tr)
        mixers = []
        for L, row_blk0, latent in ((SEQ, 0, True), (CTX_LEN, ROWS_L // CTX_LEN, False)):
            if last and not latent:
                continue
            g, gn = _hyena_filter(L, w1_p, b1_3, fr_3, hf_w2, b2_3, w3_p, l)
            hy = _hyena(u_all, conv_w, conv_b, hy_bias, g, gn, l, L=L, row_blk0=row_blk0)
            att = _attention(q, k, vt, u_all, latent=latent, tq=256)
            po = _pool(u_all, w_pool_b, ps3, l, L=L, row_blk0=row_blk0)
            mixers.append((hy, att, po))
        x_all = _outproj(x_all, mixers, w_out_b, mod, final_norm_g[None, :], l, final=last)
    return x_all.reshape(BATCH, SEQ, D_MODEL)
```

```python
import functools
import math

import jax
import jax.numpy as jnp
import numpy as np
from jax import lax
from jax.experimental import pallas as pl
from jax.experimental.pallas import tpu as pltpu

F32 = jnp.float32
BF16 = jnp.bfloat16

D_MODEL = 2048
BATCH = 4
SEQ = 2048
DEPTH = 4
CTX_LEN = 256
GRID_W = 64
EPS = 1e-6

HY_W = 512
MLA_HEADS = 8
MLA_DN = 128
MLA_DR = 64
MLA_DV = 128
MLA_W = MLA_HEADS * MLA_DV
Q_RANK = 512
KV_RANK = 256
POOL_W = 512
POOL_WINDOWS = (2, 4, 8, 16)
POOL_GROUP = 128

FILTER_EMB = 33
FILTER_BANDS = 16
FILTER_HIDDEN = 64
DECAY_TARGET = 1e-2
FAST_DECAY = 0.3
SLOW_DECAY = 1.5
ROPE_BASE = 10000.0
ATTN_SCALE = (MLA_DN + MLA_DR) ** -0.5

R_OFF_KR = 2816
R_OFF_MLA_G = 2880

U_HY = 0
U_Q = 2048
U_KV = 2560
U_KR1 = 2816
U_KR2 = 2944
U_MLA_G = 3072
U_POOL = 4096
U_POOL_G = 4608
U_W = 5120

ROWS_L = BATCH * SEQ
ROWS_C = BATCH * CTX_LEN
ROWS_ALL = ROWS_L + ROWS_C

VMEM_LIMIT_BYTES = 56 * 1024 * 1024
LANE = 128


def _cparams(sem):
    return pltpu.CompilerParams(dimension_semantics=sem, vmem_limit_bytes=VMEM_LIMIT_BYTES)


def _silu(x):
    return x * jax.nn.sigmoid(x)


def _dot(a, b):
    return jnp.dot(a, b, preferred_element_type=F32)


def _dot_bf16x3(a, b):
    a_hi, b_hi = a.astype(BF16), b.astype(BF16)
    a_lo = (a - a_hi.astype(F32)).astype(BF16)
    b_lo = (b - b_hi.astype(F32)).astype(BF16)
    return _dot(a_hi, b_hi) + (_dot(a_hi, b_lo) + _dot(a_lo, b_hi))


def _dot_nt(a, b):
    return lax.dot_general(a, b, (((1,), (1,)), ((), ())), preferred_element_type=F32)


def _adaln_kernel(c_ref, w_ref, b_ref, o_ref):
    a = _silu(c_ref[...]).astype(BF16)
    o_ref[...] = _dot(a, w_ref[...].astype(BF16)) + b_ref[...]


def _adaln(cond, w_ada, b_ada, tn=1024):
    depth, d, n = w_ada.shape
    return pl.pallas_call(
        _adaln_kernel,
        grid=(depth, n // tn),
        in_specs=[
            pl.BlockSpec((8, d), lambda l, j: (0, 0)),
            pl.BlockSpec((None, d, tn), lambda l, j: (l, 0, j)),
            pl.BlockSpec((None, 1, tn), lambda l, j: (l, 0, j)),
        ],
        out_specs=pl.BlockSpec((None, 8, tn), lambda l, j: (l, 0, j)),
        out_shape=jax.ShapeDtypeStruct((depth, 8, n), F32),
        compiler_params=_cparams(("parallel", "parallel")),
        name="adaln",
    )(cond, w_ada, b_ada)


def _inproj_kernel(x_ref, g_ref, sh_ref, sc_ref, w_ref, o_ref, h_ref, *, sub):
    first = pl.program_id(1) == 0

    @pl.when(first)
    def _():
        gs = g_ref[...] * (1.0 + sc_ref[...])
        sh = sh_ref[...]
        for r in range(0, x_ref.shape[0], sub):
            rows = pl.ds(r, sub)
            x = x_ref[rows, :]
            ms = jnp.mean(x * x, axis=-1, keepdims=True)
            h = (x * lax.rsqrt(ms + EPS) * gs + sh).astype(BF16)
            h_ref[rows, :] = h
            o_ref[rows, :] = _dot_nt(h, w_ref[...]).astype(o_ref.dtype)

    @pl.when(jnp.logical_not(first))
    def _():
        o_ref[...] = _dot_nt(h_ref[...], w_ref[...]).astype(o_ref.dtype)


def _layer_spec(arr, l):
    zeros = (0,) * (arr.ndim - 1)
    return pl.BlockSpec((None,) + arr.shape[1:], lambda *_: (l,) + zeros)


def _inproj(x_all, g, mod, w_t, l, *, n_row_tiles, tm=1024, tn=1280):
    d = x_all.shape[1]
    n = w_t.shape[1]
    tiles_per_batch = SEQ // tm

    def mod_row(i):
        return jnp.minimum(i // tiles_per_batch, BATCH)

    return pl.pallas_call(
        functools.partial(_inproj_kernel, sub=256),
        grid=(n_row_tiles, n // tn),
        in_specs=[
            pl.BlockSpec((tm, d), lambda i, j: (i, 0)),
            _layer_spec(g, l),
            pl.BlockSpec((None, None, 1, d), lambda i, j: (l, mod_row(i), 0, 0)),
            pl.BlockSpec((None, None, 1, d), lambda i, j: (l, mod_row(i), 0, 1)),
            pl.BlockSpec((None, tn, d), lambda i, j: (l, j, 0)),
        ],
        out_specs=pl.BlockSpec((tm, tn), lambda i, j: (i, j)),
        out_shape=jax.ShapeDtypeStruct((x_all.shape[0], n), BF16),
        scratch_shapes=[pltpu.VMEM((tm, d), BF16)],
        compiler_params=_cparams(("parallel", "arbitrary")),
        name="inproj",
    )(x_all, g, mod, mod, w_t)


def _hyena_block(L):
    return min(L, 512)


def _filter_kernel(feats_ref, t_ref, dl_ref, w1_ref, b1_ref, fr_ref, w2_ref, b2_ref, w3_ref,
                   c_ref, s_ref, g_ref, gn_ref, h_ref, *, L, P):
    hp = lax.Precision.HIGHEST
    hdot = lambda a, b: jnp.dot(a, b, precision=hp, preferred_element_type=F32)
    nb = L // P
    m_fft = 2 * P

    @pl.when(pl.program_id(0) == 0)
    def _():
        fr = fr_ref[...]
        h1 = jnp.sin(fr * (hdot(feats_ref[...], w1_ref[...]) + b1_ref[...]))
        h_ref[...] = jnp.sin(fr * (hdot(h1, w2_ref[...]) + b2_ref[...]))

    h = h_ref[...]
    decay = jnp.exp(-t_ref[...] * dl_ref[...])
    row = lax.broadcasted_iota(jnp.int32, (L, 1), 0)
    rp = lax.broadcasted_iota(jnp.int32, (P, 1), 0)
    sgn = jnp.where(rp % 2 == 0, 1.0, -1.0).astype(F32)
    wgt = jnp.where(rp == 0, 1.0 / m_fft, 2.0 / m_fft).astype(F32)
    cm = c_ref[...]
    sm = s_ref[...]
    hf = _dot_bf16x3(h, w3_ref[0]) * decay
    hb = _dot_bf16x3(h, w3_ref[1]) * decay
    hb = jnp.where(row == 0, 0.0, hb)
    nrm = lax.rsqrt(jnp.sum(hf * hf + hb * hb, axis=0, keepdims=True) + EPS)
    F, B = [], []
    for arr, out in ((hf * nrm, F), (hb * nrm, B)):
        for j in range(nb):
            blk = arr[j * P:(j + 1) * P].astype(BF16)
            b32 = blk.astype(F32)
            out.append((_dot(cm, blk), _dot(sm, blk), jnp.sum(b32 * sgn, axis=0, keepdims=True), b32[0:1]))
    for d in range(-(nb - 1), nb):
        if d >= 1:
            gr = F[d][0] + sgn * (F[d - 1][0] - F[d - 1][3])
            gi = F[d][1] + sgn * F[d - 1][1]
            gn = F[d][2] + F[d - 1][2] - F[d - 1][3]
        elif d == 0:
            gr = F[0][0] + B[0][0]
            gi = F[0][1] - B[0][1]
            gn = F[0][2] + B[0][2]
        else:
            e = -d
            gr = B[e][0] + sgn * (B[e - 1][0] - B[e - 1][3])
            gi = -B[e][1] - sgn * B[e - 1][1]
            gn = B[e][2] + B[e - 1][2] - B[e - 1][3]
        g_ref[d + nb - 1, 0] = gr * wgt
        g_ref[d + nb - 1, 1] = gi * wgt
        gn_ref[d + nb - 1] = gn * (1.0 / m_fft)


def _hyena_filter(L, w1, b1, fr, w2, b2, w3, l, *, cb=256):
    P = _hyena_block(L)
    nd = 2 * (L // P) - 1
    feats, t = _filter_tables(L)
    deltas = _decay_rates()
    cmat, smat = _dft_tables(P, BF16)
    full2 = lambda s: (0, 0)
    return pl.pallas_call(
        functools.partial(_filter_kernel, L=L, P=P),
        grid=(2 * (HY_W // cb),),
        in_specs=[
            pl.BlockSpec(feats.shape, full2),
            pl.BlockSpec(t.shape, full2),
            pl.BlockSpec((1, cb), lambda s: (0, s // 2)),
            _layer_spec(w1, l), _layer_spec(b1, l), _layer_spec(fr, l), _layer_spec(w2, l), _layer_spec(b2, l),
            pl.BlockSpec((None, None, 2, FILTER_HIDDEN, cb), lambda s: (l, s % 2, 0, 0, s // 2)),
            pl.BlockSpec((P, P), full2),
            pl.BlockSpec((P, P), full2),
        ],
        out_specs=[
            pl.BlockSpec((None, nd, 2, P, cb), lambda s: (s % 2, 0, 0, 0, s // 2)),
            pl.BlockSpec((None, nd, 1, cb), lambda s: (s % 2, 0, 0, s // 2)),
        ],
        out_shape=[
            jax.ShapeDtypeStruct((2, nd, 2, P, HY_W), F32),
            jax.ShapeDtypeStruct((2, nd, 1, HY_W), F32),
        ],
        scratch_shapes=[pltpu.VMEM((L, FILTER_HIDDEN), F32)],
        compiler_params=_cparams(("arbitrary",)),
        name=f"hyena_filter_{L}",
    )(feats, t, deltas, w1, b1, fr, w2, b2, w3, cmat, smat)


_MAC_ROWS = 32


def _hyena_kernel(v_ref, x1_ref, x2_ref, gate_ref, cw_ref, cb_ref, hb_ref, g_ref, gn_ref,
                  c_ref, s_ref, o_ref, z_ref, zb_ref, xc_ref, zf_ref, yr_ref, yi_ref, *, L, P):
    nb = L // P

    def short_conv(dst_ref, src_ref, p):
        x = src_ref[...].astype(F32)
        w = cw_ref[:, p, :]
        bias = cb_ref[p:p + 1, :]
        prev = pltpu.roll(x, 1, axis=0)
        nxt = pltpu.roll(x, L - 1, axis=0)
        dst_ref[...] = prev * w[0:1] + x * w[1:2] + nxt * w[2:3] + bias
        dst_ref[0:1, :] = x[0:1] * w[1:2] + x[1:2] * w[2:3] + bias
        dst_ref[L - 1:L, :] = x[L - 2:L - 1] * w[0:1] + x[L - 1:L] * w[1:2] + bias

    sign = jnp.where(lax.broadcasted_iota(jnp.int32, (P, 1), 0) % 2 == 0, 1.0, -1.0).astype(F32)
    cm = c_ref[...]
    sm = s_ref[...]

    short_conv(z_ref, v_ref, 0)
    for o in range(2):
        short_conv(xc_ref, x1_ref if o == 0 else x2_ref, 1 + o)
        zb_ref[...] = z_ref[...].astype(BF16)
        zn = []
        for j in range(nb):
            rows = pl.ds(j * P, P)
            zf_ref[0, j] = _dot(cm, zb_ref[rows, :])
            zf_ref[1, j] = _dot(sm, zb_ref[rows, :])
            zn.append(jnp.sum(z_ref[rows, :] * sign, axis=0, keepdims=True))
        bias = hb_ref[o:o + 1, :]
        def spectrum_product(i):
            for r in range(0, P, _MAC_ROWS):
                rr = pl.ds(r, _MAC_ROWS)
                yr = yi = None
                for j in range(nb):
                    d = i - j + nb - 1
                    gr, gi = g_ref[o, d, 0, rr, :], g_ref[o, d, 1, rr, :]
                    zr, zi = zf_ref[0, j, rr, :], zf_ref[1, j, rr, :]
                    pr, pi = gr * zr - gi * zi, gr * zi + gi * zr
                    yr, yi = (pr, pi) if yr is None else (yr + pr, yi + pi)
                yr_ref[i, rr, :] = yr.astype(BF16)
                yi_ref[i, rr, :] = yi.astype(BF16)

        spectrum_product(0)
        for i in range(nb):
            if i + 1 < nb:
                spectrum_product(i + 1)
            yn = gn_ref[o, i + nb - 1] * zn[0]
            for j in range(1, nb):
                yn = yn + gn_ref[o, i - j + nb - 1] * zn[j]
            rows = pl.ds(i * P, P)
            y = _dot(cm, yr_ref[i]) + _dot(sm, yi_ref[i])
            y = y + sign * yn + z_ref[rows, :] * bias
            y = xc_ref[rows, :] * y
            if o == 0:
                z_ref[rows, :] = y
            else:
                o_ref[rows, :] = (y * _silu(gate_ref[rows, :].astype(F32))).astype(o_ref.dtype)


def _hyena(u_all, conv_w, conv_b, hy_bias, g, gn, l, *, L, row_blk0, cb=256):
    P = _hyena_block(L)
    nb = L // P
    nd = 2 * nb - 1
    ncb = HY_W // cb
    cmat, smat = _dft_tables(P, BF16)

    def ublk(part):
        return pl.BlockSpec((L, cb), lambda j, b: (row_blk0 + b, part * ncb + j))

    in_specs = [
        ublk(0), ublk(1), ublk(2), ublk(3),
        pl.BlockSpec((None, 3, 3, cb), lambda j, b: (l, 0, 0, j)),
        pl.BlockSpec((None, 3, cb), lambda j, b: (l, 0, j)),
        pl.BlockSpec((None, 2, cb), lambda j, b: (l, 0, j)),
        pl.BlockSpec((2, nd, 2, P, cb), lambda j, b: (0, 0, 0, 0, j)),
        pl.BlockSpec((2, nd, 1, cb), lambda j, b: (0, 0, 0, j)),
        pl.BlockSpec((P, P), lambda j, b: (0, 0)),
        pl.BlockSpec((P, P), lambda j, b: (0, 0)),
    ]
    args = [u_all, u_all, u_all, u_all, conv_w, conv_b, hy_bias, g, gn, cmat, smat]
    return pl.pallas_call(
        functools.partial(_hyena_kernel, L=L, P=P),
        grid=(ncb, BATCH),
        in_specs=in_specs,
        out_specs=pl.BlockSpec((L, cb), lambda j, b: (b, j)),
        out_shape=jax.ShapeDtypeStruct((BATCH * L, HY_W), BF16),
        scratch_shapes=[
            pltpu.VMEM((L, cb), F32),
            pltpu.VMEM((L, cb), BF16),
            pltpu.VMEM((L, cb), F32),
            pltpu.VMEM((2, nb, P, cb), F32),
            pltpu.VMEM((nb, P, cb), BF16),
            pltpu.VMEM((nb, P, cb), BF16),
        ],
        compiler_params=_cparams(("parallel", "parallel")),
        name=f"hyena_{L}",
    )(*args)


_Q_SCALE = ATTN_SCALE * math.log2(math.e)
_VT_ROWS = MLA_DV + 16


def _qkv_kernel(uq_ref, ukv_ref, k1_ref, k2_ref, qg_ref, kg_ref, wq_ref, wk_ref, wvt_ref, cos_ref, sin_ref,
                q_ref, k_ref, vt_ref):
    def rms(x, g):
        x = x.astype(F32)
        return (x * lax.rsqrt(jnp.mean(x * x, axis=-1, keepdims=True) + EPS) * g).astype(BF16)

    cos = cos_ref[...]
    sin = sin_ref[...]
    lane = lax.broadcasted_iota(jnp.int32, (1, LANE), 1)
    half_mask = [(lane < MLA_DR), (lane >= MLA_DR)]

    qa = _dot(rms(uq_ref[...], qg_ref[...]), wq_ref[...])
    kvn = rms(ukv_ref[...], kg_ref[...])
    kn = _dot(kvn, wk_ref[...])
    vt = _dot_nt(wvt_ref[...], kvn)
    k_rope = (k1_ref[...].astype(F32) * cos + k2_ref[...].astype(F32) * sin).astype(BF16)
    n_rot = MLA_HEADS * MLA_DR
    for h in range(MLA_HEADS):
        c = h // 2
        qr = qa[:, MLA_W + c * LANE:MLA_W + (c + 1) * LANE]
        qs = qa[:, MLA_W + n_rot + c * LANE:MLA_W + n_rot + (c + 1) * LANE]
        rot = jnp.where(half_mask[h % 2], qr * cos + qs * sin, 0.0)
        q_ref[h, :, 0:LANE] = (qa[:, h * LANE:(h + 1) * LANE] * _Q_SCALE).astype(BF16)
        q_ref[h, :, LANE:2 * LANE] = (rot * _Q_SCALE).astype(BF16)
        k_ref[h, :, 0:LANE] = kn[:, h * LANE:(h + 1) * LANE].astype(BF16)
        k_ref[h, :, LANE:2 * LANE] = k_rope
        vt_ref[h, 0:MLA_DV, :] = vt[h * MLA_DV:(h + 1) * MLA_DV, :].astype(BF16)
        vt_ref[h, MLA_DV:_VT_ROWS, :] = jnp.ones((_VT_ROWS - MLA_DV, vt.shape[1]), BF16)


def _qkv(u_all, qg, kg, wq, wk, wvt, cos_t, sin_t, l, *, tr=512):
    n_lat = ROWS_L // tr
    per_seq = SEQ // tr

    def tab(i):
        return (jnp.where(i < n_lat, i % per_seq, per_seq), 0)

    return pl.pallas_call(
        _qkv_kernel,
        grid=(ROWS_ALL // tr,),
        in_specs=[
            pl.BlockSpec((tr, Q_RANK), lambda i: (i, U_Q // Q_RANK)),
            pl.BlockSpec((tr, KV_RANK), lambda i: (i, U_KV // KV_RANK)),
            pl.BlockSpec((tr, LANE), lambda i: (i, U_KR1 // LANE)),
            pl.BlockSpec((tr, LANE), lambda i: (i, U_KR2 // LANE)),
            _layer_spec(qg, l), _layer_spec(kg, l), _layer_spec(wq, l), _layer_spec(wk, l), _layer_spec(wvt, l),
            pl.BlockSpec((tr, LANE), tab),
            pl.BlockSpec((tr, LANE), tab),
        ],
        out_specs=[
            pl.BlockSpec((MLA_HEADS, tr, 2 * LANE), lambda i: (0, i, 0)),
            pl.BlockSpec((MLA_HEADS, tr, 2 * LANE), lambda i: (0, i, 0)),
            pl.BlockSpec((MLA_HEADS, _VT_ROWS, tr), lambda i: (0, 0, i)),
        ],
        out_shape=[
            jax.ShapeDtypeStruct((MLA_HEADS, ROWS_ALL, 2 * LANE), BF16),
            jax.ShapeDtypeStruct((MLA_HEADS, ROWS_ALL, 2 * LANE), BF16),
            jax.ShapeDtypeStruct((MLA_HEADS, _VT_ROWS, ROWS_ALL), BF16),
        ],
        compiler_params=_cparams(("parallel",)),
        name="qkv",
    )(u_all, u_all, u_all, u_all, qg, kg, wq, wk, wvt, cos_t, sin_t)


_KEY_CHUNK = 1024
_SCORE_LEAD = 1


def _fold_max(x):
    rows = x.shape[0]
    while rows > 8 and rows % 16 == 0:
        rows //= 2
        x = jnp.maximum(x[:rows], x[rows:])
    return x.max(axis=0, keepdims=True)


def _attn_kernel(*refs, n_src):
    q_ref = refs[0]
    k_refs = refs[1:1 + n_src]
    vt_refs = refs[1 + n_src:1 + 2 * n_src]
    g_ref = refs[1 + 2 * n_src]
    o_ref = refs[2 + 2 * n_src]
    chunks = [(k, vt, r0, min(_KEY_CHUNK, k.shape[1] - r0))
              for k, vt in zip(k_refs, vt_refs) for r0 in range(0, k.shape[1], _KEY_CHUNK)]

    def scores(h):
        return [_dot_nt(k[h, r0:r0 + n, :], q_ref[h]) for k, _, r0, n in chunks]

    pending = [scores(h) for h in range(_SCORE_LEAD)]
    for h in range(MLA_HEADS):
        if h + _SCORE_LEAD < MLA_HEADS:
            pending.append(scores(h + _SCORE_LEAD))
        s = pending.pop(0)
        m = _fold_max(s[0])
        for si in s[1:]:
            m = jnp.maximum(m, _fold_max(si))
        acc = None
        for si, (_, vt, r0, n) in zip(s, chunks):
            pv = _dot(vt[h, :, r0:r0 + n], jnp.exp2((si - m).astype(BF16)))
            acc = pv if acc is None else acc + pv
        out = (acc[0:MLA_DV] / acc[MLA_DV:MLA_DV + 1]).T
        gate = _silu(g_ref[:, h * LANE:(h + 1) * LANE].astype(F32))
        o_ref[:, h * LANE:(h + 1) * LANE] = (out * gate).astype(o_ref.dtype)


def _attention(q, k, vt, u_all, *, latent, tq=256):
    H = MLA_HEADS
    ctx_blk0 = ROWS_L // CTX_LEN
    if latent:
        n_q = SEQ // tq
        q_off = 0
        k_specs = [pl.BlockSpec((H, SEQ, 2 * LANE), lambda b, i: (0, b, 0)),
                   pl.BlockSpec((H, CTX_LEN, 2 * LANE), lambda b, i: (0, ctx_blk0 + b, 0))]
        v_specs = [pl.BlockSpec((H, _VT_ROWS, SEQ), lambda b, i: (0, 0, b)),
                   pl.BlockSpec((H, _VT_ROWS, CTX_LEN), lambda b, i: (0, 0, ctx_blk0 + b))]
    else:
        n_q = CTX_LEN // tq
        q_off = ROWS_L // tq
        k_specs = [pl.BlockSpec((H, CTX_LEN, 2 * LANE), lambda b, i: (0, ctx_blk0 + b, 0))]
        v_specs = [pl.BlockSpec((H, _VT_ROWS, CTX_LEN), lambda b, i: (0, 0, ctx_blk0 + b))]
    n_src = len(k_specs)
    in_specs = ([pl.BlockSpec((H, tq, 2 * LANE), lambda b, i: (0, q_off + b * n_q + i, 0))] + k_specs + v_specs
                + [pl.BlockSpec((tq, MLA_W), lambda b, i: (q_off + b * n_q + i, U_MLA_G // MLA_W))])
    args = [q] + [k] * n_src + [vt] * n_src + [u_all]
    return pl.pallas_call(
        functools.partial(_attn_kernel, n_src=n_src),
        grid=(BATCH, n_q),
        in_specs=in_specs,
        out_specs=pl.BlockSpec((tq, MLA_W), lambda b, i: (b * n_q + i, 0)),
        out_shape=jax.ShapeDtypeStruct((BATCH * n_q * tq, MLA_W), BF16),
        compiler_params=_cparams(("parallel", "parallel")),
        name="attn_latent" if latent else "attn_ctx",
    )(*args)


_POOL_PAD = 16


def _pool_kernel(x_ref, g_ref, w_ref, sc_ref, o_ref, pad_ref, *, L):
    zeros = jnp.zeros((_POOL_PAD, POOL_GROUP), F32)
    pad_ref[pl.ds(0, _POOL_PAD), :] = zeros
    pad_ref[pl.ds(L + _POOL_PAD, _POOL_PAD), :] = zeros
    t = lax.broadcasted_iota(jnp.int32, (L, 1), 0)
    for gi, win in enumerate(POOL_WINDOWS):
        half = win // 2
        cols = slice(gi * POOL_GROUP, (gi + 1) * POOL_GROUP)
        x = x_ref[:, cols].astype(F32)
        pad_ref[pl.ds(_POOL_PAD, L), :] = x
        acc = pad_ref[pl.ds(_POOL_PAD - half, L), :]
        for j in range(-half + 1, half):
            acc = acc + pad_ref[pl.ds(_POOL_PAD + j, L), :]
        cnt = (jnp.minimum(t + half, L) - jnp.maximum(t - half, 0)).astype(F32)
        dlt = (acc / cnt - x).astype(BF16)
        y = _dot(dlt, w_ref[gi]) * sc_ref[:, cols]
        o_ref[:, cols] = (y * _silu(g_ref[:, cols].astype(F32))).astype(o_ref.dtype)


def _pool(u_all, w_pool, pool_scale, l, *, L, row_blk0):
    return pl.pallas_call(
        functools.partial(_pool_kernel, L=L),
        grid=(BATCH,),
        in_specs=[
            pl.BlockSpec((L, POOL_W), lambda b: (row_blk0 + b, U_POOL // POOL_W)),
            pl.BlockSpec((L, POOL_W), lambda b: (row_blk0 + b, U_POOL_G // POOL_W)),
            _layer_spec(w_pool, l),
            _layer_spec(pool_scale, l),
        ],
        out_specs=pl.BlockSpec((L, POOL_W), lambda b: (b, 0)),
        out_shape=jax.ShapeDtypeStruct((BATCH * L, POOL_W), BF16),
        scratch_shapes=[pltpu.VMEM((L + 2 * _POOL_PAD, POOL_GROUP), F32)],
        compiler_params=_cparams(("parallel",)),
        name=f"pool_{L}",
    )(u_all, u_all, w_pool, pool_scale)


def _outproj_kernel(x_ref, *refs, n_lat_tiles, n_streams, final, sub):
    mixers = [refs[3 * s:3 * s + 3] for s in range(n_streams)]
    w_ref, gt_ref, fg_ref, o_ref = refs[3 * n_streams:]

    def run(hy_ref, at_ref, po_ref):
        for r in range(0, x_ref.shape[0], sub):
            rows = pl.ds(r, sub)
            acc = (_dot(hy_ref[rows, :], w_ref[0:HY_W, :]) + _dot(at_ref[rows, :], w_ref[HY_W:HY_W + MLA_W, :])
                   + _dot(po_ref[rows, :], w_ref[HY_W + MLA_W:, :]))
            y = x_ref[rows, :] + gt_ref[...] * acc
            if final:
                y = y * lax.rsqrt(jnp.mean(y * y, axis=-1, keepdims=True) + EPS) * fg_ref[...]
            o_ref[rows, :] = y

    if n_streams == 1:
        run(*mixers[0])
    else:
        is_latent = pl.program_id(0) < n_lat_tiles
        pl.when(is_latent)(lambda: run(*mixers[0]))
        pl.when(jnp.logical_not(is_latent))(lambda: run(*mixers[1]))


def _outproj(x_all, mixers, w_out, mod, fg, l, *, final, tm=512):
    d = x_all.shape[1]
    tiles_per_batch = SEQ // tm
    n_lat_tiles = ROWS_L // tm
    n_row_tiles = sum(m[0].shape[0] for m in mixers) // tm

    def mod_row(i):
        return jnp.minimum(i // tiles_per_batch, BATCH)

    lat_blk = lambda i: (jnp.minimum(i, n_lat_tiles - 1), 0)
    ctx_blk = lambda i: (jnp.maximum(i - n_lat_tiles, 0), 0)
    mixer_specs, mixer_args = [], []
    for blk, (hy, att, po) in zip((lat_blk, ctx_blk), mixers):
        mixer_specs += [pl.BlockSpec((tm, HY_W), blk), pl.BlockSpec((tm, MLA_W), blk), pl.BlockSpec((tm, POOL_W), blk)]
        mixer_args += [hy, att, po]

    return pl.pallas_call(
        functools.partial(_outproj_kernel, n_lat_tiles=n_lat_tiles, n_streams=len(mixers), final=final, sub=256),
        grid=(n_row_tiles,),
        in_specs=[pl.BlockSpec((tm, d), lambda i: (i, 0))] + mixer_specs + [
            pl.BlockSpec((None,) + w_out.shape[1:], lambda i: (l, 0, 0), pipeline_mode=pl.Buffered(1)),
            pl.BlockSpec((None, None, 1, d), lambda i: (l, mod_row(i), 0, 2)),
            pl.BlockSpec((1, d), lambda i: (0, 0)),
        ],
        out_specs=pl.BlockSpec((tm, d), lambda i: (i, 0)),
        out_shape=jax.ShapeDtypeStruct((n_row_tiles * tm, d), F32),
        compiler_params=_cparams(("parallel",)),
        name="outproj_final" if final else "outproj",
    )(x_all, *mixer_args, w_out, mod, fg)


def _rope_lane_tables(ident_rows):
    n_rows = SEQ // GRID_W
    row = np.repeat(np.arange(n_rows, dtype=np.float64), GRID_W)
    col = np.tile(np.arange(GRID_W, dtype=np.float64), n_rows)
    n_freq = MLA_DR // 4
    inv = ROPE_BASE ** (-np.arange(n_freq, dtype=np.float64) / n_freq)
    ang = np.concatenate([row[:, None] * inv, col[:, None] * inv], axis=-1)
    cos, sin = np.cos(ang), np.sin(ang)
    cos_t = np.concatenate([cos, cos, cos, cos], axis=-1)
    sin_t = np.concatenate([-sin, sin, -sin, sin], axis=-1)
    cos_t = np.concatenate([cos_t, np.ones((ident_rows, LANE))], axis=0)
    sin_t = np.concatenate([sin_t, np.zeros((ident_rows, LANE))], axis=0)
    return jnp.asarray(cos_t, F32), jnp.asarray(sin_t, F32)


def _dft_tables(P, dtype):
    idx = np.arange(P, dtype=np.int64)
    ang = ((idx[:, None] * idx[None, :]) % (2 * P)).astype(np.float64) * (math.pi / P)
    return jnp.asarray(np.cos(ang), F32).astype(dtype), jnp.asarray(np.sin(ang), F32).astype(dtype)


def _filter_tables(L):
    t = np.linspace(0.0, 1.0, L)[:, None]
    wpos = (2.0 * math.pi / L) * np.arange(L, dtype=np.float64)[:, None]
    bands = np.linspace(1e-4, FILTER_BANDS - 1, FILTER_BANDS)[None, :]
    feats = np.concatenate([t, np.cos(bands * wpos), -np.sin(bands * wpos)], axis=-1)
    feats = np.pad(feats, ((0, 0), (0, LANE - FILTER_EMB)))
    return jnp.asarray(feats, F32), jnp.asarray(t, F32)


def _decay_rates():
    d = np.abs(np.linspace(math.log(DECAY_TARGET) / SLOW_DECAY, math.log(DECAY_TARGET) / FAST_DECAY, HY_W))
    return jnp.asarray(d[None, :], F32)


def _kr_permutation():
    p = np.zeros((2 * LANE, MLA_DR), np.float32)
    half = MLA_DR // 2
    for grp, odd in enumerate((0, 1, 0, 1, 1, 0, 1, 0)):
        for i in range(half):
            p[grp * half + i, 2 * i + odd] = 1.0
    return jnp.asarray(p, BF16)


_PACK_UNIT = MLA_DR
_PACK_BLK = 4 * _PACK_UNIT
_PACK_KR_STEP = R_OFF_KR // _PACK_BLK


def _pack_w_in_kernel(a0_ref, a1_ref, a2_ref, a3_ref, p_ref, o_ref):
    i = pl.program_id(1)

    @pl.when(i != _PACK_KR_STEP)
    def _():
        for k, a_ref in enumerate((a0_ref, a1_ref, a2_ref, a3_ref)):
            o_ref[pl.ds(k * _PACK_UNIT, _PACK_UNIT), :] = a_ref[...].astype(BF16)

    @pl.when(i == _PACK_KR_STEP)
    def _():
        o_ref[...] = _dot(p_ref[...], a0_ref[...].astype(BF16)).astype(BF16)


def _pack_w_in(w_in_t):
    depth, n, d = w_in_t.shape
    assert R_OFF_KR % _PACK_BLK == 0 and U_KR1 == R_OFF_KR and U_MLA_G == U_KR1 + _PACK_BLK
    perm = _kr_permutation()

    def src(k):
        def index_map(l, i):
            unit = jnp.where(i < _PACK_KR_STEP, 4 * i + k,
                             jnp.where(i == _PACK_KR_STEP, R_OFF_KR // _PACK_UNIT, 4 * i + k - 3))
            return (l, unit, 0)
        return pl.BlockSpec((None, _PACK_UNIT, d), index_map)

    return pl.pallas_call(
        _pack_w_in_kernel,
        grid=(depth, U_W // _PACK_BLK),
        in_specs=[src(0), src(1), src(2), src(3), pl.BlockSpec(perm.shape, lambda l, i: (0, 0))],
        out_specs=pl.BlockSpec((None, _PACK_BLK, d), lambda l, i: (l, i, 0)),
        out_shape=jax.ShapeDtypeStruct((depth, U_W, d), BF16),
        compiler_params=_cparams(("parallel", "parallel")),
        name="pack_w_in",
    )(w_in_t, w_in_t, w_in_t, w_in_t, perm)


def _pack_w_uq(w_uq):
    w = w_uq.reshape(DEPTH, Q_RANK, MLA_HEADS, MLA_DN + MLA_DR)
    nope = w[..., :MLA_DN].reshape(DEPTH, Q_RANK, MLA_W)
    a, b = w[..., MLA_DN::2], w[..., MLA_DN + 1::2]
    rot = jnp.concatenate([a, b], axis=-1).reshape(DEPTH, Q_RANK, MLA_HEADS * MLA_DR)
    swp = jnp.concatenate([b, a], axis=-1).reshape(DEPTH, Q_RANK, MLA_HEADS * MLA_DR)
    return jnp.concatenate([nope, rot, swp], axis=-1).astype(BF16)


def _pack_w_ukv(w_ukv):
    w = w_ukv.reshape(DEPTH, KV_RANK, MLA_HEADS, MLA_DN + MLA_DV)
    wk = w[..., :MLA_DN].reshape(DEPTH, KV_RANK, MLA_W)
    wv = w[..., MLA_DN:].reshape(DEPTH, KV_RANK, MLA_W)
    return wk.astype(BF16), jnp.swapaxes(wv, 1, 2).astype(BF16)


def kernel(x, c, ctx, c_ctx, norm_g, w_ada, b_ada, w_in, hy_conv_w, hy_conv_b, hf_w1, hf_b1, hf_freq,
           hf_w2, hf_b2, hf_w3, hy_bias, q_norm_g, w_uq, kv_norm_g, w_ukv, w_pool, pool_scale, w_out,
           final_norm_g):
    assert x.shape == (BATCH, SEQ, D_MODEL) and ctx.shape == (BATCH, CTX_LEN, D_MODEL)

    qkv_tr = 512
    cos_t, sin_t = _rope_lane_tables(qkv_tr)

    w_in_p = _pack_w_in(jnp.swapaxes(w_in, 1, 2))
    w_q_p = _pack_w_uq(w_uq)
    w_k_p, w_vt_p = _pack_w_ukv(w_ukv)
    w_out_b = w_out.astype(BF16)
    w_pool_b = w_pool.astype(BF16)
    w1_p = jnp.pad(hf_w1, ((0, 0), (0, LANE - FILTER_EMB), (0, 0)))
    w3_p = hf_w3.reshape(DEPTH, FILTER_HIDDEN, 2, 2, HY_W).transpose(0, 2, 3, 1, 4)
    conv_w = hy_conv_w.reshape(DEPTH, 3, 3, HY_W)
    conv_b = hy_conv_b.reshape(DEPTH, 3, HY_W)
    rows = lambda a: a[:, None, :]
    norm_g3, qg3, kg3, ps3 = rows(norm_g), rows(q_norm_g), rows(kv_norm_g), rows(pool_scale)
    b1_3, fr_3, b2_3 = rows(hf_b1), rows(hf_freq), rows(hf_b2)

    cond = jnp.concatenate([c, c_ctx[None], jnp.zeros((8 - BATCH - 1, D_MODEL), F32)], axis=0)
    mod = _adaln(cond, w_ada, b_ada[:, None, :])
    mod = mod.reshape(DEPTH, 8, 1, 3 * D_MODEL)

    x_all = jnp.concatenate([x.reshape(ROWS_L, D_MODEL), ctx.reshape(ROWS_C, D_MODEL)], axis=0)

    for l in range(DEPTH):
        last = l == DEPTH - 1
        u_all = _inproj(x_all, norm_g3, mod, w_in_p, l, n_row_tiles=ROWS_ALL // 1024)
        q, k, vt = _qkv(u_all, qg3, kg3, w_q_p, w_k_p, w_vt_p, cos_t, sin_t, l, tr=qkv_tr)
        mixers = []
        for L, row_blk0, latent in ((SEQ, 0, True), (CTX_LEN, ROWS_L // CTX_LEN, False)):
            if last and not latent:
                continue
            g, gn = _hyena_filter(L, w1_p, b1_3, fr_3, hf_w2, b2_3, w3_p, l)
            hy = _hyena(u_all, conv_w, conv_b, hy_bias, g, gn, l, L=L, row_blk0=row_blk0)
            att = _attention(q, k, vt, u_all, latent=latent, tq=256)
            po = _pool(u_all, w_pool_b, ps3, l, L=L, row_blk0=row_blk0)
            mixers.append((hy, att, po))
        x_all = _outproj(x_all, mixers, w_out_b, mod, final_norm_g[None, :], l, final=last)
    return x_all.reshape(BATCH, SEQ, D_MODEL)
```

```python
import functools
import math

import jax
import jax.numpy as jnp
import numpy as np
from jax import lax
from jax.experimental import pallas as pl
from jax.experimental.pallas import tpu as pltpu

F32 = jnp.float32
BF16 = jnp.bfloat16

D_MODEL = 2048
BATCH = 4
SEQ = 2048
DEPTH = 4
CTX_LEN = 256
GRID_W = 64
EPS = 1e-6

HY_W = 512
MLA_HEADS = 8
MLA_DN = 128
MLA_DR = 64
MLA_DV = 128
MLA_W = MLA_HEADS * MLA_DV
Q_RANK = 512
KV_RANK = 256
POOL_W = 512
POOL_WINDOWS = (2, 4, 8, 16)
POOL_GROUP = 128

FILTER_EMB = 33
FILTER_BANDS = 16
FILTER_HIDDEN = 64
DECAY_TARGET = 1e-2
FAST_DECAY = 0.3
SLOW_DECAY = 1.5
ROPE_BASE = 10000.0
ATTN_SCALE = (MLA_DN + MLA_DR) ** -0.5

R_OFF_KR = 2816
R_OFF_MLA_G = 2880

U_HY = 0
U_Q = 2048
U_KV = 2560
U_KR1 = 2816
U_KR2 = 2944
U_MLA_G = 3072
U_POOL = 4096
U_POOL_G = 4608
U_W = 5120

ROWS_L = BATCH * SEQ
ROWS_C = BATCH * CTX_LEN
ROWS_ALL = ROWS_L + ROWS_C

VMEM_LIMIT_BYTES = 56 * 1024 * 1024
LANE = 128


def _cparams(sem):
    return pltpu.CompilerParams(dimension_semantics=sem, vmem_limit_bytes=VMEM_LIMIT_BYTES)


def _silu(x):
    return x * jax.nn.sigmoid(x)


def _dot(a, b):
    return jnp.dot(a, b, preferred_element_type=F32)


def _dot_bf16x3(a, b):
    a_hi, b_hi = a.astype(BF16), b.astype(BF16)
    a_lo = (a - a_hi.astype(F32)).astype(BF16)
    b_lo = (b - b_hi.astype(F32)).astype(BF16)
    return _dot(a_hi, b_hi) + (_dot(a_hi, b_lo) + _dot(a_lo, b_hi))


def _dot_nt(a, b):
    return lax.dot_general(a, b, (((1,), (1,)), ((), ())), preferred_element_type=F32)


def _adaln_kernel(c_ref, w_ref, b_ref, o_ref):
    a = _silu(c_ref[...]).astype(BF16)
    o_ref[...] = _dot(a, w_ref[...].astype(BF16)) + b_ref[...]


def _adaln(cond, w_ada, b_ada, tn=1024):
    depth, d, n = w_ada.shape
    return pl.pallas_call(
        _adaln_kernel,
        grid=(depth, n // tn),
        in_specs=[
            pl.BlockSpec((8, d), lambda l, j: (0, 0)),
            pl.BlockSpec((None, d, tn), lambda l, j: (l, 0, j)),
            pl.BlockSpec((None, 1, tn), lambda l, j: (l, 0, j)),
        ],
        out_specs=pl.BlockSpec((None, 8, tn), lambda l, j: (l, 0, j)),
        out_shape=jax.ShapeDtypeStruct((depth, 8, n), F32),
        compiler_params=_cparams(("parallel", "parallel")),
        name="adaln",
    )(cond, w_ada, b_ada)


def _inproj_kernel(*refs, n_src, n_lat_tiles, sub):
    x_refs = refs[:n_src]
    g_ref, sh_ref, sc_ref, w_ref, o_ref, h_ref = refs[n_src:]
    first = pl.program_id(1) == 0

    def norm_and_project(x_ref):
        gs = g_ref[...] * (1.0 + sc_ref[...])
        sh = sh_ref[...]
        for r in range(0, x_ref.shape[0], sub):
            rows = pl.ds(r, sub)
            x = x_ref[rows, :]
            ms = jnp.mean(x * x, axis=-1, keepdims=True)
            h = (x * lax.rsqrt(ms + EPS) * gs + sh).astype(BF16)
            h_ref[rows, :] = h
            o_ref[rows, :] = _dot_nt(h, w_ref[...]).astype(o_ref.dtype)

    if n_src == 1:
        pl.when(first)(lambda: norm_and_project(x_refs[0]))
    else:
        is_latent = pl.program_id(0) < n_lat_tiles
        pl.when(jnp.logical_and(first, is_latent))(lambda: norm_and_project(x_refs[0]))
        pl.when(jnp.logical_and(first, jnp.logical_not(is_latent)))(lambda: norm_and_project(x_refs[1]))

    @pl.when(jnp.logical_not(first))
    def _():
        o_ref[...] = _dot_nt(h_ref[...], w_ref[...]).astype(o_ref.dtype)


def _layer_spec(arr, l):
    zeros = (0,) * (arr.ndim - 1)
    return pl.BlockSpec((None,) + arr.shape[1:], lambda *_: (l,) + zeros)


def _row_source_specs(xs, tm):
    d = xs[0].shape[1]
    n_lat_tiles = ROWS_L // tm
    if len(xs) == 1:
        return [pl.BlockSpec((tm, d), lambda i, *_: (i, 0))]
    return [pl.BlockSpec((tm, d), lambda i, *_: (jnp.minimum(i, n_lat_tiles - 1), 0)),
            pl.BlockSpec((tm, d), lambda i, *_: (jnp.maximum(i - n_lat_tiles, 0), 0))]


def _inproj(xs, g, mod, w_t, l, *, tm=1024, tn=1280):
    d = xs[0].shape[1]
    n = w_t.shape[1]
    tiles_per_batch = SEQ // tm
    n_row_tiles = ROWS_ALL // tm

    def mod_row(i):
        return jnp.minimum(i // tiles_per_batch, BATCH)

    return pl.pallas_call(
        functools.partial(_inproj_kernel, n_src=len(xs), n_lat_tiles=ROWS_L // tm, sub=256),
        grid=(n_row_tiles, n // tn),
        in_specs=_row_source_specs(xs, tm) + [
            _layer_spec(g, l),
            pl.BlockSpec((None, None, 1, d), lambda i, j: (l, mod_row(i), 0, 0)),
            pl.BlockSpec((None, None, 1, d), lambda i, j: (l, mod_row(i), 0, 1)),
            pl.BlockSpec((None, tn, d), lambda i, j: (l, j, 0)),
        ],
        out_specs=pl.BlockSpec((tm, tn), lambda i, j: (i, j)),
        out_shape=jax.ShapeDtypeStruct((ROWS_ALL, n), BF16),
        scratch_shapes=[pltpu.VMEM((tm, d), BF16)],
        compiler_params=_cparams(("parallel", "arbitrary")),
        name="inproj",
    )(*xs, g, mod, mod, w_t)


def _hyena_block(L):
    return min(L, 512)


def _filter_kernel(feats_ref, t_ref, dl_ref, w1_ref, b1_ref, fr_ref, w2_ref, b2_ref, w3_ref,
                   c_ref, s_ref, g_ref, gn_ref, h_ref, *, L, P):
    hp = lax.Precision.HIGHEST
    hdot = lambda a, b: jnp.dot(a, b, precision=hp, preferred_element_type=F32)
    nb = L // P
    m_fft = 2 * P

    @pl.when(pl.program_id(0) == 0)
    def _():
        fr = fr_ref[...]
        h1 = jnp.sin(fr * (hdot(feats_ref[...], w1_ref[...]) + b1_ref[...]))
        h_ref[...] = jnp.sin(fr * (hdot(h1, w2_ref[...]) + b2_ref[...]))

    h = h_ref[...]
    decay = jnp.exp(-t_ref[...] * dl_ref[...])
    row = lax.broadcasted_iota(jnp.int32, (L, 1), 0)
    rp = lax.broadcasted_iota(jnp.int32, (P, 1), 0)
    sgn = jnp.where(rp % 2 == 0, 1.0, -1.0).astype(F32)
    wgt = jnp.where(rp == 0, 1.0 / m_fft, 2.0 / m_fft).astype(F32)
    cm = c_ref[...]
    sm = s_ref[...]
    hf = _dot_bf16x3(h, w3_ref[0]) * decay
    hb = _dot_bf16x3(h, w3_ref[1]) * decay
    hb = jnp.where(row == 0, 0.0, hb)
    nrm = lax.rsqrt(jnp.sum(hf * hf + hb * hb, axis=0, keepdims=True) + EPS)
    F, B = [], []
    for arr, out in ((hf * nrm, F), (hb * nrm, B)):
        for j in range(nb):
            blk = arr[j * P:(j + 1) * P].astype(BF16)
            b32 = blk.astype(F32)
            out.append((_dot(cm, blk), _dot(sm, blk), jnp.sum(b32 * sgn, axis=0, keepdims=True), b32[0:1]))
    for d in range(-(nb - 1), nb):
        if d >= 1:
            gr = F[d][0] + sgn * (F[d - 1][0] - F[d - 1][3])
            gi = F[d][1] + sgn * F[d - 1][1]
            gn = F[d][2] + F[d - 1][2] - F[d - 1][3]
        elif d == 0:
            gr = F[0][0] + B[0][0]
            gi = F[0][1] - B[0][1]
            gn = F[0][2] + B[0][2]
        else:
            e = -d
            gr = B[e][0] + sgn * (B[e - 1][0] - B[e - 1][3])
            gi = -B[e][1] - sgn * B[e - 1][1]
            gn = B[e][2] + B[e - 1][2] - B[e - 1][3]
        g_ref[d + nb - 1, 0] = gr * wgt
        g_ref[d + nb - 1, 1] = gi * wgt
        gn_ref[d + nb - 1] = gn * (1.0 / m_fft)


def _hyena_filter(L, w1, b1, fr, w2, b2, w3, l, *, cb=256):
    P = _hyena_block(L)
    nd = 2 * (L // P) - 1
    feats, t = _filter_tables(L)
    deltas = _decay_rates()
    cmat, smat = _dft_tables(P, BF16)
    full2 = lambda s: (0, 0)
    return pl.pallas_call(
        functools.partial(_filter_kernel, L=L, P=P),
        grid=(2 * (HY_W // cb),),
        in_specs=[
            pl.BlockSpec(feats.shape, full2),
            pl.BlockSpec(t.shape, full2),
            pl.BlockSpec((1, cb), lambda s: (0, s // 2)),
            _layer_spec(w1, l), _layer_spec(b1, l), _layer_spec(fr, l), _layer_spec(w2, l), _layer_spec(b2, l),
            pl.BlockSpec((None, None, 2, FILTER_HIDDEN, cb), lambda s: (l, s % 2, 0, 0, s // 2)),
            pl.BlockSpec((P, P), full2),
            pl.BlockSpec((P, P), full2),
        ],
        out_specs=[
            pl.BlockSpec((None, nd, 2, P, cb), lambda s: (s % 2, 0, 0, 0, s // 2)),
            pl.BlockSpec((None, nd, 1, cb), lambda s: (s % 2, 0, 0, s // 2)),
        ],
        out_shape=[
            jax.ShapeDtypeStruct((2, nd, 2, P, HY_W), F32),
            jax.ShapeDtypeStruct((2, nd, 1, HY_W), F32),
        ],
        scratch_shapes=[pltpu.VMEM((L, FILTER_HIDDEN), F32)],
        compiler_params=_cparams(("arbitrary",)),
        name=f"hyena_filter_{L}",
    )(feats, t, deltas, w1, b1, fr, w2, b2, w3, cmat, smat)


_MAC_ROWS = 32


def _hyena_kernel(v_ref, x1_ref, x2_ref, gate_ref, cw_ref, cb_ref, hb_ref, g_ref, gn_ref,
                  c_ref, s_ref, o_ref, z_ref, zb_ref, xc_ref, zf_ref, yr_ref, yi_ref, *, L, P):
    nb = L // P

    def short_conv(dst_ref, src_ref, p):
        x = src_ref[...].astype(F32)
        w = cw_ref[:, p, :]
        bias = cb_ref[p:p + 1, :]
        prev = pltpu.roll(x, 1, axis=0)
        nxt = pltpu.roll(x, L - 1, axis=0)
        dst_ref[...] = prev * w[0:1] + x * w[1:2] + nxt * w[2:3] + bias
        dst_ref[0:1, :] = x[0:1] * w[1:2] + x[1:2] * w[2:3] + bias
        dst_ref[L - 1:L, :] = x[L - 2:L - 1] * w[0:1] + x[L - 1:L] * w[1:2] + bias

    sign = jnp.where(lax.broadcasted_iota(jnp.int32, (P, 1), 0) % 2 == 0, 1.0, -1.0).astype(F32)
    cm = c_ref[...]
    sm = s_ref[...]

    short_conv(z_ref, v_ref, 0)
    for o in range(2):
        short_conv(xc_ref, x1_ref if o == 0 else x2_ref, 1 + o)
        zb_ref[...] = z_ref[...].astype(BF16)
        zn = []
        for j in range(nb):
            rows = pl.ds(j * P, P)
            zf_ref[0, j] = _dot(cm, zb_ref[rows, :])
            zf_ref[1, j] = _dot(sm, zb_ref[rows, :])
            zn.append(jnp.sum(z_ref[rows, :] * sign, axis=0, keepdims=True))
        bias = hb_ref[o:o + 1, :]
        def spectrum_product(i):
            for r in range(0, P, _MAC_ROWS):
                rr = pl.ds(r, _MAC_ROWS)
                yr = yi = None
                for j in range(nb):
                    d = i - j + nb - 1
                    gr, gi = g_ref[o, d, 0, rr, :], g_ref[o, d, 1, rr, :]
                    zr, zi = zf_ref[0, j, rr, :], zf_ref[1, j, rr, :]
                    pr, pi = gr * zr - gi * zi, gr * zi + gi * zr
                    yr, yi = (pr, pi) if yr is None else (yr + pr, yi + pi)
                yr_ref[i, rr, :] = yr.astype(BF16)
                yi_ref[i, rr, :] = yi.astype(BF16)

        spectrum_product(0)
        for i in range(nb):
            if i + 1 < nb:
                spectrum_product(i + 1)
            yn = gn_ref[o, i + nb - 1] * zn[0]
            for j in range(1, nb):
                yn = yn + gn_ref[o, i - j + nb - 1] * zn[j]
            rows = pl.ds(i * P, P)
            y = _dot(cm, yr_ref[i]) + _dot(sm, yi_ref[i])
            y = y + sign * yn + z_ref[rows, :] * bias
            y = xc_ref[rows, :] * y
            if o == 0:
                z_ref[rows, :] = y
            else:
                o_ref[rows, :] = (y * _silu(gate_ref[rows, :].astype(F32))).astype(o_ref.dtype)


def _hyena(u_all, conv_w, conv_b, hy_bias, g, gn, l, *, L, row_blk0, cb=256):
    P = _hyena_block(L)
    nb = L // P
    nd = 2 * nb - 1
    ncb = HY_W // cb
    cmat, smat = _dft_tables(P, BF16)

    def ublk(part):
        return pl.BlockSpec((L, cb), lambda j, b: (row_blk0 + b, part * ncb + j))

    in_specs = [
        ublk(0), ublk(1), ublk(2), ublk(3),
        pl.BlockSpec((None, 3, 3, cb), lambda j, b: (l, 0, 0, j)),
        pl.BlockSpec((None, 3, cb), lambda j, b: (l, 0, j)),
        pl.BlockSpec((None, 2, cb), lambda j, b: (l, 0, j)),
        pl.BlockSpec((2, nd, 2, P, cb), lambda j, b: (0, 0, 0, 0, j)),
        pl.BlockSpec((2, nd, 1, cb), lambda j, b: (0, 0, 0, j)),
        pl.BlockSpec((P, P), lambda j, b: (0, 0)),
        pl.BlockSpec((P, P), lambda j, b: (0, 0)),
    ]
    args = [u_all, u_all, u_all, u_all, conv_w, conv_b, hy_bias, g, gn, cmat, smat]
    return pl.pallas_call(
        functools.partial(_hyena_kernel, L=L, P=P),
        grid=(ncb, BATCH),
        in_specs=in_specs,
        out_specs=pl.BlockSpec((L, cb), lambda j, b: (b, j)),
        out_shape=jax.ShapeDtypeStruct((BATCH * L, HY_W), BF16),
        scratch_shapes=[
            pltpu.VMEM((L, cb), F32),
            pltpu.VMEM((L, cb), BF16),
            pltpu.VMEM((L, cb), F32),
            pltpu.VMEM((2, nb, P, cb), F32),
            pltpu.VMEM((nb, P, cb), BF16),
            pltpu.VMEM((nb, P, cb), BF16),
        ],
        compiler_params=_cparams(("parallel", "parallel")),
        name=f"hyena_{L}",
    )(*args)


_Q_SCALE = ATTN_SCALE * math.log2(math.e)
_VT_ROWS = MLA_DV + 16


def _qkv_kernel(uq_ref, ukv_ref, k1_ref, k2_ref, qg_ref, kg_ref, wq_ref, wk_ref, wvt_ref, cos_ref, sin_ref,
                q_ref, k_ref, vt_ref):
    def rms(x, g):
        x = x.astype(F32)
        return (x * lax.rsqrt(jnp.mean(x * x, axis=-1, keepdims=True) + EPS) * g).astype(BF16)

    cos = cos_ref[...]
    sin = sin_ref[...]
    lane = lax.broadcasted_iota(jnp.int32, (1, LANE), 1)
    half_mask = [(lane < MLA_DR), (lane >= MLA_DR)]

    qa = _dot(rms(uq_ref[...], qg_ref[...]), wq_ref[...])
    kvn = rms(ukv_ref[...], kg_ref[...])
    kn = _dot(kvn, wk_ref[...])
    vt = _dot_nt(wvt_ref[...], kvn)
    k_rope = (k1_ref[...].astype(F32) * cos + k2_ref[...].astype(F32) * sin).astype(BF16)
    n_rot = MLA_HEADS * MLA_DR
    for h in range(MLA_HEADS):
        c = h // 2
        qr = qa[:, MLA_W + c * LANE:MLA_W + (c + 1) * LANE]
        qs = qa[:, MLA_W + n_rot + c * LANE:MLA_W + n_rot + (c + 1) * LANE]
        rot = jnp.where(half_mask[h % 2], qr * cos + qs * sin, 0.0)
        q_ref[h, :, 0:LANE] = (qa[:, h * LANE:(h + 1) * LANE] * _Q_SCALE).astype(BF16)
        q_ref[h, :, LANE:2 * LANE] = (rot * _Q_SCALE).astype(BF16)
        k_ref[h, :, 0:LANE] = kn[:, h * LANE:(h + 1) * LANE].astype(BF16)
        k_ref[h, :, LANE:2 * LANE] = k_rope
        vt_ref[h, 0:MLA_DV, :] = vt[h * MLA_DV:(h + 1) * MLA_DV, :].astype(BF16)
        vt_ref[h, MLA_DV:_VT_ROWS, :] = jnp.ones((_VT_ROWS - MLA_DV, vt.shape[1]), BF16)


def _qkv(u_all, qg, kg, wq, wk, wvt, cos_t, sin_t, l, *, tr=512):
    n_lat = ROWS_L // tr
    per_seq = SEQ // tr

    def tab(i):
        return (jnp.where(i < n_lat, i % per_seq, per_seq), 0)

    return pl.pallas_call(
        _qkv_kernel,
        grid=(ROWS_ALL // tr,),
        in_specs=[
            pl.BlockSpec((tr, Q_RANK), lambda i: (i, U_Q // Q_RANK)),
            pl.BlockSpec((tr, KV_RANK), lambda i: (i, U_KV // KV_RANK)),
            pl.BlockSpec((tr, LANE), lambda i: (i, U_KR1 // LANE)),
            pl.BlockSpec((tr, LANE), lambda i: (i, U_KR2 // LANE)),
            _layer_spec(qg, l), _layer_spec(kg, l), _layer_spec(wq, l), _layer_spec(wk, l), _layer_spec(wvt, l),
            pl.BlockSpec((tr, LANE), tab),
            pl.BlockSpec((tr, LANE), tab),
        ],
        out_specs=[
            pl.BlockSpec((MLA_HEADS, tr, 2 * LANE), lambda i: (0, i, 0)),
            pl.BlockSpec((MLA_HEADS, tr, 2 * LANE), lambda i: (0, i, 0)),
            pl.BlockSpec((MLA_HEADS, _VT_ROWS, tr), lambda i: (0, 0, i)),
        ],
        out_shape=[
            jax.ShapeDtypeStruct((MLA_HEADS, ROWS_ALL, 2 * LANE), BF16),
            jax.ShapeDtypeStruct((MLA_HEADS, ROWS_ALL, 2 * LANE), BF16),
            jax.ShapeDtypeStruct((MLA_HEADS, _VT_ROWS, ROWS_ALL), BF16),
        ],
        compiler_params=_cparams(("parallel",)),
        name="qkv",
    )(u_all, u_all, u_all, u_all, qg, kg, wq, wk, wvt, cos_t, sin_t)


_KEY_CHUNK = 1024
_SCORE_LEAD = 1


def _fold_max(x):
    rows = x.shape[0]
    while rows > 8 and rows % 16 == 0:
        rows //= 2
        x = jnp.maximum(x[:rows], x[rows:])
    return x.max(axis=0, keepdims=True)


def _attn_kernel(*refs, n_src):
    q_ref = refs[0]
    k_refs = refs[1:1 + n_src]
    vt_refs = refs[1 + n_src:1 + 2 * n_src]
    g_ref = refs[1 + 2 * n_src]
    o_ref = refs[2 + 2 * n_src]
    chunks = [(k, vt, r0, min(_KEY_CHUNK, k.shape[1] - r0))
              for k, vt in zip(k_refs, vt_refs) for r0 in range(0, k.shape[1], _KEY_CHUNK)]

    def scores(h):
        return [_dot_nt(k[h, r0:r0 + n, :], q_ref[h]) for k, _, r0, n in chunks]

    pending = [scores(h) for h in range(_SCORE_LEAD)]
    for h in range(MLA_HEADS):
        if h + _SCORE_LEAD < MLA_HEADS:
            pending.append(scores(h + _SCORE_LEAD))
        s = pending.pop(0)
        m = _fold_max(s[0])
        for si in s[1:]:
            m = jnp.maximum(m, _fold_max(si))
        acc = None
        for si, (_, vt, r0, n) in zip(s, chunks):
            pv = _dot(vt[h, :, r0:r0 + n], jnp.exp2((si - m).astype(BF16)))
            acc = pv if acc is None else acc + pv
        out = (acc[0:MLA_DV] / acc[MLA_DV:MLA_DV + 1]).T
        gate = _silu(g_ref[:, h * LANE:(h + 1) * LANE].astype(F32))
        o_ref[:, h * LANE:(h + 1) * LANE] = (out * gate).astype(o_ref.dtype)


def _attention(q, k, vt, u_all, *, latent, tq=256):
    H = MLA_HEADS
    ctx_blk0 = ROWS_L // CTX_LEN
    if latent:
        n_q = SEQ // tq
        q_off = 0
        k_specs = [pl.BlockSpec((H, SEQ, 2 * LANE), lambda b, i: (0, b, 0)),
                   pl.BlockSpec((H, CTX_LEN, 2 * LANE), lambda b, i: (0, ctx_blk0 + b, 0))]
        v_specs = [pl.BlockSpec((H, _VT_ROWS, SEQ), lambda b, i: (0, 0, b)),
                   pl.BlockSpec((H, _VT_ROWS, CTX_LEN), lambda b, i: (0, 0, ctx_blk0 + b))]
    else:
        n_q = CTX_LEN // tq
        q_off = ROWS_L // tq
        k_specs = [pl.BlockSpec((H, CTX_LEN, 2 * LANE), lambda b, i: (0, ctx_blk0 + b, 0))]
        v_specs = [pl.BlockSpec((H, _VT_ROWS, CTX_LEN), lambda b, i: (0, 0, ctx_blk0 + b))]
    n_src = len(k_specs)
    in_specs = ([pl.BlockSpec((H, tq, 2 * LANE), lambda b, i: (0, q_off + b * n_q + i, 0))] + k_specs + v_specs
                + [pl.BlockSpec((tq, MLA_W), lambda b, i: (q_off + b * n_q + i, U_MLA_G // MLA_W))])
    args = [q] + [k] * n_src + [vt] * n_src + [u_all]
    return pl.pallas_call(
        functools.partial(_attn_kernel, n_src=n_src),
        grid=(BATCH, n_q),
        in_specs=in_specs,
        out_specs=pl.BlockSpec((tq, MLA_W), lambda b, i: (b * n_q + i, 0)),
        out_shape=jax.ShapeDtypeStruct((BATCH * n_q * tq, MLA_W), BF16),
        compiler_params=_cparams(("parallel", "parallel")),
        name="attn_latent" if latent else "attn_ctx",
    )(*args)


_POOL_PAD = 16


def _pool_kernel(x_ref, g_ref, w_ref, sc_ref, o_ref, pad_ref, *, L):
    zeros = jnp.zeros((_POOL_PAD, POOL_GROUP), F32)
    pad_ref[pl.ds(0, _POOL_PAD), :] = zeros
    pad_ref[pl.ds(L + _POOL_PAD, _POOL_PAD), :] = zeros
    t = lax.broadcasted_iota(jnp.int32, (L, 1), 0)
    for gi, win in enumerate(POOL_WINDOWS):
        half = win // 2
        cols = slice(gi * POOL_GROUP, (gi + 1) * POOL_GROUP)
        x = x_ref[:, cols].astype(F32)
        pad_ref[pl.ds(_POOL_PAD, L), :] = x
        acc = pad_ref[pl.ds(_POOL_PAD - half, L), :]
        for j in range(-half + 1, half):
            acc = acc + pad_ref[pl.ds(_POOL_PAD + j, L), :]
        cnt = (jnp.minimum(t + half, L) - jnp.maximum(t - half, 0)).astype(F32)
        dlt = (acc / cnt - x).astype(BF16)
        y = _dot(dlt, w_ref[gi]) * sc_ref[:, cols]
        o_ref[:, cols] = (y * _silu(g_ref[:, cols].astype(F32))).astype(o_ref.dtype)


def _pool(u_all, w_pool, pool_scale, l, *, L, row_blk0):
    return pl.pallas_call(
        functools.partial(_pool_kernel, L=L),
        grid=(BATCH,),
        in_specs=[
            pl.BlockSpec((L, POOL_W), lambda b: (row_blk0 + b, U_POOL // POOL_W)),
            pl.BlockSpec((L, POOL_W), lambda b: (row_blk0 + b, U_POOL_G // POOL_W)),
            _layer_spec(w_pool, l),
            _layer_spec(pool_scale, l),
        ],
        out_specs=pl.BlockSpec((L, POOL_W), lambda b: (b, 0)),
        out_shape=jax.ShapeDtypeStruct((BATCH * L, POOL_W), BF16),
        scratch_shapes=[pltpu.VMEM((L + 2 * _POOL_PAD, POOL_GROUP), F32)],
        compiler_params=_cparams(("parallel",)),
        name=f"pool_{L}",
    )(u_all, u_all, w_pool, pool_scale)


def _outproj_kernel(*refs, n_src, n_lat_tiles, n_streams, final, sub):
    x_refs = refs[:n_src]
    refs = refs[n_src:]
    mixers = [refs[3 * s:3 * s + 3] for s in range(n_streams)]
    w_ref, gt_ref, fg_ref, o_ref = refs[3 * n_streams:]

    def run(x_ref, hy_ref, at_ref, po_ref):
        for r in range(0, x_ref.shape[0], sub):
            rows = pl.ds(r, sub)
            acc = (_dot(hy_ref[rows, :], w_ref[0:HY_W, :]) + _dot(at_ref[rows, :], w_ref[HY_W:HY_W + MLA_W, :])
                   + _dot(po_ref[rows, :], w_ref[HY_W + MLA_W:, :]))
            y = x_ref[rows, :] + gt_ref[...] * acc
            if final:
                y = y * lax.rsqrt(jnp.mean(y * y, axis=-1, keepdims=True) + EPS) * fg_ref[...]
            o_ref[rows, :] = y

    if n_streams == 1:
        run(x_refs[0], *mixers[0])
    else:
        is_latent = pl.program_id(0) < n_lat_tiles
        pl.when(is_latent)(lambda: run(x_refs[0], *mixers[0]))
        pl.when(jnp.logical_not(is_latent))(lambda: run(x_refs[-1], *mixers[1]))


def _outproj(xs, mixers, w_out, mod, fg, l, *, final, tm=512):
    d = xs[0].shape[1]
    tiles_per_batch = SEQ // tm
    n_lat_tiles = ROWS_L // tm
    n_row_tiles = sum(m[0].shape[0] for m in mixers) // tm

    def mod_row(i):
        return jnp.minimum(i // tiles_per_batch, BATCH)

    lat_blk = lambda i: (jnp.minimum(i, n_lat_tiles - 1), 0)
    ctx_blk = lambda i: (jnp.maximum(i - n_lat_tiles, 0), 0)
    mixer_specs, mixer_args = [], []
    for blk, (hy, att, po) in zip((lat_blk, ctx_blk), mixers):
        mixer_specs += [pl.BlockSpec((tm, HY_W), blk), pl.BlockSpec((tm, MLA_W), blk), pl.BlockSpec((tm, POOL_W), blk)]
        mixer_args += [hy, att, po]

    return pl.pallas_call(
        functools.partial(_outproj_kernel, n_src=len(xs), n_lat_tiles=n_lat_tiles, n_streams=len(mixers),
                          final=final, sub=256),
        grid=(n_row_tiles,),
        in_specs=_row_source_specs(xs, tm) + mixer_specs + [
            pl.BlockSpec((None,) + w_out.shape[1:], lambda i: (l, 0, 0), pipeline_mode=pl.Buffered(1)),
            pl.BlockSpec((None, None, 1, d), lambda i: (l, mod_row(i), 0, 2)),
            pl.BlockSpec((1, d), lambda i: (0, 0)),
        ],
        out_specs=pl.BlockSpec((tm, d), lambda i: (i, 0)),
        out_shape=jax.ShapeDtypeStruct((n_row_tiles * tm, d), F32),
        compiler_params=_cparams(("parallel",)),
        name="outproj_final" if final else "outproj",
    )(*xs, *mixer_args, w_out, mod, fg)


def _rope_lane_tables(ident_rows):
    n_rows = SEQ // GRID_W
    row = np.repeat(np.arange(n_rows, dtype=np.float64), GRID_W)
    col = np.tile(np.arange(GRID_W, dtype=np.float64), n_rows)
    n_freq = MLA_DR // 4
    inv = ROPE_BASE ** (-np.arange(n_freq, dtype=np.float64) / n_freq)
    ang = np.concatenate([row[:, None] * inv, col[:, None] * inv], axis=-1)
    cos, sin = np.cos(ang), np.sin(ang)
    cos_t = np.concatenate([cos, cos, cos, cos], axis=-1)
    sin_t = np.concatenate([-sin, sin, -sin, sin], axis=-1)
    cos_t = np.concatenate([cos_t, np.ones((ident_rows, LANE))], axis=0)
    sin_t = np.concatenate([sin_t, np.zeros((ident_rows, LANE))], axis=0)
    return jnp.asarray(cos_t, F32), jnp.asarray(sin_t, F32)


def _dft_tables(P, dtype):
    idx = np.arange(P, dtype=np.int64)
    ang = ((idx[:, None] * idx[None, :]) % (2 * P)).astype(np.float64) * (math.pi / P)
    return jnp.asarray(np.cos(ang), F32).astype(dtype), jnp.asarray(np.sin(ang), F32).astype(dtype)


def _filter_tables(L):
    t = np.linspace(0.0, 1.0, L)[:, None]
    wpos = (2.0 * math.pi / L) * np.arange(L, dtype=np.float64)[:, None]
    bands = np.linspace(1e-4, FILTER_BANDS - 1, FILTER_BANDS)[None, :]
    feats = np.concatenate([t, np.cos(bands * wpos), -np.sin(bands * wpos)], axis=-1)
    feats = np.pad(feats, ((0, 0), (0, LANE - FILTER_EMB)))
    return jnp.asarray(feats, F32), jnp.asarray(t, F32)


def _decay_rates():
    d = np.abs(np.linspace(math.log(DECAY_TARGET) / SLOW_DECAY, math.log(DECAY_TARGET) / FAST_DECAY, HY_W))
    return jnp.asarray(d[None, :], F32)


def _kr_permutation():
    p = np.zeros((2 * LANE, MLA_DR), np.float32)
    half = MLA_DR // 2
    for grp, odd in enumerate((0, 1, 0, 1, 1, 0, 1, 0)):
        for i in range(half):
            p[grp * half + i, 2 * i + odd] = 1.0
    return jnp.asarray(p, BF16)


_PACK_BLK = U_W // 4
_PACK_KR_STEP = U_KR1 // _PACK_BLK
_PACK_KR_ROW = U_KR1 - _PACK_KR_STEP * _PACK_BLK
_PACK_SHIFT = U_MLA_G - R_OFF_MLA_G


def _pack_w_in_kernel(a3_ref, p_ref, o_ref):
    i = pl.program_id(1)
    a_ref = a3_ref.at[0]

    @pl.when(i != _PACK_KR_STEP)
    def _():
        o_ref[...] = a_ref[...].astype(BF16)

    @pl.when(i == _PACK_KR_STEP)
    def _():
        r0, r1 = _PACK_KR_ROW, _PACK_KR_ROW + 2 * LANE
        o_ref[0:r0, :] = a_ref[0:r0, :].astype(BF16)
        o_ref[r0:r1, :] = _dot(p_ref[...], a_ref[r0:r0 + MLA_DR, :].astype(BF16)).astype(BF16)
        o_ref[r1:, :] = a_ref[r1 - _PACK_SHIFT:_PACK_BLK - _PACK_SHIFT, :].astype(BF16)


def _pack_w_in(w_in_t):
    depth, n, d = w_in_t.shape
    assert U_KR1 == R_OFF_KR and U_MLA_G == U_KR1 + 2 * LANE and n == U_W - _PACK_SHIFT
    perm = _kr_permutation()

    def src_row(l, i):
        row = jnp.where(i <= _PACK_KR_STEP, i * _PACK_BLK, i * _PACK_BLK - _PACK_SHIFT)
        return (l, pl.multiple_of(row, MLA_DR), 0)

    return pl.pallas_call(
        _pack_w_in_kernel,
        grid=(depth, U_W // _PACK_BLK),
        in_specs=[pl.BlockSpec((pl.Element(1), pl.Element(_PACK_BLK), pl.Element(d)), src_row),
                  pl.BlockSpec(perm.shape, lambda l, i: (0, 0))],
        out_specs=pl.BlockSpec((None, _PACK_BLK, d), lambda l, i: (l, i, 0)),
        out_shape=jax.ShapeDtypeStruct((depth, U_W, d), BF16),
        compiler_params=_cparams(("parallel", "parallel")),
        name="pack_w_in",
    )(w_in_t, perm)


def _pack_w_uq(w_uq):
    w = w_uq.reshape(DEPTH, Q_RANK, MLA_HEADS, MLA_DN + MLA_DR)
    nope = w[..., :MLA_DN].reshape(DEPTH, Q_RANK, MLA_W)
    a, b = w[..., MLA_DN::2], w[..., MLA_DN + 1::2]
    rot = jnp.concatenate([a, b], axis=-1).reshape(DEPTH, Q_RANK, MLA_HEADS * MLA_DR)
    swp = jnp.concatenate([b, a], axis=-1).reshape(DEPTH, Q_RANK, MLA_HEADS * MLA_DR)
    return jnp.concatenate([nope, rot, swp], axis=-1).astype(BF16)


def _pack_w_ukv(w_ukv):
    w = w_ukv.reshape(DEPTH, KV_RANK, MLA_HEADS, MLA_DN + MLA_DV)
    wk = w[..., :MLA_DN].reshape(DEPTH, KV_RANK, MLA_W)
    wv = w[..., MLA_DN:].reshape(DEPTH, KV_RANK, MLA_W)
    return wk.astype(BF16), jnp.swapaxes(wv, 1, 2).astype(BF16)


def kernel(x, c, ctx, c_ctx, norm_g, w_ada, b_ada, w_in, hy_conv_w, hy_conv_b, hf_w1, hf_b1, hf_freq,
           hf_w2, hf_b2, hf_w3, hy_bias, q_norm_g, w_uq, kv_norm_g, w_ukv, w_pool, pool_scale, w_out,
           final_norm_g):
    assert x.shape == (BATCH, SEQ, D_MODEL) and ctx.shape == (BATCH, CTX_LEN, D_MODEL)

    qkv_tr = 512
    cos_t, sin_t = _rope_lane_tables(qkv_tr)

    w_in_p = _pack_w_in(jnp.swapaxes(w_in, 1, 2))
    w_q_p = _pack_w_uq(w_uq)
    w_k_p, w_vt_p = _pack_w_ukv(w_ukv)
    w_out_b = w_out.astype(BF16)
    w_pool_b = w_pool.astype(BF16)
    w1_p = jnp.pad(hf_w1, ((0, 0), (0, LANE - FILTER_EMB), (0, 0)))
    w3_p = hf_w3.reshape(DEPTH, FILTER_HIDDEN, 2, 2, HY_W).transpose(0, 2, 3, 1, 4)
    conv_w = hy_conv_w.reshape(DEPTH, 3, 3, HY_W)
    conv_b = hy_conv_b.reshape(DEPTH, 3, HY_W)
    rows = lambda a: a[:, None, :]
    norm_g3, qg3, kg3, ps3 = rows(norm_g), rows(q_norm_g), rows(kv_norm_g), rows(pool_scale)
    b1_3, fr_3, b2_3 = rows(hf_b1), rows(hf_freq), rows(hf_b2)

    cond = jnp.concatenate([c, c_ctx[None], jnp.zeros((8 - BATCH - 1, D_MODEL), F32)], axis=0)
    mod = _adaln(cond, w_ada, b_ada[:, None, :])
    mod = mod.reshape(DEPTH, 8, 1, 3 * D_MODEL)

    xs = [x.reshape(ROWS_L, D_MODEL), ctx.reshape(ROWS_C, D_MODEL)]

    for l in range(DEPTH):
        last = l == DEPTH - 1
        u_all = _inproj(xs, norm_g3, mod, w_in_p, l)
        q, k, vt = _qkv(u_all, qg3, kg3, w_q_p, w_k_p, w_vt_p, cos_t, sin_t, l, tr=qkv_tr)
        mixers = []
        for L, row_blk0, latent in ((SEQ, 0, True), (CTX_LEN, ROWS_L // CTX_LEN, False)):
            if last and not latent:
                continue
            g, gn = _hyena_filter(L, w1_p, b1_3, fr_3, hf_w2, b2_3, w3_p, l)
            hy = _hyena(u_all, conv_w, conv_b, hy_bias, g, gn, l, L=L, row_blk0=row_blk0)
            att = _attention(q, k, vt, u_all, latent=latent, tq=256)
            po = _pool(u_all, w_pool_b, ps3, l, L=L, row_blk0=row_blk0)
            mixers.append((hy, att, po))
        xs = [_outproj(xs, mixers, w_out_b, mod, final_norm_g[None, :], l, final=last)]
    return xs[0].reshape(BATCH, SEQ, D_MODEL)
```

```python
import functools
import math

import jax
import jax.numpy as jnp
import numpy as np
from jax import lax
from jax.experimental import pallas as pl
from jax.experimental.pallas import tpu as pltpu

F32 = jnp.float32
BF16 = jnp.bfloat16

D_MODEL = 2048
BATCH = 4
SEQ = 2048
DEPTH = 4
CTX_LEN = 256
GRID_W = 64
EPS = 1e-6

HY_W = 512
MLA_HEADS = 8
MLA_DN = 128
MLA_DR = 64
MLA_DV = 128
MLA_W = MLA_HEADS * MLA_DV
Q_RANK = 512
KV_RANK = 256
POOL_W = 512
POOL_WINDOWS = (2, 4, 8, 16)
POOL_GROUP = 128

FILTER_EMB = 33
FILTER_BANDS = 16
FILTER_HIDDEN = 64
DECAY_TARGET = 1e-2
FAST_DECAY = 0.3
SLOW_DECAY = 1.5
ROPE_BASE = 10000.0
ATTN_SCALE = (MLA_DN + MLA_DR) ** -0.5

R_OFF_KR = 2816
R_OFF_MLA_G = 2880

U_HY = 0
U_Q = 2048
U_KV = 2560
U_KR1 = 2816
U_KR2 = 2944
U_MLA_G = 3072
U_POOL = 4096
U_POOL_G = 4608
U_W = 5120

ROWS_L = BATCH * SEQ
ROWS_C = BATCH * CTX_LEN
ROWS_ALL = ROWS_L + ROWS_C

VMEM_LIMIT_BYTES = 56 * 1024 * 1024
LANE = 128


def _cparams(sem):
    return pltpu.CompilerParams(dimension_semantics=sem, vmem_limit_bytes=VMEM_LIMIT_BYTES)


def _silu(x):
    return x * jax.nn.sigmoid(x)


def _dot(a, b):
    return jnp.dot(a, b, preferred_element_type=F32)


def _dot_bf16x3(a, b):
    a_hi, b_hi = a.astype(BF16), b.astype(BF16)
    a_lo = (a - a_hi.astype(F32)).astype(BF16)
    b_lo = (b - b_hi.astype(F32)).astype(BF16)
    return _dot(a_hi, b_hi) + (_dot(a_hi, b_lo) + _dot(a_lo, b_hi))


def _dot_nt(a, b):
    return lax.dot_general(a, b, (((1,), (1,)), ((), ())), preferred_element_type=F32)


def _adaln_kernel(c_ref, w_ref, b_ref, o_ref):
    a = _silu(c_ref[...]).astype(BF16)
    o_ref[...] = _dot(a, w_ref[...].astype(BF16)) + b_ref[...]


def _adaln(cond, w_ada, b_ada, tn=1024):
    depth, d, n = w_ada.shape
    return pl.pallas_call(
        _adaln_kernel,
        grid=(depth, n // tn),
        in_specs=[
            pl.BlockSpec((8, d), lambda l, j: (0, 0)),
            pl.BlockSpec((None, d, tn), lambda l, j: (l, 0, j)),
            pl.BlockSpec((None, 1, tn), lambda l, j: (l, 0, j)),
        ],
        out_specs=pl.BlockSpec((None, 8, tn), lambda l, j: (l, 0, j)),
        out_shape=jax.ShapeDtypeStruct((depth, 8, n), F32),
        compiler_params=_cparams(("parallel", "parallel")),
        name="adaln",
    )(cond, w_ada, b_ada)


def _inproj_kernel(*refs, n_src, n_lat_tiles, ctx_col_tile, sub):
    x_refs = refs[:n_src]
    g_ref, sh_ref, sc_ref, w_ref, o_ref, h_ref = refs[n_src:]
    first = pl.program_id(1) == 0
    is_latent = pl.program_id(0) < n_lat_tiles
    both = jnp.logical_and

    def norm_and_project(x_ref, project=True):
        gs = g_ref[...] * (1.0 + sc_ref[...])
        sh = sh_ref[...]
        for r in range(0, x_ref.shape[0], sub):
            rows = pl.ds(r, sub)
            x = x_ref[rows, :]
            ms = jnp.mean(x * x, axis=-1, keepdims=True)
            h = (x * lax.rsqrt(ms + EPS) * gs + sh).astype(BF16)
            h_ref[rows, :] = h
            if project:
                o_ref[rows, :] = _dot_nt(h, w_ref[...]).astype(o_ref.dtype)
        if not project:
            o_ref[...] = jnp.zeros_like(o_ref)

    def project_all():
        o_ref[...] = _dot_nt(h_ref[...], w_ref[...]).astype(o_ref.dtype)

    if ctx_col_tile is None:
        if n_src == 1:
            pl.when(first)(lambda: norm_and_project(x_refs[0]))
        else:
            pl.when(both(first, is_latent))(lambda: norm_and_project(x_refs[0]))
            pl.when(both(first, jnp.logical_not(is_latent)))(lambda: norm_and_project(x_refs[1]))
        pl.when(jnp.logical_not(first))(project_all)
    else:
        assert n_src == 1 and ctx_col_tile > 0
        wanted = jnp.logical_or(is_latent, pl.program_id(1) == ctx_col_tile)
        pl.when(both(first, is_latent))(lambda: norm_and_project(x_refs[0]))
        pl.when(both(first, jnp.logical_not(is_latent)))(lambda: norm_and_project(x_refs[0], project=False))
        pl.when(both(jnp.logical_not(first), wanted))(project_all)

        @pl.when(both(jnp.logical_not(first), jnp.logical_not(wanted)))
        def _():
            o_ref[...] = jnp.zeros_like(o_ref)


def _layer_spec(arr, l):
    zeros = (0,) * (arr.ndim - 1)
    return pl.BlockSpec((None,) + arr.shape[1:], lambda *_: (l,) + zeros)


def _row_source_specs(xs, tm):
    d = xs[0].shape[1]
    n_lat_tiles = ROWS_L // tm
    if len(xs) == 1:
        return [pl.BlockSpec((tm, d), lambda i, *_: (i, 0))]
    return [pl.BlockSpec((tm, d), lambda i, *_: (jnp.minimum(i, n_lat_tiles - 1), 0)),
            pl.BlockSpec((tm, d), lambda i, *_: (jnp.maximum(i - n_lat_tiles, 0), 0))]


def _inproj(xs, g, mod, w_t, l, *, ctx_keys_only=False, tm=1024, tn=1280):
    d = xs[0].shape[1]
    n = w_t.shape[1]
    tiles_per_batch = SEQ // tm
    n_row_tiles = ROWS_ALL // tm
    ctx_col_tile = None
    if ctx_keys_only:
        ctx_col_tile = U_KV // tn
        assert ctx_col_tile * tn <= U_KV and U_MLA_G <= (ctx_col_tile + 1) * tn

    def mod_row(i):
        return jnp.minimum(i // tiles_per_batch, BATCH)

    return pl.pallas_call(
        functools.partial(_inproj_kernel, n_src=len(xs), n_lat_tiles=ROWS_L // tm, ctx_col_tile=ctx_col_tile,
                          sub=256),
        grid=(n_row_tiles, n // tn),
        in_specs=_row_source_specs(xs, tm) + [
            _layer_spec(g, l),
            pl.BlockSpec((None, None, 1, d), lambda i, j: (l, mod_row(i), 0, 0)),
            pl.BlockSpec((None, None, 1, d), lambda i, j: (l, mod_row(i), 0, 1)),
            pl.BlockSpec((None, tn, d), lambda i, j: (l, j, 0)),
        ],
        out_specs=pl.BlockSpec((tm, tn), lambda i, j: (i, j)),
        out_shape=jax.ShapeDtypeStruct((ROWS_ALL, n), BF16),
        scratch_shapes=[pltpu.VMEM((tm, d), BF16)],
        compiler_params=_cparams(("parallel", "arbitrary")),
        name="inproj",
    )(*xs, g, mod, mod, w_t)


def _hyena_block(L):
    return min(L, 512)


def _filter_kernel(feats_ref, t_ref, dl_ref, w1_ref, b1_ref, fr_ref, w2_ref, b2_ref, w3_ref,
                   c_ref, s_ref, g_ref, gn_ref, h_ref, *, L, P):
    hp = lax.Precision.HIGHEST
    hdot = lambda a, b: jnp.dot(a, b, precision=hp, preferred_element_type=F32)
    nb = L // P
    m_fft = 2 * P

    @pl.when(pl.program_id(0) == 0)
    def _():
        fr = fr_ref[...]
        h1 = jnp.sin(fr * (hdot(feats_ref[...], w1_ref[...]) + b1_ref[...]))
        h_ref[...] = jnp.sin(fr * (hdot(h1, w2_ref[...]) + b2_ref[...]))

    h = h_ref[...]
    decay = jnp.exp(-t_ref[...] * dl_ref[...])
    row = lax.broadcasted_iota(jnp.int32, (L, 1), 0)
    rp = lax.broadcasted_iota(jnp.int32, (P, 1), 0)
    sgn = jnp.where(rp % 2 == 0, 1.0, -1.0).astype(F32)
    wgt = jnp.where(rp == 0, 1.0 / m_fft, 2.0 / m_fft).astype(F32)
    cm = c_ref[...]
    sm = s_ref[...]
    hf = _dot_bf16x3(h, w3_ref[0]) * decay
    hb = _dot_bf16x3(h, w3_ref[1]) * decay
    hb = jnp.where(row == 0, 0.0, hb)
    nrm = lax.rsqrt(jnp.sum(hf * hf + hb * hb, axis=0, keepdims=True) + EPS)
    F, B = [], []
    for arr, out in ((hf * nrm, F), (hb * nrm, B)):
        for j in range(nb):
            blk = arr[j * P:(j + 1) * P].astype(BF16)
            b32 = blk.astype(F32)
            out.append((_dot(cm, blk), _dot(sm, blk), jnp.sum(b32 * sgn, axis=0, keepdims=True), b32[0:1]))
    for d in range(-(nb - 1), nb):
        if d >= 1:
            gr = F[d][0] + sgn * (F[d - 1][0] - F[d - 1][3])
            gi = F[d][1] + sgn * F[d - 1][1]
            gn = F[d][2] + F[d - 1][2] - F[d - 1][3]
        elif d == 0:
            gr = F[0][0] + B[0][0]
            gi = F[0][1] - B[0][1]
            gn = F[0][2] + B[0][2]
        else:
            e = -d
            gr = B[e][0] + sgn * (B[e - 1][0] - B[e - 1][3])
            gi = -B[e][1] - sgn * B[e - 1][1]
            gn = B[e][2] + B[e - 1][2] - B[e - 1][3]
        g_ref[d + nb - 1, 0] = gr * wgt
        g_ref[d + nb - 1, 1] = gi * wgt
        gn_ref[d + nb - 1] = gn * (1.0 / m_fft)


def _hyena_filter(L, w1, b1, fr, w2, b2, w3, l, *, cb=256):
    P = _hyena_block(L)
    nd = 2 * (L // P) - 1
    feats, t = _filter_tables(L)
    deltas = _decay_rates()
    cmat, smat = _dft_tables(P, BF16)
    full2 = lambda s: (0, 0)
    return pl.pallas_call(
        functools.partial(_filter_kernel, L=L, P=P),
        grid=(2 * (HY_W // cb),),
        in_specs=[
            pl.BlockSpec(feats.shape, full2),
            pl.BlockSpec(t.shape, full2),
            pl.BlockSpec((1, cb), lambda s: (0, s // 2)),
            _layer_spec(w1, l), _layer_spec(b1, l), _layer_spec(fr, l), _layer_spec(w2, l), _layer_spec(b2, l),
            pl.BlockSpec((None, None, 2, FILTER_HIDDEN, cb), lambda s: (l, s % 2, 0, 0, s // 2)),
            pl.BlockSpec((P, P), full2),
            pl.BlockSpec((P, P), full2),
        ],
        out_specs=[
            pl.BlockSpec((None, nd, 2, P, cb), lambda s: (s % 2, 0, 0, 0, s // 2)),
            pl.BlockSpec((None, nd, 1, cb), lambda s: (s % 2, 0, 0, s // 2)),
        ],
        out_shape=[
            jax.ShapeDtypeStruct((2, nd, 2, P, HY_W), F32),
            jax.ShapeDtypeStruct((2, nd, 1, HY_W), F32),
        ],
        scratch_shapes=[pltpu.VMEM((L, FILTER_HIDDEN), F32)],
        compiler_params=_cparams(("arbitrary",)),
        name=f"hyena_filter_{L}",
    )(feats, t, deltas, w1, b1, fr, w2, b2, w3, cmat, smat)


_MAC_ROWS = 32


def _hyena_kernel(v_ref, x1_ref, x2_ref, gate_ref, cw_ref, cb_ref, hb_ref, g_ref, gn_ref,
                  c_ref, s_ref, o_ref, z_ref, zb_ref, xc_ref, zf_ref, yr_ref, yi_ref, *, L, P):
    nb = L // P

    def short_conv(dst_ref, src_ref, p):
        x = src_ref[...].astype(F32)
        w = cw_ref[:, p, :]
        bias = cb_ref[p:p + 1, :]
        prev = pltpu.roll(x, 1, axis=0)
        nxt = pltpu.roll(x, L - 1, axis=0)
        dst_ref[...] = prev * w[0:1] + x * w[1:2] + nxt * w[2:3] + bias
        dst_ref[0:1, :] = x[0:1] * w[1:2] + x[1:2] * w[2:3] + bias
        dst_ref[L - 1:L, :] = x[L - 2:L - 1] * w[0:1] + x[L - 1:L] * w[1:2] + bias

    sign = jnp.where(lax.broadcasted_iota(jnp.int32, (P, 1), 0) % 2 == 0, 1.0, -1.0).astype(F32)
    cm = c_ref[...]
    sm = s_ref[...]

    short_conv(z_ref, v_ref, 0)
    for o in range(2):
        zb_ref[...] = z_ref[...].astype(BF16)
        zn = []
        for j in range(nb):
            rows = pl.ds(j * P, P)
            zf_ref[0, j] = _dot(cm, zb_ref[rows, :])
            zf_ref[1, j] = _dot(sm, zb_ref[rows, :])
            zn.append(jnp.sum(z_ref[rows, :] * sign, axis=0, keepdims=True))
        short_conv(xc_ref, x1_ref if o == 0 else x2_ref, 1 + o)
        bias = hb_ref[o:o + 1, :]
        def spectrum_product(i):
            for r in range(0, P, _MAC_ROWS):
                rr = pl.ds(r, _MAC_ROWS)
                yr = yi = None
                for j in range(nb):
                    d = i - j + nb - 1
                    gr, gi = g_ref[o, d, 0, rr, :], g_ref[o, d, 1, rr, :]
                    zr, zi = zf_ref[0, j, rr, :], zf_ref[1, j, rr, :]
                    pr, pi = gr * zr - gi * zi, gr * zi + gi * zr
                    yr, yi = (pr, pi) if yr is None else (yr + pr, yi + pi)
                yr_ref[i, rr, :] = yr.astype(BF16)
                yi_ref[i, rr, :] = yi.astype(BF16)

        spectrum_product(0)
        for i in range(nb):
            if i + 1 < nb:
                spectrum_product(i + 1)
            yn = gn_ref[o, i + nb - 1] * zn[0]
            for j in range(1, nb):
                yn = yn + gn_ref[o, i - j + nb - 1] * zn[j]
            rows = pl.ds(i * P, P)
            y = _dot(cm, yr_ref[i]) + _dot(sm, yi_ref[i])
            y = y + sign * yn + z_ref[rows, :] * bias
            y = xc_ref[rows, :] * y
            if o == 0:
                z_ref[rows, :] = y
            else:
                o_ref[rows, :] = (y * _silu(gate_ref[rows, :].astype(F32))).astype(o_ref.dtype)


def _hyena(u_all, conv_w, conv_b, hy_bias, g, gn, l, *, L, row_blk0, cb=256):
    P = _hyena_block(L)
    nb = L // P
    nd = 2 * nb - 1
    ncb = HY_W // cb
    cmat, smat = _dft_tables(P, BF16)

    def ublk(part):
        return pl.BlockSpec((L, cb), lambda j, b: (row_blk0 + b, part * ncb + j))

    in_specs = [
        ublk(0), ublk(1), ublk(2), ublk(3),
        pl.BlockSpec((None, 3, 3, cb), lambda j, b: (l, 0, 0, j)),
        pl.BlockSpec((None, 3, cb), lambda j, b: (l, 0, j)),
        pl.BlockSpec((None, 2, cb), lambda j, b: (l, 0, j)),
        pl.BlockSpec((2, nd, 2, P, cb), lambda j, b: (0, 0, 0, 0, j)),
        pl.BlockSpec((2, nd, 1, cb), lambda j, b: (0, 0, 0, j)),
        pl.BlockSpec((P, P), lambda j, b: (0, 0)),
        pl.BlockSpec((P, P), lambda j, b: (0, 0)),
    ]
    args = [u_all, u_all, u_all, u_all, conv_w, conv_b, hy_bias, g, gn, cmat, smat]
    return pl.pallas_call(
        functools.partial(_hyena_kernel, L=L, P=P),
        grid=(ncb, BATCH),
        in_specs=in_specs,
        out_specs=pl.BlockSpec((L, cb), lambda j, b: (b, j)),
        out_shape=jax.ShapeDtypeStruct((BATCH * L, HY_W), BF16),
        scratch_shapes=[
            pltpu.VMEM((L, cb), F32),
            pltpu.VMEM((L, cb), BF16),
            pltpu.VMEM((L, cb), F32),
            pltpu.VMEM((2, nb, P, cb), F32),
            pltpu.VMEM((nb, P, cb), BF16),
            pltpu.VMEM((nb, P, cb), BF16),
        ],
        compiler_params=_cparams(("parallel", "parallel")),
        name=f"hyena_{L}",
    )(*args)


_Q_SCALE = ATTN_SCALE * math.log2(math.e)
_VT_ROWS = MLA_DV + 16


def _qkv_kernel(uq_ref, ukv_ref, k1_ref, k2_ref, qg_ref, kg_ref, wq_ref, wk_ref, wvt_ref, cos_ref, sin_ref,
                q_ref, k_ref, vt_ref):
    def rms(x, g):
        x = x.astype(F32)
        return (x * lax.rsqrt(jnp.mean(x * x, axis=-1, keepdims=True) + EPS) * g).astype(BF16)

    cos = cos_ref[...]
    sin = sin_ref[...]
    lane = lax.broadcasted_iota(jnp.int32, (1, LANE), 1)
    half_mask = [(lane < MLA_DR), (lane >= MLA_DR)]

    qa = _dot(rms(uq_ref[...], qg_ref[...]), wq_ref[...])
    kvn = rms(ukv_ref[...], kg_ref[...])
    kn = _dot(kvn, wk_ref[...])
    vt = _dot_nt(wvt_ref[...], kvn)
    k_rope = (k1_ref[...].astype(F32) * cos + k2_ref[...].astype(F32) * sin).astype(BF16)
    n_rot = MLA_HEADS * MLA_DR
    for h in range(MLA_HEADS):
        c = h // 2
        qr = qa[:, MLA_W + c * LANE:MLA_W + (c + 1) * LANE]
        qs = qa[:, MLA_W + n_rot + c * LANE:MLA_W + n_rot + (c + 1) * LANE]
        rot = jnp.where(half_mask[h % 2], qr * cos + qs * sin, 0.0)
        q_ref[h, :, 0:LANE] = (qa[:, h * LANE:(h + 1) * LANE] * _Q_SCALE).astype(BF16)
        q_ref[h, :, LANE:2 * LANE] = (rot * _Q_SCALE).astype(BF16)
        k_ref[h, :, 0:LANE] = kn[:, h * LANE:(h + 1) * LANE].astype(BF16)
        k_ref[h, :, LANE:2 * LANE] = k_rope
        vt_ref[h, 0:MLA_DV, :] = vt[h * MLA_DV:(h + 1) * MLA_DV, :].astype(BF16)
        vt_ref[h, MLA_DV:_VT_ROWS, :] = jnp.ones((_VT_ROWS - MLA_DV, vt.shape[1]), BF16)


def _qkv(u_all, qg, kg, wq, wk, wvt, cos_t, sin_t, l, *, tr=512):
    n_lat = ROWS_L // tr
    per_seq = SEQ // tr

    def tab(i):
        return (jnp.where(i < n_lat, i % per_seq, per_seq), 0)

    return pl.pallas_call(
        _qkv_kernel,
        grid=(ROWS_ALL // tr,),
        in_specs=[
            pl.BlockSpec((tr, Q_RANK), lambda i: (i, U_Q // Q_RANK)),
            pl.BlockSpec((tr, KV_RANK), lambda i: (i, U_KV // KV_RANK)),
            pl.BlockSpec((tr, LANE), lambda i: (i, U_KR1 // LANE)),
            pl.BlockSpec((tr, LANE), lambda i: (i, U_KR2 // LANE)),
            _layer_spec(qg, l), _layer_spec(kg, l), _layer_spec(wq, l), _layer_spec(wk, l), _layer_spec(wvt, l),
            pl.BlockSpec((tr, LANE), tab),
            pl.BlockSpec((tr, LANE), tab),
        ],
        out_specs=[
            pl.BlockSpec((MLA_HEADS, tr, 2 * LANE), lambda i: (0, i, 0)),
            pl.BlockSpec((MLA_HEADS, tr, 2 * LANE), lambda i: (0, i, 0)),
            pl.BlockSpec((MLA_HEADS, _VT_ROWS, tr), lambda i: (0, 0, i)),
        ],
        out_shape=[
            jax.ShapeDtypeStruct((MLA_HEADS, ROWS_ALL, 2 * LANE), BF16),
            jax.ShapeDtypeStruct((MLA_HEADS, ROWS_ALL, 2 * LANE), BF16),
            jax.ShapeDtypeStruct((MLA_HEADS, _VT_ROWS, ROWS_ALL), BF16),
        ],
        compiler_params=_cparams(("parallel",)),
        name="qkv",
    )(u_all, u_all, u_all, u_all, qg, kg, wq, wk, wvt, cos_t, sin_t)


_KEY_CHUNK = 1024
_SCORE_LEAD = 1


def _fold_max(x):
    rows = x.shape[0]
    while rows > 8 and rows % 16 == 0:
        rows //= 2
        x = jnp.maximum(x[:rows], x[rows:])
    return x.max(axis=0, keepdims=True)


def _attn_kernel(*refs, n_src):
    q_ref = refs[0]
    k_refs = refs[1:1 + n_src]
    vt_refs = refs[1 + n_src:1 + 2 * n_src]
    g_ref = refs[1 + 2 * n_src]
    o_ref = refs[2 + 2 * n_src]
    chunks = [(k, vt, r0, min(_KEY_CHUNK, k.shape[1] - r0))
              for k, vt in zip(k_refs, vt_refs) for r0 in range(0, k.shape[1], _KEY_CHUNK)]

    def scores(h):
        return [_dot_nt(k[h, r0:r0 + n, :], q_ref[h]) for k, _, r0, n in chunks]

    pending = [scores(h) for h in range(_SCORE_LEAD)]
    for h in range(MLA_HEADS):
        if h + _SCORE_LEAD < MLA_HEADS:
            pending.append(scores(h + _SCORE_LEAD))
        s = pending.pop(0)
        m = _fold_max(s[0])
        for si in s[1:]:
            m = jnp.maximum(m, _fold_max(si))
        acc = None
        for si, (_, vt, r0, n) in zip(s, chunks):
            pv = _dot(vt[h, :, r0:r0 + n], jnp.exp2((si - m).astype(BF16)))
            acc = pv if acc is None else acc + pv
        out = (acc[0:MLA_DV] / acc[MLA_DV:MLA_DV + 1]).T
        gate = _silu(g_ref[:, h * LANE:(h + 1) * LANE].astype(F32))
        o_ref[:, h * LANE:(h + 1) * LANE] = (out * gate).astype(o_ref.dtype)


def _attention(q, k, vt, u_all, *, latent, tq=256):
    H = MLA_HEADS
    ctx_blk0 = ROWS_L // CTX_LEN
    if latent:
        n_q = SEQ // tq
        q_off = 0
        k_specs = [pl.BlockSpec((H, SEQ, 2 * LANE), lambda b, i: (0, b, 0)),
                   pl.BlockSpec((H, CTX_LEN, 2 * LANE), lambda b, i: (0, ctx_blk0 + b, 0))]
        v_specs = [pl.BlockSpec((H, _VT_ROWS, SEQ), lambda b, i: (0, 0, b)),
                   pl.BlockSpec((H, _VT_ROWS, CTX_LEN), lambda b, i: (0, 0, ctx_blk0 + b))]
    else:
        n_q = CTX_LEN // tq
        q_off = ROWS_L // tq
        k_specs = [pl.BlockSpec((H, CTX_LEN, 2 * LANE), lambda b, i: (0, ctx_blk0 + b, 0))]
        v_specs = [pl.BlockSpec((H, _VT_ROWS, CTX_LEN), lambda b, i: (0, 0, ctx_blk0 + b))]
    n_src = len(k_specs)
    in_specs = ([pl.BlockSpec((H, tq, 2 * LANE), lambda b, i: (0, q_off + b * n_q + i, 0))] + k_specs + v_specs
                + [pl.BlockSpec((tq, MLA_W), lambda b, i: (q_off + b * n_q + i, U_MLA_G // MLA_W))])
    args = [q] + [k] * n_src + [vt] * n_src + [u_all]
    return pl.pallas_call(
        functools.partial(_attn_kernel, n_src=n_src),
        grid=(BATCH, n_q),
        in_specs=in_specs,
        out_specs=pl.BlockSpec((tq, MLA_W), lambda b, i: (b * n_q + i, 0)),
        out_shape=jax.ShapeDtypeStruct((BATCH * n_q * tq, MLA_W), BF16),
        compiler_params=_cparams(("parallel", "parallel")),
        name="attn_latent" if latent else "attn_ctx",
    )(*args)


_POOL_PAD = 16


def _pool_kernel(x_ref, g_ref, w_ref, sc_ref, o_ref, pad_ref, *, L):
    zeros = jnp.zeros((_POOL_PAD, POOL_GROUP), F32)
    pad_ref[pl.ds(0, _POOL_PAD), :] = zeros
    pad_ref[pl.ds(L + _POOL_PAD, _POOL_PAD), :] = zeros
    t = lax.broadcasted_iota(jnp.int32, (L, 1), 0)
    for gi, win in enumerate(POOL_WINDOWS):
        half = win // 2
        cols = slice(gi * POOL_GROUP, (gi + 1) * POOL_GROUP)
        x = x_ref[:, cols].astype(F32)
        pad_ref[pl.ds(_POOL_PAD, L), :] = x
        acc = pad_ref[pl.ds(_POOL_PAD - half, L), :]
        for j in range(-half + 1, half):
            acc = acc + pad_ref[pl.ds(_POOL_PAD + j, L), :]
        cnt = (jnp.minimum(t + half, L) - jnp.maximum(t - half, 0)).astype(F32)
        dlt = (acc / cnt - x).astype(BF16)
        y = _dot(dlt, w_ref[gi]) * sc_ref[:, cols]
        o_ref[:, cols] = (y * _silu(g_ref[:, cols].astype(F32))).astype(o_ref.dtype)


def _pool(u_all, w_pool, pool_scale, l, *, L, row_blk0):
    return pl.pallas_call(
        functools.partial(_pool_kernel, L=L),
        grid=(BATCH,),
        in_specs=[
            pl.BlockSpec((L, POOL_W), lambda b: (row_blk0 + b, U_POOL // POOL_W)),
            pl.BlockSpec((L, POOL_W), lambda b: (row_blk0 + b, U_POOL_G // POOL_W)),
            _layer_spec(w_pool, l),
            _layer_spec(pool_scale, l),
        ],
        out_specs=pl.BlockSpec((L, POOL_W), lambda b: (b, 0)),
        out_shape=jax.ShapeDtypeStruct((BATCH * L, POOL_W), BF16),
        scratch_shapes=[pltpu.VMEM((L + 2 * _POOL_PAD, POOL_GROUP), F32)],
        compiler_params=_cparams(("parallel",)),
        name=f"pool_{L}",
    )(u_all, u_all, w_pool, pool_scale)


def _outproj_kernel(*refs, n_src, n_lat_tiles, n_streams, final, sub):
    x_refs = refs[:n_src]
    refs = refs[n_src:]
    mixers = [refs[3 * s:3 * s + 3] for s in range(n_streams)]
    w_ref, gt_ref, fg_ref, o_ref = refs[3 * n_streams:]

    def run(x_ref, hy_ref, at_ref, po_ref):
        for r in range(0, x_ref.shape[0], sub):
            rows = pl.ds(r, sub)
            acc = (_dot(hy_ref[rows, :], w_ref[0:HY_W, :]) + _dot(at_ref[rows, :], w_ref[HY_W:HY_W + MLA_W, :])
                   + _dot(po_ref[rows, :], w_ref[HY_W + MLA_W:, :]))
            y = x_ref[rows, :] + gt_ref[...] * acc
            if final:
                y = y * lax.rsqrt(jnp.mean(y * y, axis=-1, keepdims=True) + EPS) * fg_ref[...]
            o_ref[rows, :] = y

    if n_streams == 1:
        run(x_refs[0], *mixers[0])
    else:
        is_latent = pl.program_id(0) < n_lat_tiles
        pl.when(is_latent)(lambda: run(x_refs[0], *mixers[0]))
        pl.when(jnp.logical_not(is_latent))(lambda: run(x_refs[-1], *mixers[1]))


def _outproj(xs, mixers, w_out, mod, fg, l, *, final, tm=512):
    d = xs[0].shape[1]
    tiles_per_batch = SEQ // tm
    n_lat_tiles = ROWS_L // tm
    n_row_tiles = sum(m[0].shape[0] for m in mixers) // tm

    def mod_row(i):
        return jnp.minimum(i // tiles_per_batch, BATCH)

    lat_blk = lambda i: (jnp.minimum(i, n_lat_tiles - 1), 0)
    ctx_blk = lambda i: (jnp.maximum(i - n_lat_tiles, 0), 0)
    mixer_specs, mixer_args = [], []
    for blk, (hy, att, po) in zip((lat_blk, ctx_blk), mixers):
        mixer_specs += [pl.BlockSpec((tm, HY_W), blk), pl.BlockSpec((tm, MLA_W), blk), pl.BlockSpec((tm, POOL_W), blk)]
        mixer_args += [hy, att, po]

    return pl.pallas_call(
        functools.partial(_outproj_kernel, n_src=len(xs), n_lat_tiles=n_lat_tiles, n_streams=len(mixers),
                          final=final, sub=256),
        grid=(n_row_tiles,),
        in_specs=_row_source_specs(xs, tm) + mixer_specs + [
            pl.BlockSpec((None,) + w_out.shape[1:], lambda i: (l, 0, 0), pipeline_mode=pl.Buffered(1)),
            pl.BlockSpec((None, None, 1, d), lambda i: (l, mod_row(i), 0, 2)),
            pl.BlockSpec((1, d), lambda i: (0, 0)),
        ],
        out_specs=pl.BlockSpec((tm, d), lambda i: (i, 0)),
        out_shape=jax.ShapeDtypeStruct((n_row_tiles * tm, d), F32),
        compiler_params=_cparams(("parallel",)),
        name="outproj_final" if final else "outproj",
    )(*xs, *mixer_args, w_out, mod, fg)


def _rope_lane_tables(ident_rows):
    n_rows = SEQ // GRID_W
    row = np.repeat(np.arange(n_rows, dtype=np.float64), GRID_W)
    col = np.tile(np.arange(GRID_W, dtype=np.float64), n_rows)
    n_freq = MLA_DR // 4
    inv = ROPE_BASE ** (-np.arange(n_freq, dtype=np.float64) / n_freq)
    ang = np.concatenate([row[:, None] * inv, col[:, None] * inv], axis=-1)
    cos, sin = np.cos(ang), np.sin(ang)
    cos_t = np.concatenate([cos, cos, cos, cos], axis=-1)
    sin_t = np.concatenate([-sin, sin, -sin, sin], axis=-1)
    cos_t = np.concatenate([cos_t, np.ones((ident_rows, LANE))], axis=0)
    sin_t = np.concatenate([sin_t, np.zeros((ident_rows, LANE))], axis=0)
    return jnp.asarray(cos_t, F32), jnp.asarray(sin_t, F32)


def _dft_tables(P, dtype):
    idx = np.arange(P, dtype=np.int64)
    ang = ((idx[:, None] * idx[None, :]) % (2 * P)).astype(np.float64) * (math.pi / P)
    return jnp.asarray(np.cos(ang), F32).astype(dtype), jnp.asarray(np.sin(ang), F32).astype(dtype)


def _filter_tables(L):
    t = np.linspace(0.0, 1.0, L)[:, None]
    wpos = (2.0 * math.pi / L) * np.arange(L, dtype=np.float64)[:, None]
    bands = np.linspace(1e-4, FILTER_BANDS - 1, FILTER_BANDS)[None, :]
    feats = np.concatenate([t, np.cos(bands * wpos), -np.sin(bands * wpos)], axis=-1)
    feats = np.pad(feats, ((0, 0), (0, LANE - FILTER_EMB)))
    return jnp.asarray(feats, F32), jnp.asarray(t, F32)


def _decay_rates():
    d = np.abs(np.linspace(math.log(DECAY_TARGET) / SLOW_DECAY, math.log(DECAY_TARGET) / FAST_DECAY, HY_W))
    return jnp.asarray(d[None, :], F32)


def _kr_permutation():
    p = np.zeros((2 * LANE, MLA_DR), np.float32)
    half = MLA_DR // 2
    for grp, odd in enumerate((0, 1, 0, 1, 1, 0, 1, 0)):
        for i in range(half):
            p[grp * half + i, 2 * i + odd] = 1.0
    return jnp.asarray(p, BF16)


_PACK_BLK = U_W // 4
_PACK_KR_STEP = U_KR1 // _PACK_BLK
_PACK_KR_ROW = U_KR1 - _PACK_KR_STEP * _PACK_BLK
_PACK_SHIFT = U_MLA_G - R_OFF_MLA_G


def _pack_w_in_kernel(a3_ref, p_ref, o_ref):
    i = pl.program_id(1)
    a_ref = a3_ref.at[0]

    @pl.when(i != _PACK_KR_STEP)
    def _():
        o_ref[...] = a_ref[...].astype(BF16)

    @pl.when(i == _PACK_KR_STEP)
    def _():
        r0, r1 = _PACK_KR_ROW, _PACK_KR_ROW + 2 * LANE
        o_ref[0:r0, :] = a_ref[0:r0, :].astype(BF16)
        o_ref[r0:r1, :] = _dot(p_ref[...], a_ref[r0:r0 + MLA_DR, :].astype(BF16)).astype(BF16)
        o_ref[r1:, :] = a_ref[r1 - _PACK_SHIFT:_PACK_BLK - _PACK_SHIFT, :].astype(BF16)


def _pack_w_in(w_in_t):
    depth, n, d = w_in_t.shape
    assert U_KR1 == R_OFF_KR and U_MLA_G == U_KR1 + 2 * LANE and n == U_W - _PACK_SHIFT
    perm = _kr_permutation()

    def src_row(l, i):
        row = jnp.where(i <= _PACK_KR_STEP, i * _PACK_BLK, i * _PACK_BLK - _PACK_SHIFT)
        return (l, pl.multiple_of(row, MLA_DR), 0)

    return pl.pallas_call(
        _pack_w_in_kernel,
        grid=(depth, U_W // _PACK_BLK),
        in_specs=[pl.BlockSpec((pl.Element(1), pl.Element(_PACK_BLK), pl.Element(d)), src_row),
                  pl.BlockSpec(perm.shape, lambda l, i: (0, 0))],
        out_specs=pl.BlockSpec((None, _PACK_BLK, d), lambda l, i: (l, i, 0)),
        out_shape=jax.ShapeDtypeStruct((depth, U_W, d), BF16),
        compiler_params=_cparams(("parallel", "parallel")),
        name="pack_w_in",
    )(w_in_t, perm)


def _pack_w_uq(w_uq):
    w = w_uq.reshape(DEPTH, Q_RANK, MLA_HEADS, MLA_DN + MLA_DR)
    nope = w[..., :MLA_DN].reshape(DEPTH, Q_RANK, MLA_W)
    a, b = w[..., MLA_DN::2], w[..., MLA_DN + 1::2]
    rot = jnp.concatenate([a, b], axis=-1).reshape(DEPTH, Q_RANK, MLA_HEADS * MLA_DR)
    swp = jnp.concatenate([b, a], axis=-1).reshape(DEPTH, Q_RANK, MLA_HEADS * MLA_DR)
    return jnp.concatenate([nope, rot, swp], axis=-1).astype(BF16)


def _pack_w_ukv(w_ukv):
    w = w_ukv.reshape(DEPTH, KV_RANK, MLA_HEADS, MLA_DN + MLA_DV)
    wk = w[..., :MLA_DN].reshape(DEPTH, KV_RANK, MLA_W)
    wv = w[..., MLA_DN:].reshape(DEPTH, KV_RANK, MLA_W)
    return wk.astype(BF16), jnp.swapaxes(wv, 1, 2).astype(BF16)


def kernel(x, c, ctx, c_ctx, norm_g, w_ada, b_ada, w_in, hy_conv_w, hy_conv_b, hf_w1, hf_b1, hf_freq,
           hf_w2, hf_b2, hf_w3, hy_bias, q_norm_g, w_uq, kv_norm_g, w_ukv, w_pool, pool_scale, w_out,
           final_norm_g):
    assert x.shape == (BATCH, SEQ, D_MODEL) and ctx.shape == (BATCH, CTX_LEN, D_MODEL)

    qkv_tr = 512
    cos_t, sin_t = _rope_lane_tables(qkv_tr)

    w_in_p = _pack_w_in(jnp.swapaxes(w_in, 1, 2))
    w_q_p = _pack_w_uq(w_uq)
    w_k_p, w_vt_p = _pack_w_ukv(w_ukv)
    w_out_b = w_out.astype(BF16)
    w_pool_b = w_pool.astype(BF16)
    w1_p = jnp.pad(hf_w1, ((0, 0), (0, LANE - FILTER_EMB), (0, 0)))
    w3_p = hf_w3.reshape(DEPTH, FILTER_HIDDEN, 2, 2, HY_W).transpose(0, 2, 3, 1, 4)
    conv_w = hy_conv_w.reshape(DEPTH, 3, 3, HY_W)
    conv_b = hy_conv_b.reshape(DEPTH, 3, HY_W)
    rows = lambda a: a[:, None, :]
    norm_g3, qg3, kg3, ps3 = rows(norm_g), rows(q_norm_g), rows(kv_norm_g), rows(pool_scale)
    b1_3, fr_3, b2_3 = rows(hf_b1), rows(hf_freq), rows(hf_b2)

    cond = jnp.concatenate([c, c_ctx[None], jnp.zeros((8 - BATCH - 1, D_MODEL), F32)], axis=0)
    mod = _adaln(cond, w_ada, b_ada[:, None, :])
    mod = mod.reshape(DEPTH, 8, 1, 3 * D_MODEL)

    xs = [x.reshape(ROWS_L, D_MODEL), ctx.reshape(ROWS_C, D_MODEL)]

    for l in range(DEPTH):
        last = l == DEPTH - 1
        u_all = _inproj(xs, norm_g3, mod, w_in_p, l, ctx_keys_only=last)
        q, k, vt = _qkv(u_all, qg3, kg3, w_q_p, w_k_p, w_vt_p, cos_t, sin_t, l, tr=qkv_tr)
        mixers = []
        for L, row_blk0, latent in ((SEQ, 0, True), (CTX_LEN, ROWS_L // CTX_LEN, False)):
            if last and not latent:
                continue
            g, gn = _hyena_filter(L, w1_p, b1_3, fr_3, hf_w2, b2_3, w3_p, l)
            hy = _hyena(u_all, conv_w, conv_b, hy_bias, g, gn, l, L=L, row_blk0=row_blk0)
            att = _attention(q, k, vt, u_all, latent=latent, tq=256)
            po = _pool(u_all, w_pool_b, ps3, l, L=L, row_blk0=row_blk0)
            mixers.append((hy, att, po))
        xs = [_outproj(xs, mixers, w_out_b, mod, final_norm_g[None, :], l, final=last)]
    return xs[0].reshape(BATCH, SEQ, D_MODEL)
```

```python
import functools
import math

import jax
import jax.numpy as jnp
import numpy as np
from jax import lax
from jax.experimental import pallas as pl
from jax.experimental.pallas import tpu as pltpu

F32 = jnp.float32
BF16 = jnp.bfloat16

D_MODEL = 2048
BATCH = 4
SEQ = 2048
DEPTH = 4
CTX_LEN = 256
GRID_W = 64
EPS = 1e-6

HY_W = 512
MLA_HEADS = 8
MLA_DN = 128
MLA_DR = 64
MLA_DV = 128
MLA_W = MLA_HEADS * MLA_DV
Q_RANK = 512
KV_RANK = 256
POOL_W = 512
POOL_WINDOWS = (2, 4, 8, 16)
POOL_GROUP = 128

FILTER_EMB = 33
FILTER_BANDS = 16
FILTER_HIDDEN = 64
DECAY_TARGET = 1e-2
FAST_DECAY = 0.3
SLOW_DECAY = 1.5
ROPE_BASE = 10000.0
ATTN_SCALE = (MLA_DN + MLA_DR) ** -0.5

R_OFF_KR = 2816
R_OFF_MLA_G = 2880

U_HY = 0
U_Q = 2048
U_KV = 2560
U_KR1 = 2816
U_KR2 = 2944
U_MLA_G = 3072
U_POOL = 4096
U_POOL_G = 4608
U_W = 5120

ROWS_L = BATCH * SEQ
ROWS_C = BATCH * CTX_LEN
ROWS_ALL = ROWS_L + ROWS_C

VMEM_LIMIT_BYTES = 56 * 1024 * 1024
LANE = 128


def _cparams(sem):
    return pltpu.CompilerParams(dimension_semantics=sem, vmem_limit_bytes=VMEM_LIMIT_BYTES)


def _silu(x):
    return x * jax.nn.sigmoid(x)


def _dot(a, b):
    return jnp.dot(a, b, preferred_element_type=F32)


def _dot_bf16x3(a, b):
    a_hi, b_hi = a.astype(BF16), b.astype(BF16)
    a_lo = (a - a_hi.astype(F32)).astype(BF16)
    b_lo = (b - b_hi.astype(F32)).astype(BF16)
    return _dot(a_hi, b_hi) + (_dot(a_hi, b_lo) + _dot(a_lo, b_hi))


def _dot_nt(a, b):
    return lax.dot_general(a, b, (((1,), (1,)), ((), ())), preferred_element_type=F32)


def _adaln_kernel(c_ref, w_ref, b_ref, o_ref):
    a = _silu(c_ref[...]).astype(BF16)
    o_ref[...] = _dot(a, w_ref[...].astype(BF16)) + b_ref[...]


def _adaln(cond, w_ada, b_ada, tn=1024):
    depth, d, n = w_ada.shape
    return pl.pallas_call(
        _adaln_kernel,
        grid=(depth, n // tn),
        in_specs=[
            pl.BlockSpec((8, d), lambda l, j: (0, 0)),
            pl.BlockSpec((None, d, tn), lambda l, j: (l, 0, j)),
            pl.BlockSpec((None, 1, tn), lambda l, j: (l, 0, j)),
        ],
        out_specs=pl.BlockSpec((None, 8, tn), lambda l, j: (l, 0, j)),
        out_shape=jax.ShapeDtypeStruct((depth, 8, n), F32),
        compiler_params=_cparams(("parallel", "parallel")),
        name="adaln",
    )(cond, w_ada, b_ada)


def _inproj_kernel(*refs, n_src, n_lat_tiles, ctx_col_tile, sub):
    x_refs = refs[:n_src]
    g_ref, sh_ref, sc_ref, w_ref, o_ref, h_ref = refs[n_src:]
    first = pl.program_id(1) == 0
    is_latent = pl.program_id(0) < n_lat_tiles
    both = jnp.logical_and

    def norm_and_project(x_ref, project=True):
        gs = g_ref[...] * (1.0 + sc_ref[...])
        sh = sh_ref[...]
        for r in range(0, x_ref.shape[0], sub):
            rows = pl.ds(r, sub)
            x = x_ref[rows, :]
            ms = jnp.mean(x * x, axis=-1, keepdims=True)
            h = (x * lax.rsqrt(ms + EPS) * gs + sh).astype(BF16)
            h_ref[rows, :] = h
            if project:
                o_ref[rows, :] = _dot_nt(h, w_ref[...]).astype(o_ref.dtype)
        if not project:
            o_ref[...] = jnp.zeros_like(o_ref)

    def project_all():
        o_ref[...] = _dot_nt(h_ref[...], w_ref[...]).astype(o_ref.dtype)

    if ctx_col_tile is None:
        if n_src == 1:
            pl.when(first)(lambda: norm_and_project(x_refs[0]))
        else:
            pl.when(both(first, is_latent))(lambda: norm_and_project(x_refs[0]))
            pl.when(both(first, jnp.logical_not(is_latent)))(lambda: norm_and_project(x_refs[1]))
        pl.when(jnp.logical_not(first))(project_all)
    else:
        assert n_src == 1 and ctx_col_tile > 0
        wanted = jnp.logical_or(is_latent, pl.program_id(1) == ctx_col_tile)
        pl.when(both(first, is_latent))(lambda: norm_and_project(x_refs[0]))
        pl.when(both(first, jnp.logical_not(is_latent)))(lambda: norm_and_project(x_refs[0], project=False))
        pl.when(both(jnp.logical_not(first), wanted))(project_all)

        @pl.when(both(jnp.logical_not(first), jnp.logical_not(wanted)))
        def _():
            o_ref[...] = jnp.zeros_like(o_ref)


def _layer_spec(arr, l):
    zeros = (0,) * (arr.ndim - 1)
    return pl.BlockSpec((None,) + arr.shape[1:], lambda *_: (l,) + zeros)


def _row_source_specs(xs, tm):
    d = xs[0].shape[1]
    n_lat_tiles = ROWS_L // tm
    if len(xs) == 1:
        return [pl.BlockSpec((tm, d), lambda i, *_: (i, 0))]
    return [pl.BlockSpec((tm, d), lambda i, *_: (jnp.minimum(i, n_lat_tiles - 1), 0)),
            pl.BlockSpec((tm, d), lambda i, *_: (jnp.maximum(i - n_lat_tiles, 0), 0))]


def _inproj(xs, g, mod, w_t, l, *, ctx_keys_only=False, tm=1024, tn=1280):
    d = xs[0].shape[1]
    n = w_t.shape[1]
    tiles_per_batch = SEQ // tm
    n_row_tiles = ROWS_ALL // tm
    ctx_col_tile = None
    if ctx_keys_only:
        ctx_col_tile = U_KV // tn
        assert ctx_col_tile * tn <= U_KV and U_MLA_G <= (ctx_col_tile + 1) * tn

    def mod_row(i):
        return jnp.minimum(i // tiles_per_batch, BATCH)

    return pl.pallas_call(
        functools.partial(_inproj_kernel, n_src=len(xs), n_lat_tiles=ROWS_L // tm, ctx_col_tile=ctx_col_tile,
                          sub=256),
        grid=(n_row_tiles, n // tn),
        in_specs=_row_source_specs(xs, tm) + [
            _layer_spec(g, l),
            pl.BlockSpec((None, None, 1, d), lambda i, j: (l, mod_row(i), 0, 0)),
            pl.BlockSpec((None, None, 1, d), lambda i, j: (l, mod_row(i), 0, 1)),
            pl.BlockSpec((None, tn, d), lambda i, j: (l, j, 0)),
        ],
        out_specs=pl.BlockSpec((tm, tn), lambda i, j: (i, j)),
        out_shape=jax.ShapeDtypeStruct((ROWS_ALL, n), BF16),
        scratch_shapes=[pltpu.VMEM((tm, d), BF16)],
        compiler_params=_cparams(("parallel", "arbitrary")),
        name="inproj",
    )(*xs, g, mod, mod, w_t)


def _hyena_block(L):
    return min(L, 512)


def _filter_kernel(feats_ref, t_ref, dl_ref, w1_ref, b1_ref, fr_ref, w2_ref, b2_ref, w3_ref,
                   c_ref, s_ref, g_ref, gn_ref, h_ref, *, L, P):
    nb = L // P
    m_fft = 2 * P

    @pl.when(pl.program_id(0) == 0)
    def _():
        fr = fr_ref[...]
        h1 = jnp.sin(fr * (_dot_bf16x3(feats_ref[...], w1_ref[...]) + b1_ref[...]))
        h_ref[...] = jnp.sin(fr * (_dot_bf16x3(h1, w2_ref[...]) + b2_ref[...]))

    h = h_ref[...]
    decay = jnp.exp(-t_ref[...] * dl_ref[...])
    row = lax.broadcasted_iota(jnp.int32, (L, 1), 0)
    rp = lax.broadcasted_iota(jnp.int32, (P, 1), 0)
    sgn = jnp.where(rp % 2 == 0, 1.0, -1.0).astype(F32)
    wgt = jnp.where(rp == 0, 1.0 / m_fft, 2.0 / m_fft).astype(F32)
    cm = c_ref[...]
    sm = s_ref[...]
    hf = _dot_bf16x3(h, w3_ref[0]) * decay
    hb = _dot_bf16x3(h, w3_ref[1]) * decay
    hb = jnp.where(row == 0, 0.0, hb)
    nrm = lax.rsqrt(jnp.sum(hf * hf + hb * hb, axis=0, keepdims=True) + EPS)
    F, B = [], []
    for arr, out in ((hf * nrm, F), (hb * nrm, B)):
        for j in range(nb):
            blk = arr[j * P:(j + 1) * P].astype(BF16)
            b32 = blk.astype(F32)
            out.append((_dot(cm, blk), _dot(sm, blk), jnp.sum(b32 * sgn, axis=0, keepdims=True), b32[0:1]))
    for d in range(-(nb - 1), nb):
        if d >= 1:
            gr = F[d][0] + sgn * (F[d - 1][0] - F[d - 1][3])
            gi = F[d][1] + sgn * F[d - 1][1]
            gn = F[d][2] + F[d - 1][2] - F[d - 1][3]
        elif d == 0:
            gr = F[0][0] + B[0][0]
            gi = F[0][1] - B[0][1]
            gn = F[0][2] + B[0][2]
        else:
            e = -d
            gr = B[e][0] + sgn * (B[e - 1][0] - B[e - 1][3])
            gi = -B[e][1] - sgn * B[e - 1][1]
            gn = B[e][2] + B[e - 1][2] - B[e - 1][3]
        g_ref[d + nb - 1, 0] = gr * wgt
        g_ref[d + nb - 1, 1] = gi * wgt
        gn_ref[d + nb - 1] = gn * (1.0 / m_fft)


def _hyena_filter(L, w1, b1, fr, w2, b2, w3, l, *, cb=256):
    P = _hyena_block(L)
    nd = 2 * (L // P) - 1
    feats, t = _filter_tables(L)
    deltas = _decay_rates()
    cmat, smat = _dft_tables(P, BF16)
    full2 = lambda s: (0, 0)
    return pl.pallas_call(
        functools.partial(_filter_kernel, L=L, P=P),
        grid=(2 * (HY_W // cb),),
        in_specs=[
            pl.BlockSpec(feats.shape, full2),
            pl.BlockSpec(t.shape, full2),
            pl.BlockSpec((1, cb), lambda s: (0, s // 2)),
            _layer_spec(w1, l), _layer_spec(b1, l), _layer_spec(fr, l), _layer_spec(w2, l), _layer_spec(b2, l),
            pl.BlockSpec((None, None, 2, FILTER_HIDDEN, cb), lambda s: (l, s % 2, 0, 0, s // 2)),
            pl.BlockSpec((P, P), full2),
            pl.BlockSpec((P, P), full2),
        ],
        out_specs=[
            pl.BlockSpec((None, nd, 2, P, cb), lambda s: (s % 2, 0, 0, 0, s // 2)),
            pl.BlockSpec((None, nd, 1, cb), lambda s: (s % 2, 0, 0, s // 2)),
        ],
        out_shape=[
            jax.ShapeDtypeStruct((2, nd, 2, P, HY_W), F32),
            jax.ShapeDtypeStruct((2, nd, 1, HY_W), F32),
        ],
        scratch_shapes=[pltpu.VMEM((L, FILTER_HIDDEN), F32)],
        compiler_params=_cparams(("arbitrary",)),
        name=f"hyena_filter_{L}",
    )(feats, t, deltas, w1, b1, fr, w2, b2, w3, cmat, smat)


_MAC_ROWS = 32


def _hyena_kernel(v_ref, x1_ref, x2_ref, gate_ref, cw_ref, cb_ref, hb_ref, g_ref, gn_ref,
                  c_ref, s_ref, o_ref, z_ref, zb_ref, xc_ref, zf_ref, yr_ref, yi_ref, *, L, P):
    nb = L // P

    def short_conv(dst_ref, src_ref, p):
        x = src_ref[...].astype(F32)
        w = cw_ref[:, p, :]
        bias = cb_ref[p:p + 1, :]
        prev = pltpu.roll(x, 1, axis=0)
        nxt = pltpu.roll(x, L - 1, axis=0)
        dst_ref[...] = prev * w[0:1] + x * w[1:2] + nxt * w[2:3] + bias
        dst_ref[0:1, :] = x[0:1] * w[1:2] + x[1:2] * w[2:3] + bias
        dst_ref[L - 1:L, :] = x[L - 2:L - 1] * w[0:1] + x[L - 1:L] * w[1:2] + bias

    sign = jnp.where(lax.broadcasted_iota(jnp.int32, (P, 1), 0) % 2 == 0, 1.0, -1.0).astype(F32)
    cm = c_ref[...]
    sm = s_ref[...]

    short_conv(z_ref, v_ref, 0)
    for o in range(2):
        zb_ref[...] = z_ref[...].astype(BF16)
        zn = []
        for j in range(nb):
            rows = pl.ds(j * P, P)
            zf_ref[0, j] = _dot(cm, zb_ref[rows, :])
            zf_ref[1, j] = _dot(sm, zb_ref[rows, :])
            zn.append(jnp.sum(z_ref[rows, :] * sign, axis=0, keepdims=True))
        short_conv(xc_ref, x1_ref if o == 0 else x2_ref, 1 + o)
        bias = hb_ref[o:o + 1, :]
        def spectrum_product(i):
            for r in range(0, P, _MAC_ROWS):
                rr = pl.ds(r, _MAC_ROWS)
                yr = yi = None
                for j in range(nb):
                    d = i - j + nb - 1
                    gr, gi = g_ref[o, d, 0, rr, :], g_ref[o, d, 1, rr, :]
                    zr, zi = zf_ref[0, j, rr, :], zf_ref[1, j, rr, :]
                    pr, pi = gr * zr - gi * zi, gr * zi + gi * zr
                    yr, yi = (pr, pi) if yr is None else (yr + pr, yi + pi)
                yr_ref[i, rr, :] = yr.astype(BF16)
                yi_ref[i, rr, :] = yi.astype(BF16)

        spectrum_product(0)
        for i in range(nb):
            if i + 1 < nb:
                spectrum_product(i + 1)
            yn = gn_ref[o, i + nb - 1] * zn[0]
            for j in range(1, nb):
                yn = yn + gn_ref[o, i - j + nb - 1] * zn[j]
            rows = pl.ds(i * P, P)
            y = _dot(cm, yr_ref[i]) + _dot(sm, yi_ref[i])
            y = y + sign * yn + z_ref[rows, :] * bias
            y = xc_ref[rows, :] * y
            if o == 0:
                z_ref[rows, :] = y
            else:
                o_ref[rows, :] = (y * _silu(gate_ref[rows, :].astype(F32))).astype(o_ref.dtype)


def _hyena(u_all, conv_w, conv_b, hy_bias, g, gn, l, *, L, row_blk0, cb=256):
    P = _hyena_block(L)
    nb = L // P
    nd = 2 * nb - 1
    ncb = HY_W // cb
    cmat, smat = _dft_tables(P, BF16)

    def ublk(part):
        return pl.BlockSpec((L, cb), lambda j, b: (row_blk0 + b, part * ncb + j))

    in_specs = [
        ublk(0), ublk(1), ublk(2), ublk(3),
        pl.BlockSpec((None, 3, 3, cb), lambda j, b: (l, 0, 0, j)),
        pl.BlockSpec((None, 3, cb), lambda j, b: (l, 0, j)),
        pl.BlockSpec((None, 2, cb), lambda j, b: (l, 0, j)),
        pl.BlockSpec((2, nd, 2, P, cb), lambda j, b: (0, 0, 0, 0, j)),
        pl.BlockSpec((2, nd, 1, cb), lambda j, b: (0, 0, 0, j)),
        pl.BlockSpec((P, P), lambda j, b: (0, 0)),
        pl.BlockSpec((P, P), lambda j, b: (0, 0)),
    ]
    args = [u_all, u_all, u_all, u_all, conv_w, conv_b, hy_bias, g, gn, cmat, smat]
    return pl.pallas_call(
        functools.partial(_hyena_kernel, L=L, P=P),
        grid=(ncb, BATCH),
        in_specs=in_specs,
        out_specs=pl.BlockSpec((L, cb), lambda j, b: (b, j)),
        out_shape=jax.ShapeDtypeStruct((BATCH * L, HY_W), BF16),
        scratch_shapes=[
            pltpu.VMEM((L, cb), F32),
            pltpu.VMEM((L, cb), BF16),
            pltpu.VMEM((L, cb), F32),
            pltpu.VMEM((2, nb, P, cb), F32),
            pltpu.VMEM((nb, P, cb), BF16),
            pltpu.VMEM((nb, P, cb), BF16),
        ],
        compiler_params=_cparams(("parallel", "parallel")),
        name=f"hyena_{L}",
    )(*args)


_Q_SCALE = ATTN_SCALE * math.log2(math.e)
_VT_ROWS = MLA_DV + 16


def _qkv_kernel(uq_ref, ukv_ref, k1_ref, k2_ref, qg_ref, kg_ref, wq_ref, wk_ref, wvt_ref, cos_ref, sin_ref,
                q_ref, k_ref, vt_ref):
    def rms(x, g):
        x = x.astype(F32)
        return (x * lax.rsqrt(jnp.mean(x * x, axis=-1, keepdims=True) + EPS) * g).astype(BF16)

    cos = cos_ref[...]
    sin = sin_ref[...]
    lane = lax.broadcasted_iota(jnp.int32, (1, LANE), 1)
    half_mask = [(lane < MLA_DR), (lane >= MLA_DR)]

    qa = _dot(rms(uq_ref[...], qg_ref[...]), wq_ref[...])
    kvn = rms(ukv_ref[...], kg_ref[...])
    kn = _dot(kvn, wk_ref[...])
    vt = _dot_nt(wvt_ref[...], kvn)
    k_rope = (k1_ref[...].astype(F32) * cos + k2_ref[...].astype(F32) * sin).astype(BF16)
    n_rot = MLA_HEADS * MLA_DR
    for h in range(MLA_HEADS):
        c = h // 2
        qr = qa[:, MLA_W + c * LANE:MLA_W + (c + 1) * LANE]
        qs = qa[:, MLA_W + n_rot + c * LANE:MLA_W + n_rot + (c + 1) * LANE]
        rot = jnp.where(half_mask[h % 2], qr * cos + qs * sin, 0.0)
        q_ref[h, :, 0:LANE] = (qa[:, h * LANE:(h + 1) * LANE] * _Q_SCALE).astype(BF16)
        q_ref[h, :, LANE:2 * LANE] = (rot * _Q_SCALE).astype(BF16)
        k_ref[h, :, 0:LANE] = kn[:, h * LANE:(h + 1) * LANE].astype(BF16)
        k_ref[h, :, LANE:2 * LANE] = k_rope
        vt_ref[h, 0:MLA_DV, :] = vt[h * MLA_DV:(h + 1) * MLA_DV, :].astype(BF16)
        vt_ref[h, MLA_DV:_VT_ROWS, :] = jnp.ones((_VT_ROWS - MLA_DV, vt.shape[1]), BF16)


def _qkv(u_all, qg, kg, wq, wk, wvt, cos_t, sin_t, l, *, tr=512):
    n_lat = ROWS_L // tr
    per_seq = SEQ // tr

    def tab(i):
        return (jnp.where(i < n_lat, i % per_seq, per_seq), 0)

    return pl.pallas_call(
        _qkv_kernel,
        grid=(ROWS_ALL // tr,),
        in_specs=[
            pl.BlockSpec((tr, Q_RANK), lambda i: (i, U_Q // Q_RANK)),
            pl.BlockSpec((tr, KV_RANK), lambda i: (i, U_KV // KV_RANK)),
            pl.BlockSpec((tr, LANE), lambda i: (i, U_KR1 // LANE)),
            pl.BlockSpec((tr, LANE), lambda i: (i, U_KR2 // LANE)),
            _layer_spec(qg, l), _layer_spec(kg, l), _layer_spec(wq, l), _layer_spec(wk, l), _layer_spec(wvt, l),
            pl.BlockSpec((tr, LANE), tab),
            pl.BlockSpec((tr, LANE), tab),
        ],
        out_specs=[
            pl.BlockSpec((MLA_HEADS, tr, 2 * LANE), lambda i: (0, i, 0)),
            pl.BlockSpec((MLA_HEADS, tr, 2 * LANE), lambda i: (0, i, 0)),
            pl.BlockSpec((MLA_HEADS, _VT_ROWS, tr), lambda i: (0, 0, i)),
        ],
        out_shape=[
            jax.ShapeDtypeStruct((MLA_HEADS, ROWS_ALL, 2 * LANE), BF16),
            jax.ShapeDtypeStruct((MLA_HEADS, ROWS_ALL, 2 * LANE), BF16),
            jax.ShapeDtypeStruct((MLA_HEADS, _VT_ROWS, ROWS_ALL), BF16),
        ],
        compiler_params=_cparams(("parallel",)),
        name="qkv",
    )(u_all, u_all, u_all, u_all, qg, kg, wq, wk, wvt, cos_t, sin_t)


_KEY_CHUNK = 1024
_SCORE_LEAD = 1


def _fold_max(x):
    rows = x.shape[0]
    while rows > 8 and rows % 16 == 0:
        rows //= 2
        x = jnp.maximum(x[:rows], x[rows:])
    return x.max(axis=0, keepdims=True)


def _attn_kernel(*refs, n_src):
    q_ref = refs[0]
    k_refs = refs[1:1 + n_src]
    vt_refs = refs[1 + n_src:1 + 2 * n_src]
    g_ref = refs[1 + 2 * n_src]
    o_ref = refs[2 + 2 * n_src]
    chunks = [(k, vt, r0, min(_KEY_CHUNK, k.shape[1] - r0))
              for k, vt in zip(k_refs, vt_refs) for r0 in range(0, k.shape[1], _KEY_CHUNK)]

    def scores(h):
        return [_dot_nt(k[h, r0:r0 + n, :], q_ref[h]) for k, _, r0, n in chunks]

    pending = [scores(h) for h in range(_SCORE_LEAD)]
    for h in range(MLA_HEADS):
        if h + _SCORE_LEAD < MLA_HEADS:
            pending.append(scores(h + _SCORE_LEAD))
        s = pending.pop(0)
        m = _fold_max(s[0])
        for si in s[1:]:
            m = jnp.maximum(m, _fold_max(si))
        acc = None
        for si, (_, vt, r0, n) in zip(s, chunks):
            pv = _dot(vt[h, :, r0:r0 + n], jnp.exp2((si - m).astype(BF16)))
            acc = pv if acc is None else acc + pv
        out = (acc[0:MLA_DV] / acc[MLA_DV:MLA_DV + 1]).T
        gate = _silu(g_ref[:, h * LANE:(h + 1) * LANE].astype(F32))
        o_ref[:, h * LANE:(h + 1) * LANE] = (out * gate).astype(o_ref.dtype)


def _attention(q, k, vt, u_all, *, latent, tq=256):
    H = MLA_HEADS
    ctx_blk0 = ROWS_L // CTX_LEN
    if latent:
        n_q = SEQ // tq
        q_off = 0
        k_specs = [pl.BlockSpec((H, SEQ, 2 * LANE), lambda b, i: (0, b, 0)),
                   pl.BlockSpec((H, CTX_LEN, 2 * LANE), lambda b, i: (0, ctx_blk0 + b, 0))]
        v_specs = [pl.BlockSpec((H, _VT_ROWS, SEQ), lambda b, i: (0, 0, b)),
                   pl.BlockSpec((H, _VT_ROWS, CTX_LEN), lambda b, i: (0, 0, ctx_blk0 + b))]
    else:
        n_q = CTX_LEN // tq
        q_off = ROWS_L // tq
        k_specs = [pl.BlockSpec((H, CTX_LEN, 2 * LANE), lambda b, i: (0, ctx_blk0 + b, 0))]
        v_specs = [pl.BlockSpec((H, _VT_ROWS, CTX_LEN), lambda b, i: (0, 0, ctx_blk0 + b))]
    n_src = len(k_specs)
    in_specs = ([pl.BlockSpec((H, tq, 2 * LANE), lambda b, i: (0, q_off + b * n_q + i, 0))] + k_specs + v_specs
                + [pl.BlockSpec((tq, MLA_W), lambda b, i: (q_off + b * n_q + i, U_MLA_G // MLA_W))])
    args = [q] + [k] * n_src + [vt] * n_src + [u_all]
    return pl.pallas_call(
        functools.partial(_attn_kernel, n_src=n_src),
        grid=(BATCH, n_q),
        in_specs=in_specs,
        out_specs=pl.BlockSpec((tq, MLA_W), lambda b, i: (b * n_q + i, 0)),
        out_shape=jax.ShapeDtypeStruct((BATCH * n_q * tq, MLA_W), BF16),
        compiler_params=_cparams(("parallel", "parallel")),
        name="attn_latent" if latent else "attn_ctx",
    )(*args)


_POOL_PAD = 16


_POOL_HALO = 8


def _pool_kernel(x_ref, g_ref, w_ref, sc_ref, o_ref, a_ref, b_ref, *, L):
    zeros = jnp.zeros((_POOL_PAD, POOL_GROUP), F32)
    for ref in (a_ref, b_ref):
        ref[pl.ds(0, _POOL_PAD), :] = zeros
        ref[pl.ds(L + _POOL_PAD, _POOL_PAD), :] = zeros
    lo, n = _POOL_PAD - _POOL_HALO, L + 2 * _POOL_HALO
    halo_zeros = jnp.zeros((_POOL_HALO, POOL_GROUP), F32)
    t = lax.broadcasted_iota(jnp.int32, (L, 1), 0)
    for gi, win in enumerate(POOL_WINDOWS):
        half = win // 2
        levels = win.bit_length() - 1
        assert win == 1 << levels and 1 <= levels and half <= _POOL_HALO
        cols = slice(gi * POOL_GROUP, (gi + 1) * POOL_GROUP)
        x = x_ref[:, cols].astype(F32)
        a_ref[pl.ds(lo, _POOL_HALO), :] = halo_zeros
        a_ref[pl.ds(L + _POOL_PAD, _POOL_HALO), :] = halo_zeros
        a_ref[pl.ds(_POOL_PAD, L), :] = x
        src, dst = a_ref, b_ref
        for m in range(1, levels + 1):
            d0, d1 = (-1, 0) if m == 1 else (-(1 << (m - 2)), 1 << (m - 2))
            dst[pl.ds(lo, n), :] = src[pl.ds(lo + d0, n), :] + src[pl.ds(lo + d1, n), :]
            src, dst = dst, src
        acc = src[pl.ds(_POOL_PAD, L), :]
        cnt = (jnp.minimum(t + half, L) - jnp.maximum(t - half, 0)).astype(F32)
        dlt = (acc / cnt - x).astype(BF16)
        y = _dot(dlt, w_ref[gi]) * sc_ref[:, cols]
        o_ref[:, cols] = (y * _silu(g_ref[:, cols].astype(F32))).astype(o_ref.dtype)


def _pool(u_all, w_pool, pool_scale, l, *, L, row_blk0):
    return pl.pallas_call(
        functools.partial(_pool_kernel, L=L),
        grid=(BATCH,),
        in_specs=[
            pl.BlockSpec((L, POOL_W), lambda b: (row_blk0 + b, U_POOL // POOL_W)),
            pl.BlockSpec((L, POOL_W), lambda b: (row_blk0 + b, U_POOL_G // POOL_W)),
            _layer_spec(w_pool, l),
            _layer_spec(pool_scale, l),
        ],
        out_specs=pl.BlockSpec((L, POOL_W), lambda b: (b, 0)),
        out_shape=jax.ShapeDtypeStruct((BATCH * L, POOL_W), BF16),
        scratch_shapes=[pltpu.VMEM((L + 2 * _POOL_PAD, POOL_GROUP), F32),
                        pltpu.VMEM((L + 2 * _POOL_PAD, POOL_GROUP), F32)],
        compiler_params=_cparams(("parallel",)),
        name=f"pool_{L}",
    )(u_all, u_all, w_pool, pool_scale)


def _outproj_kernel(*refs, n_src, n_lat_tiles, n_streams, final, sub):
    x_refs = refs[:n_src]
    refs = refs[n_src:]
    mixers = [refs[3 * s:3 * s + 3] for s in range(n_streams)]
    w_ref, gt_ref, fg_ref, o_ref = refs[3 * n_streams:]

    def run(x_ref, hy_ref, at_ref, po_ref):
        for r in range(0, x_ref.shape[0], sub):
            rows = pl.ds(r, sub)
            acc = (_dot(hy_ref[rows, :], w_ref[0:HY_W, :]) + _dot(at_ref[rows, :], w_ref[HY_W:HY_W + MLA_W, :])
                   + _dot(po_ref[rows, :], w_ref[HY_W + MLA_W:, :]))
            y = x_ref[rows, :] + gt_ref[...] * acc
            if final:
                y = y * lax.rsqrt(jnp.mean(y * y, axis=-1, keepdims=True) + EPS) * fg_ref[...]
            o_ref[rows, :] = y

    if n_streams == 1:
        run(x_refs[0], *mixers[0])
    else:
        is_latent = pl.program_id(0) < n_lat_tiles
        pl.when(is_latent)(lambda: run(x_refs[0], *mixers[0]))
        pl.when(jnp.logical_not(is_latent))(lambda: run(x_refs[-1], *mixers[1]))


def _outproj(xs, mixers, w_out, mod, fg, l, *, final, tm=512):
    d = xs[0].shape[1]
    tiles_per_batch = SEQ // tm
    n_lat_tiles = ROWS_L // tm
    n_row_tiles = sum(m[0].shape[0] for m in mixers) // tm

    def mod_row(i):
        return jnp.minimum(i // tiles_per_batch, BATCH)

    lat_blk = lambda i: (jnp.minimum(i, n_lat_tiles - 1), 0)
    ctx_blk = lambda i: (jnp.maximum(i - n_lat_tiles, 0), 0)
    mixer_specs, mixer_args = [], []
    for blk, (hy, att, po) in zip((lat_blk, ctx_blk), mixers):
        mixer_specs += [pl.BlockSpec((tm, HY_W), blk), pl.BlockSpec((tm, MLA_W), blk), pl.BlockSpec((tm, POOL_W), blk)]
        mixer_args += [hy, att, po]

    return pl.pallas_call(
        functools.partial(_outproj_kernel, n_src=len(xs), n_lat_tiles=n_lat_tiles, n_streams=len(mixers),
                          final=final, sub=256),
        grid=(n_row_tiles,),
        in_specs=_row_source_specs(xs, tm) + mixer_specs + [
            pl.BlockSpec((None,) + w_out.shape[1:], lambda i: (l, 0, 0), pipeline_mode=pl.Buffered(1)),
            pl.BlockSpec((None, None, 1, d), lambda i: (l, mod_row(i), 0, 2)),
            pl.BlockSpec((1, d), lambda i: (0, 0)),
        ],
        out_specs=pl.BlockSpec((tm, d), lambda i: (i, 0)),
        out_shape=jax.ShapeDtypeStruct((n_row_tiles * tm, d), F32),
        compiler_params=_cparams(("parallel",)),
        name="outproj_final" if final else "outproj",
    )(*xs, *mixer_args, w_out, mod, fg)


def _rope_lane_tables(ident_rows):
    n_rows = SEQ // GRID_W
    row = np.repeat(np.arange(n_rows, dtype=np.float64), GRID_W)
    col = np.tile(np.arange(GRID_W, dtype=np.float64), n_rows)
    n_freq = MLA_DR // 4
    inv = ROPE_BASE ** (-np.arange(n_freq, dtype=np.float64) / n_freq)
    ang = np.concatenate([row[:, None] * inv, col[:, None] * inv], axis=-1)
    cos, sin = np.cos(ang), np.sin(ang)
    cos_t = np.concatenate([cos, cos, cos, cos], axis=-1)
    sin_t = np.concatenate([-sin, sin, -sin, sin], axis=-1)
    cos_t = np.concatenate([cos_t, np.ones((ident_rows, LANE))], axis=0)
    sin_t = np.concatenate([sin_t, np.zeros((ident_rows, LANE))], axis=0)
    return jnp.asarray(cos_t, F32), jnp.asarray(sin_t, F32)


def _dft_tables(P, dtype):
    idx = np.arange(P, dtype=np.int64)
    ang = ((idx[:, None] * idx[None, :]) % (2 * P)).astype(np.float64) * (math.pi / P)
    return jnp.asarray(np.cos(ang), F32).astype(dtype), jnp.asarray(np.sin(ang), F32).astype(dtype)


def _filter_tables(L):
    t = np.linspace(0.0, 1.0, L)[:, None]
    wpos = (2.0 * math.pi / L) * np.arange(L, dtype=np.float64)[:, None]
    bands = np.linspace(1e-4, FILTER_BANDS - 1, FILTER_BANDS)[None, :]
    feats = np.concatenate([t, np.cos(bands * wpos), -np.sin(bands * wpos)], axis=-1)
    feats = np.pad(feats, ((0, 0), (0, LANE - FILTER_EMB)))
    return jnp.asarray(feats, F32), jnp.asarray(t, F32)


def _decay_rates():
    d = np.abs(np.linspace(math.log(DECAY_TARGET) / SLOW_DECAY, math.log(DECAY_TARGET) / FAST_DECAY, HY_W))
    return jnp.asarray(d[None, :], F32)


def _kr_permutation():
    p = np.zeros((2 * LANE, MLA_DR), np.float32)
    half = MLA_DR // 2
    for grp, odd in enumerate((0, 1, 0, 1, 1, 0, 1, 0)):
        for i in range(half):
            p[grp * half + i, 2 * i + odd] = 1.0
    return jnp.asarray(p, BF16)


_PACK_BLK = U_W // 4
_PACK_KR_STEP = U_KR1 // _PACK_BLK
_PACK_KR_ROW = U_KR1 - _PACK_KR_STEP * _PACK_BLK
_PACK_SHIFT = U_MLA_G - R_OFF_MLA_G


def _pack_w_in_kernel(a3_ref, p_ref, o_ref):
    i = pl.program_id(1)
    a_ref = a3_ref.at[0]

    @pl.when(i != _PACK_KR_STEP)
    def _():
        o_ref[...] = a_ref[...].astype(BF16)

    @pl.when(i == _PACK_KR_STEP)
    def _():
        r0, r1 = _PACK_KR_ROW, _PACK_KR_ROW + 2 * LANE
        o_ref[0:r0, :] = a_ref[0:r0, :].astype(BF16)
        o_ref[r0:r1, :] = _dot(p_ref[...], a_ref[r0:r0 + MLA_DR, :].astype(BF16)).astype(BF16)
        o_ref[r1:, :] = a_ref[r1 - _PACK_SHIFT:_PACK_BLK - _PACK_SHIFT, :].astype(BF16)


def _pack_w_in(w_in_t):
    depth, n, d = w_in_t.shape
    assert U_KR1 == R_OFF_KR and U_MLA_G == U_KR1 + 2 * LANE and n == U_W - _PACK_SHIFT
    perm = _kr_permutation()

    def src_row(l, i):
        row = jnp.where(i <= _PACK_KR_STEP, i * _PACK_BLK, i * _PACK_BLK - _PACK_SHIFT)
        return (l, pl.multiple_of(row, MLA_DR), 0)

    return pl.pallas_call(
        _pack_w_in_kernel,
        grid=(depth, U_W // _PACK_BLK),
        in_specs=[pl.BlockSpec((pl.Element(1), pl.Element(_PACK_BLK), pl.Element(d)), src_row),
                  pl.BlockSpec(perm.shape, lambda l, i: (0, 0))],
        out_specs=pl.BlockSpec((None, _PACK_BLK, d), lambda l, i: (l, i, 0)),
        out_shape=jax.ShapeDtypeStruct((depth, U_W, d), BF16),
        compiler_params=_cparams(("parallel", "parallel")),
        name="pack_w_in",
    )(w_in_t, perm)


def _pack_w_uq(w_uq):
    w = w_uq.reshape(DEPTH, Q_RANK, MLA_HEADS, MLA_DN + MLA_DR)
    nope = w[..., :MLA_DN].reshape(DEPTH, Q_RANK, MLA_W)
    a, b = w[..., MLA_DN::2], w[..., MLA_DN + 1::2]
    rot = jnp.concatenate([a, b], axis=-1).reshape(DEPTH, Q_RANK, MLA_HEADS * MLA_DR)
    swp = jnp.concatenate([b, a], axis=-1).reshape(DEPTH, Q_RANK, MLA_HEADS * MLA_DR)
    return jnp.concatenate([nope, rot, swp], axis=-1).astype(BF16)


def _pack_w_ukv(w_ukv):
    w = w_ukv.reshape(DEPTH, KV_RANK, MLA_HEADS, MLA_DN + MLA_DV)
    wk = w[..., :MLA_DN].reshape(DEPTH, KV_RANK, MLA_W)
    wv = w[..., MLA_DN:].reshape(DEPTH, KV_RANK, MLA_W)
    return wk.astype(BF16), jnp.swapaxes(wv, 1, 2).astype(BF16)


def kernel(x, c, ctx, c_ctx, norm_g, w_ada, b_ada, w_in, hy_conv_w, hy_conv_b, hf_w1, hf_b1, hf_freq,
           hf_w2, hf_b2, hf_w3, hy_bias, q_norm_g, w_uq, kv_norm_g, w_ukv, w_pool, pool_scale, w_out,
           final_norm_g):
    assert x.shape == (BATCH, SEQ, D_MODEL) and ctx.shape == (BATCH, CTX_LEN, D_MODEL)

    qkv_tr = 512
    cos_t, sin_t = _rope_lane_tables(qkv_tr)

    w_in_p = _pack_w_in(jnp.swapaxes(w_in, 1, 2))
    w_q_p = _pack_w_uq(w_uq)
    w_k_p, w_vt_p = _pack_w_ukv(w_ukv)
    w_out_b = w_out.astype(BF16)
    w_pool_b = w_pool.astype(BF16)
    w1_p = jnp.pad(hf_w1, ((0, 0), (0, LANE - FILTER_EMB), (0, 0)))
    w3_p = hf_w3.reshape(DEPTH, FILTER_HIDDEN, 2, 2, HY_W).transpose(0, 2, 3, 1, 4)
    conv_w = hy_conv_w.reshape(DEPTH, 3, 3, HY_W)
    conv_b = hy_conv_b.reshape(DEPTH, 3, HY_W)
    rows = lambda a: a[:, None, :]
    norm_g3, qg3, kg3, ps3 = rows(norm_g), rows(q_norm_g), rows(kv_norm_g), rows(pool_scale)
    b1_3, fr_3, b2_3 = rows(hf_b1), rows(hf_freq), rows(hf_b2)

    cond = jnp.concatenate([c, c_ctx[None], jnp.zeros((8 - BATCH - 1, D_MODEL), F32)], axis=0)
    mod = _adaln(cond, w_ada, b_ada[:, None, :])
    mod = mod.reshape(DEPTH, 8, 1, 3 * D_MODEL)

    xs = [x.reshape(ROWS_L, D_MODEL), ctx.reshape(ROWS_C, D_MODEL)]

    for l in range(DEPTH):
        last = l == DEPTH - 1
        u_all = _inproj(xs, norm_g3, mod, w_in_p, l, ctx_keys_only=last)
        q, k, vt = _qkv(u_all, qg3, kg3, w_q_p, w_k_p, w_vt_p, cos_t, sin_t, l, tr=qkv_tr)
        mixers = []
        for L, row_blk0, latent in ((SEQ, 0, True), (CTX_LEN, ROWS_L // CTX_LEN, False)):
            if last and not latent:
                continue
            g, gn = _hyena_filter(L, w1_p, b1_3, fr_3, hf_w2, b2_3, w3_p, l)
            hy = _hyena(u_all, conv_w, conv_b, hy_bias, g, gn, l, L=L, row_blk0=row_blk0)
            att = _attention(q, k, vt, u_all, latent=latent, tq=256)
            po = _pool(u_all, w_pool_b, ps3, l, L=L, row_blk0=row_blk0)
            mixers.append((hy, att, po))
        xs = [_outproj(xs, mixers, w_out_b, mod, final_norm_g[None, :], l, final=last)]
    return xs[0].reshape(BATCH, SEQ, D_MODEL)
```
